```python
import jax
import jax.numpy as jnp
from jax import lax
import numpy as np

D_MODEL = 1024
BATCH = 8
SEQ = 4096
DEPTH = 2

CTX_LEN = 256
GRID_W = 64
M_WIDTH = D_MODEL // 2
M_HEAD_DIM = 128
M_HEADS = M_WIDTH // M_HEAD_DIM
N_WIDTH = D_MODEL - M_WIDTH
N_HEAD_DIM = 64
N_HEADS = N_WIDTH // N_HEAD_DIM
MIX_WIDTH = M_WIDTH + N_WIDTH
G_COLS = 4 * M_HEADS
Q_COLS = 2 * M_WIDTH + N_WIDTH
KV_COLS = 2 * M_WIDTH + G_COLS + 2 * N_WIDTH
N_IN = Q_COLS + KV_COLS
CONV_WIDTH = 3
CHUNK = 128
WIN_H = 8
WIN_W = 16
ROPE_AXIS_DIM = M_HEAD_DIM // 2
ROPE_BASE = 10000.0
N_EXPERTS = 16
N_GROUPS = 4
EXPERTS_PER_GROUP = N_EXPERTS // N_GROUPS
TOP_K = 2
MOE_D_FF = 512
NORM_EPS = 1e-6
F_BIAS_LO = 3.0
F_BIAS_HI = 6.0

kernel_name = "hybrid_mlstm_natten_grouped_moe_dit"


def rmsnorm(x, w):
    x32 = x.astype(jnp.float32)
    y = x32 * lax.rsqrt(jnp.mean(x32 * x32, axis=-1, keepdims=True) + NORM_EPS)
    return (y * w.astype(jnp.float32)).astype(x.dtype)


def modulate(h, shift, scale):
    return h * (1 + scale) + shift


def centred_dwconv(x, w, b):
    L = x.shape[1]
    pad = CONV_WIDTH // 2
    xp = jnp.pad(x, ((0, 0), (pad, pad), (0, 0)))
    y = b
    for j in range(CONV_WIDTH):
        y = y + xp[:, j:j + L] * w[j]
    return y


def axial_rope_tables(L):
    t = jnp.arange(L)
    row = (t // GRID_W).astype(jnp.float32)
    col = (t % GRID_W).astype(jnp.float32)
    inv = ROPE_BASE ** (-jnp.arange(0, ROPE_AXIS_DIM, 2, dtype=jnp.float32) / ROPE_AXIS_DIM)
    ang_r = row[:, None] * inv
    ang_c = col[:, None] * inv
    return (jnp.cos(ang_r)[:, None], jnp.sin(ang_r)[:, None],
            jnp.cos(ang_c)[:, None], jnp.sin(ang_c)[:, None])


def _rotate(x, cos, sin):
    x1, x2 = jnp.split(x, 2, axis=-1)
    cos = cos.astype(x.dtype)
    sin = sin.astype(x.dtype)
    return jnp.concatenate([x1 * cos - x2 * sin, x1 * sin + x2 * cos], axis=-1)


def axial_rope(x, tables):
    cr, sr, cc, sc = tables
    xr, xc = jnp.split(x, 2, axis=-1)
    return jnp.concatenate([_rotate(xr, cr, sr), _rotate(xc, cc, sc)], axis=-1)


def _to_chunks(a):
    B, L = a.shape[:2]
    a = a.reshape((B, L // CHUNK, CHUNK) + a.shape[2:])
    return jnp.moveaxis(jnp.moveaxis(a, 1, 0), 2, 3)


def mlstm_zero_state(B):
    f32 = jnp.float32
    return (jnp.zeros((B, M_HEADS, M_HEAD_DIM, M_HEAD_DIM), f32),
            jnp.zeros((B, M_HEADS, M_HEAD_DIM), f32),
            jnp.zeros((B, M_HEADS), f32))


def mlstm_scan(q, k, v, i_pre, f_pre, state):
    with_out = q is not None
    f32 = jnp.float32
    B, L = k.shape[:2]
    xs = [_to_chunks(a.astype(f32)) for a in (k, v, i_pre, f_pre)]
    if with_out:
        xs.append(_to_chunks(q.astype(f32)))
    causal = jnp.tril(jnp.ones((CHUNK, CHUNK), dtype=bool))

    def step(carry, inp):
        C, n, m = carry
        kc, vc, ic, fc = inp[:4]
        b = jnp.cumsum(jax.nn.log_sigmoid(fc), axis=-1)
        b_end = b[..., -1]
        g = b_end[..., None] - b + ic
        m_new = jnp.maximum(b_end + m, jnp.max(g, axis=-1))
        w_old = jnp.exp(b_end + m - m_new)
        w_tok = jnp.exp(g - m_new[..., None])
        C_new = w_old[..., None, None] * C + jnp.einsum('bhs,bhsd,bhse->bhde', w_tok, kc, vc)
        n_new = w_old[..., None] * n + jnp.einsum('bhs,bhsd->bhd', w_tok, kc)
        if not with_out:
            return (C_new, n_new, m_new), None
        qc = inp[4]
        logd = jnp.where(causal, b[..., :, None] - b[..., None, :] + ic[..., None, :], -jnp.inf)
        inter = b + m[..., None]
        m_row = jnp.maximum(inter, jnp.max(logd, axis=-1))
        dmat = jnp.exp(logd - m_row[..., None])
        w_inter = jnp.exp(inter - m_row)
        s = jnp.einsum('bhtd,bhsd->bhts', qc, kc) * dmat
        num = (jnp.einsum('bhts,bhse->bhte', s, vc)
               + w_inter[..., None] * jnp.einsum('bhtd,bhde->bhte', qc, C))
        qn = jnp.sum(s, axis=-1) + w_inter * jnp.einsum('bhtd,bhd->bht', qc, n)
        den = jnp.maximum(jnp.abs(qn), jnp.exp(-m_row))
        return (C_new, n_new, m_new), num / den[..., None]

    state, h = lax.scan(step, state, tuple(xs))
    if not with_out:
        return state, None
    h = jnp.moveaxis(jnp.moveaxis(h, 3, 2), 0, 1).reshape(B, L, M_HEADS, M_HEAD_DIM)
    return state, h.astype(v.dtype)


def _dir_gates(g, gate_b, d):
    g = g.astype(jnp.float32) + gate_b.astype(jnp.float32)
    i_pre = g[..., (2 * d) * M_HEADS:(2 * d + 1) * M_HEADS]
    f_pre = g[..., (2 * d + 1) * M_HEADS:(2 * d + 2) * M_HEADS]
    return i_pre, f_pre


def _ident(a):
    return a


def _rev(a):
    return jnp.flip(a, axis=1)


def mlstm_bidirectional(q_l, k_l, v_l, g_l, q_c, k_c, v_c, g_c, gate_b):
    B = k_l.shape[0]
    h_lat, h_ctx = [], []
    for d, rev in ((0, _ident), (1, _rev)):
        il, fl = _dir_gates(g_l, gate_b, d)
        ic, fc = _dir_gates(g_c, gate_b, d)
        st, hc = mlstm_scan(None if q_c is None else rev(q_c), rev(k_c), rev(v_c),
                            rev(ic), rev(fc), mlstm_zero_state(B))
        _, hl = mlstm_scan(rev(q_l), rev(k_l), rev(v_l), rev(il), rev(fl), st)
        h_lat.append(rev(hl))
        if q_c is not None:
            h_ctx.append(rev(hc))
    return h_lat[0] + h_lat[1], (h_ctx[0] + h_ctx[1] if q_c is not None else None)


def head_rmsnorm(h, w):
    B, L = h.shape[:2]
    return rmsnorm(h, w.reshape(M_HEADS, M_HEAD_DIM)).reshape(B, L, M_WIDTH)


def neighborhood_attention(q, k, v, k_ctx, v_ctx, rpb):
    B, L, H, hd = q.shape
    rows = L // GRID_W
    kh = min(WIN_H, rows)
    kw = WIN_W
    qg = (q * hd ** -0.5).reshape(B, rows, GRID_W, H, hd)
    kg = k.reshape(B, rows, GRID_W, H, hd)
    vg = v.reshape(B, rows, GRID_W, H, hd)
    cols = jnp.arange(GRID_W)
    col_idx = jnp.clip(cols - kw // 2, 0, GRID_W - kw)[:, None] + jnp.arange(kw)[None, :]
    dc = col_idx - cols[:, None] + (WIN_W - 1)

    def row_block(r):
        rs = jnp.clip(r - kh // 2, 0, rows - kh)
        kb = lax.dynamic_slice_in_dim(kg, rs, kh, axis=1)[:, :, col_idx]
        vb = lax.dynamic_slice_in_dim(vg, rs, kh, axis=1)[:, :, col_idx]
        qr = lax.dynamic_index_in_dim(qg, r, axis=1, keepdims=False)
        dr = rs + jnp.arange(kh) - r + (WIN_H - 1)
        bias = jnp.transpose(rpb[:, dr[:, None, None], dc[None]], (0, 2, 1, 3))
        s_win = (jnp.einsum('bwhd,biwjhd->bhwij', qr, kb) + bias).reshape(B, H, GRID_W, kh * kw)
        s_ctx = jnp.einsum('bwhd,bnhd->bhwn', qr, k_ctx)
        p = jax.nn.softmax(jnp.concatenate([s_win, s_ctx], axis=-1).astype(jnp.float32), axis=-1)
        p = p.astype(v.dtype)
        p_win = p[..., :kh * kw].reshape(B, H, GRID_W, kh, kw)
        return (jnp.einsum('bhwij,biwjhd->bwhd', p_win, vb)
                + jnp.einsum('bhwn,bnhd->bwhd', p[..., kh * kw:], v_ctx))

    out = lax.map(row_block, jnp.arange(rows))
    return jnp.moveaxis(out, 0, 1).reshape(B, L, H * hd)


def context_attention(q, k, v):
    B, N, H, hd = q.shape
    s = jnp.einsum('bnhd,bmhd->bhnm', q * hd ** -0.5, k)
    p = jax.nn.softmax(s.astype(jnp.float32), axis=-1).astype(v.dtype)
    return jnp.einsum('bhnm,bmhd->bnhd', p, v).reshape(B, N, H * hd)


def _q_side(p):
    return jnp.split(p, [M_WIDTH, 2 * M_WIDTH], axis=-1)


def _kv_side(p):
    return jnp.split(p, [M_WIDTH, 2 * M_WIDTH, 2 * M_WIDTH + G_COLS,
                         2 * M_WIDTH + G_COLS + N_WIDTH], axis=-1)


def _m_heads(a):
    return a.reshape(a.shape[:2] + (M_HEADS, M_HEAD_DIM))


def _n_heads(a):
    return a.reshape(a.shape[:2] + (N_HEADS, N_HEAD_DIM))


def hybrid_mixer(hx, hc, rope, w_in, conv_w, conv_b, gate_b, mnorm_w, rpb, w_out, with_ctx_out):
    px = hx @ w_in
    pc = hc @ (w_in if with_ctx_out else w_in[:, Q_COLS:])
    mq_l, mo_l, nq_l = _q_side(px[..., :Q_COLS])
    mk_l, mv_l, g_l, nk_l, nv_l = _kv_side(px[..., Q_COLS:])
    if with_ctx_out:
        mq_c, mo_c, nq_c = _q_side(pc[..., :Q_COLS])
        mk_c, mv_c, g_c, nk_c, nv_c = _kv_side(pc[..., Q_COLS:])
    else:
        mk_c, mv_c, g_c, nk_c, nv_c = _kv_side(pc)
    cw_q, cw_k = conv_w[:, :M_WIDTH], conv_w[:, M_WIDTH:]
    cb_q, cb_k = conv_b[:M_WIDTH], conv_b[M_WIDTH:]
    k_scale = M_HEAD_DIM ** -0.5
    q_l = axial_rope(_m_heads(jax.nn.silu(centred_dwconv(mq_l, cw_q, cb_q))), rope)
    k_l = axial_rope(_m_heads(jax.nn.silu(centred_dwconv(mk_l, cw_k, cb_k))), rope) * k_scale
    k_c = _m_heads(jax.nn.silu(centred_dwconv(mk_c, cw_k, cb_k))) * k_scale
    q_c = _m_heads(jax.nn.silu(centred_dwconv(mq_c, cw_q, cb_q))) if with_ctx_out else None
    h_l, h_c = mlstm_bidirectional(q_l, k_l, _m_heads(mv_l), g_l,
                                   q_c, k_c, _m_heads(mv_c), g_c, gate_b)
    m_out_l = head_rmsnorm(h_l, mnorm_w) * jax.nn.sigmoid(mo_l)
    nk_c, nv_c = _n_heads(nk_c), _n_heads(nv_c)
    n_out_l = neighborhood_attention(_n_heads(nq_l), _n_heads(nk_l), _n_heads(nv_l), nk_c, nv_c, rpb)
    y_l = jnp.concatenate([m_out_l, n_out_l], axis=-1) @ w_out
    if not with_ctx_out:
        return y_l, None
    m_out_c = head_rmsnorm(h_c, mnorm_w) * jax.nn.sigmoid(mo_c)
    n_out_c = context_attention(_n_heads(nq_c), nk_c, nv_c)
    y_c = jnp.concatenate([m_out_c, n_out_c], axis=-1) @ w_out
    return y_l, y_c


def grouped_moe(h, router_w, router_b, w1, w3, w2):
    shape = h.shape
    t = h.reshape(-1, shape[-1])
    scores = jax.nn.softmax((t @ router_w).astype(jnp.float32), axis=-1)
    sel = scores + router_b.astype(jnp.float32)
    gscore = jnp.sum(lax.top_k(sel.reshape(-1, N_GROUPS, EXPERTS_PER_GROUP), TOP_K)[0], axis=-1)
    best = jnp.argmax(gscore, axis=-1)
    in_grp = (jnp.arange(N_EXPERTS) // EXPERTS_PER_GROUP)[None, :] == best[:, None]
    _, idx = lax.top_k(jnp.where(in_grp, sel, -jnp.inf), TOP_K)
    wts = jnp.take_along_axis(scores, idx, axis=-1)
    wts = wts / jnp.sum(wts, axis=-1, keepdims=True)
    gate = jnp.sum(jax.nn.one_hot(idx, N_EXPERTS, dtype=jnp.float32) * wts[..., None], axis=1)
    gate = gate.astype(t.dtype)
    y = jnp.zeros_like(t)
    for e in range(N_EXPERTS):
        a = jax.nn.silu(t @ w1[e]) * (t @ w3[e])
        y = y + gate[:, e:e + 1] * (a @ w2[e])
    return y.reshape(shape)


def setup_inputs(seed: int = 0) -> dict:
    key = jax.random.key(seed)
    ks = jax.random.split(key, 24)
    f32 = jnp.float32

    def nrm(k, shape, s):
        return jax.random.normal(k, shape, f32) * s

    gate_base = jnp.concatenate([jnp.zeros((M_HEADS,), f32), jnp.linspace(F_BIAS_LO, F_BIAS_HI, M_HEADS),
                                 jnp.zeros((M_HEADS,), f32), jnp.linspace(F_BIAS_LO, F_BIAS_HI, M_HEADS)])
    return {
        "x": nrm(ks[0], (BATCH, SEQ, D_MODEL), 1.0),
        "c": nrm(ks[1], (BATCH, D_MODEL), 1.0),
        "ctx": nrm(ks[2], (BATCH, CTX_LEN, D_MODEL), 1.0),
        "c_ctx": nrm(ks[3], (D_MODEL,), 1.0),
        "ada_w": nrm(ks[4], (DEPTH, D_MODEL, 6 * D_MODEL), 0.5 * D_MODEL ** -0.5),
        "ada_b": nrm(ks[5], (DEPTH, 6 * D_MODEL), 0.02),
        "norm1_w": 1.0 + nrm(ks[6], (DEPTH, D_MODEL), 0.02),
        "w_in": nrm(ks[7], (DEPTH, D_MODEL, N_IN), D_MODEL ** -0.5),
        "conv_w": nrm(ks[8], (DEPTH, CONV_WIDTH, 2 * M_WIDTH), CONV_WIDTH ** -0.5),
        "conv_b": nrm(ks[9], (DEPTH, 2 * M_WIDTH), 0.02),
        "gate_b": gate_base[None, :] + nrm(ks[10], (DEPTH, G_COLS), 0.1),
        "mnorm_w": 1.0 + nrm(ks[11], (DEPTH, M_WIDTH), 0.02),
        "rpb": nrm(ks[12], (DEPTH, N_HEADS, 2 * WIN_H - 1, 2 * WIN_W - 1), 0.1),
        "w_out": nrm(ks[13], (DEPTH, MIX_WIDTH, D_MODEL), MIX_WIDTH ** -0.5),
        "norm2_w": 1.0 + nrm(ks[14], (DEPTH, D_MODEL), 0.02),
        "router_w": nrm(ks[15], (D_MODEL, N_EXPERTS), D_MODEL ** -0.5),
        "router_b": nrm(ks[16], (N_EXPERTS,), 0.01),
        "exp_w1": nrm(ks[17], (DEPTH, N_EXPERTS, D_MODEL, MOE_D_FF), D_MODEL ** -0.5),
        "exp_w3": nrm(ks[18], (DEPTH, N_EXPERTS, D_MODEL, MOE_D_FF), D_MODEL ** -0.5),
        "exp_w2": nrm(ks[19], (DEPTH, N_EXPERTS, MOE_D_FF, D_MODEL), MOE_D_FF ** -0.5),
        "final_norm_w": 1.0 + nrm(ks[20], (D_MODEL,), 0.02),
    }


def reference(x, c, ctx, c_ctx, ada_w, ada_b, norm1_w, w_in, conv_w, conv_b, gate_b, mnorm_w, rpb,
              w_out, norm2_w, router_w, router_b, exp_w1, exp_w3, exp_w2, final_norm_w):
    L = x.shape[1]
    rope = axial_rope_tables(L)
    s_lat = jax.nn.silu(c)
    s_ctx = jax.nn.silu(c_ctx)
    for layer in range(DEPTH):
        last = layer == DEPTH - 1
        mod_l = jnp.split((s_lat @ ada_w[layer] + ada_b[layer])[:, None, :], 6, axis=-1)
        mod_c = jnp.split(s_ctx @ ada_w[layer] + ada_b[layer], 6, axis=-1)
        hx = modulate(rmsnorm(x, norm1_w[layer]), mod_l[0], mod_l[1])
        hc = modulate(rmsnorm(ctx, norm1_w[layer]), mod_c[0], mod_c[1])
        y_l, y_c = hybrid_mixer(hx, hc, rope, w_in[layer], conv_w[layer], conv_b[layer], gate_b[layer],
                                mnorm_w[layer], rpb[layer], w_out[layer], not last)
        x = x + mod_l[2] * y_l
        hx = modulate(rmsnorm(x, norm2_w[layer]), mod_l[3], mod_l[4])
        if last:
            x = x + mod_l[5] * grouped_moe(hx, router_w, router_b,
                                           exp_w1[layer], exp_w3[layer], exp_w2[layer])
        else:
            ctx = ctx + mod_c[2] * y_c
            hc = modulate(rmsnorm(ctx, norm2_w[layer]), mod_c[3], mod_c[4])
            h_all = grouped_moe(jnp.concatenate([hx, hc], axis=1), router_w, router_b,
                                exp_w1[layer], exp_w3[layer], exp_w2[layer])
            x = x + mod_l[5] * h_all[:, :L]
            ctx = ctx + mod_c[5] * h_all[:, L:]
    return rmsnorm(x, final_norm_w)
```

```python
import functools

import numpy as np
import jax
import jax.numpy as jnp
from jax import lax
from jax.experimental import pallas as pl
from jax.experimental.pallas import tpu as pltpu

D_MODEL = 1024
M_WIDTH = 512
M_HEADS = 4
M_HEAD_DIM = 128
N_WIDTH = 512
N_HEADS = 8
N_HEAD_DIM = 64
G_COLS = 16
GRID_W = 64
WIN_H = 8
WIN_W = 16
CHUNK = 128
ROPE_AXIS_DIM = 64
ROPE_BASE = 10000.0
N_EXPERTS = 16
N_GROUPS = 4
EXPERTS_PER_GROUP = 4
MOE_D_FF = 512
NORM_EPS = 1e-6

LANES = 128
TOK_TILE = 256
HALO = 16
NA_QROWS = 2
NA_WROWS = 10
NA_WKEYS = NA_WROWS * GRID_W
NEG = -1e30
G_PAD = LANES
N_PROJ = 7 * M_WIDTH + G_PAD
VMEM_LIMIT = 56 * 1024 * 1024

F32 = jnp.float32
BF16 = jnp.bfloat16


def _cparams(sem):
    return pltpu.CompilerParams(dimension_semantics=sem, vmem_limit_bytes=VMEM_LIMIT)


def _split_bf16(a):
    hi = a.astype(BF16)
    lo = (a - hi.astype(F32)).astype(BF16)
    return hi, lo


def _dot3(a, b):
    ah, al = _split_bf16(a)
    bh, bl = _split_bf16(b)
    d = functools.partial(jnp.dot, preferred_element_type=F32)
    return d(ah, bh) + (d(al, bh) + d(ah, bl))


def _dot_nt(a, b):
    return lax.dot_general(a, b, (((1,), (1,)), ((), ())), preferred_element_type=F32)


def _rms(x, w):
    ms = jnp.mean(x * x, axis=-1, keepdims=True)
    return x * lax.rsqrt(ms + NORM_EPS) * w


def _adaln_kernel(c_ref, w_ref, b_ref, o_ref):
    c = c_ref[...]
    s = c * jax.nn.sigmoid(c)
    o_ref[0] = _dot3(s, w_ref[0]) + b_ref[0]


def _adaln(cvec, ada_w, ada_b):
    depth = ada_w.shape[0]
    n = ada_w.shape[2]
    tn = D_MODEL
    return pl.pallas_call(
        _adaln_kernel,
        grid=(depth, n // tn),
        in_specs=[pl.BlockSpec((16, D_MODEL), lambda l, j: (0, 0)),
                  pl.BlockSpec((1, D_MODEL, tn), lambda l, j: (l, 0, j)),
                  pl.BlockSpec((1, 1, tn), lambda l, j: (l, 0, j))],
        out_specs=pl.BlockSpec((1, 16, tn), lambda l, j: (l, 0, j)),
        out_shape=jax.ShapeDtypeStruct((depth, 16, n), F32),
        compiler_params=_cparams(("arbitrary", "arbitrary")),
        name="adaln",
    )(cvec, ada_w, ada_b.reshape(depth, 1, n))


def _inproj_kernel(*refs, split_input):
    if split_input:
        x_ref, ctx_ref, mod_ref, nw_ref, w_ref = refs[:5]
        outs = refs[5:]
        xt = jnp.where(pl.program_id(1) == 0, ctx_ref[0], x_ref[0])
    else:
        x_ref, mod_ref, nw_ref, w_ref = refs[:4]
        outs = refs[4:]
        xt = x_ref[0]
    h = _rms(xt, nw_ref[...]) * (1.0 + mod_ref[0, 0, 1:2, :]) + mod_ref[0, 0, 0:1, :]
    hb = h.astype(BF16)
    for k in range(7):
        outs[k][0] = jnp.dot(hb, w_ref[:, k * M_WIDTH:(k + 1) * M_WIDTH],
                             preferred_element_type=F32).astype(BF16)
    outs[7][0] = jnp.dot(hb, w_ref[:, 7 * M_WIDTH:], preferred_element_type=F32)


def _stream_specs(split_input):
    tm = TOK_TILE
    if split_input:
        return [pl.BlockSpec((1, tm, D_MODEL), lambda b, j: (b, jnp.maximum(j - 1, 0), 0)),
                pl.BlockSpec((1, tm, D_MODEL), lambda b, j: (b, 0, 0))]
    return [pl.BlockSpec((1, tm, D_MODEL), lambda b, j: (b, j, 0))]


def _mod_spec():
    return pl.BlockSpec((1, 1, 6, D_MODEL), lambda b, j: (b, jnp.minimum(j, 1), 0, 0))


def _inproj(stream, mod, norm_w, w_proj, B, S):
    split_input = len(stream) == 2
    tm = TOK_TILE
    tok = lambda w: pl.BlockSpec((1, tm, w), lambda b, j: (b, j, 0))
    out_shape = [jax.ShapeDtypeStruct((B, S, M_WIDTH), BF16)] * 7 + [jax.ShapeDtypeStruct((B, S, G_PAD), F32)]
    return pl.pallas_call(
        functools.partial(_inproj_kernel, split_input=split_input),
        grid=(B, S // tm),
        in_specs=_stream_specs(split_input) + [
            _mod_spec(),
            pl.BlockSpec((1, D_MODEL), lambda b, j: (0, 0)),
            pl.BlockSpec((D_MODEL, N_PROJ), lambda b, j: (0, 0))],
        out_specs=[tok(M_WIDTH)] * 7 + [tok(G_PAD)],
        out_shape=out_shape,
        compiler_params=_cparams(("arbitrary", "arbitrary")),
        name="inproj",
    )(*stream, mod, norm_w.reshape(1, D_MODEL), w_proj)


def _prep_kernel(q_ref, qp_ref, qn_ref, k_ref, kp_ref, kn_ref, cw_ref, cb_ref, cos_ref, sin_ref,
                 qo_ref, ko_ref, *, n_tiles):
    j = pl.program_id(1)
    tp = TOK_TILE
    has_prev = jnp.logical_and(j != 0, j != 1).astype(F32)
    has_next = jnp.logical_and(j != 0, j != n_tiles - 1).astype(F32)
    rows = lax.broadcasted_iota(jnp.int32, (tp, M_WIDTH), 0)
    lanes = lax.broadcasted_iota(jnp.int32, (tp, M_WIDTH), 1)
    low_half = (lanes % (ROPE_AXIS_DIM)) < (ROPE_AXIS_DIM // 2)
    cosv = jnp.concatenate([cos_ref[...]] * M_HEADS, axis=1)
    sinv = jnp.concatenate([sin_ref[...]] * M_HEADS, axis=1)

    def branch(x_ref, p_ref, n_ref, col0):
        x = x_ref[0].astype(F32)
        prev_row = p_ref[0, HALO - 1:HALO, :].astype(F32) * has_prev
        next_row = n_ref[0, 0:1, :].astype(F32) * has_next
        xm = jnp.where(rows == 0, prev_row, pltpu.roll(x, 1, 0))
        xp = jnp.where(rows == tp - 1, next_row, pltpu.roll(x, tp - 1, 0))
        w = cw_ref[:, col0:col0 + M_WIDTH]
        y = cb_ref[:, col0:col0 + M_WIDTH] + xm * w[0:1] + x * w[1:2] + xp * w[2:3]
        y = y * jax.nn.sigmoid(y)
        half = ROPE_AXIS_DIM // 2
        partner = jnp.where(low_half, pltpu.roll(y, M_WIDTH - half, 1), pltpu.roll(y, half, 1))
        return y * cosv + partner * sinv

    qo_ref[0] = branch(q_ref, qp_ref, qn_ref, 0).astype(BF16)
    ko_ref[0] = (branch(k_ref, kp_ref, kn_ref, M_WIDTH) * (M_HEAD_DIM ** -0.5)).astype(BF16)


def _prep(mq, mk, conv_w, conv_b, cos_t, sin_t, B, S):
    tp = TOK_TILE
    n_tiles = S // tp
    per = tp // HALO
    n_halo = S // HALO
    main = pl.BlockSpec((1, tp, M_WIDTH), lambda b, j: (b, j, 0))
    prev = pl.BlockSpec((1, HALO, M_WIDTH), lambda b, j: (b, jnp.maximum(j * per - 1, 0), 0))
    nxt = pl.BlockSpec((1, HALO, M_WIDTH), lambda b, j: (b, jnp.minimum((j + 1) * per, n_halo - 1), 0))
    return pl.pallas_call(
        functools.partial(_prep_kernel, n_tiles=n_tiles),
        grid=(B, n_tiles),
        in_specs=[main, prev, nxt, main, prev, nxt,
                  pl.BlockSpec((3, 2 * M_WIDTH), lambda b, j: (0, 0)),
                  pl.BlockSpec((1, 2 * M_WIDTH), lambda b, j: (0, 0)),
                  pl.BlockSpec((tp, M_HEAD_DIM), lambda b, j: (j, 0)),
                  pl.BlockSpec((tp, M_HEAD_DIM), lambda b, j: (j, 0))],
        out_specs=[main, main],
        out_shape=[jax.ShapeDtypeStruct((B, S, M_WIDTH), BF16)] * 2,
        compiler_params=_cparams(("arbitrary", "arbitrary")),
        name="mlstm_prep",
    )(mq, mq, mq, mk, mk, mk, conv_w, conv_b.reshape(1, -1), cos_t, sin_t)


def _cumsum_rows(x, reverse):
    n = x.shape[0]
    rows = lax.broadcasted_iota(jnp.int32, x.shape, 0)
    sh = 1
    while sh < n:
        if reverse:
            x = x + jnp.where(rows < n - sh, pltpu.roll(x, n - sh, 0), 0.0)
        else:
            x = x + jnp.where(rows >= sh, pltpu.roll(x, sh, 0), 0.0)
        sh *= 2
    return x


def _log_sigmoid(x):
    return jnp.minimum(x, 0.0) - jnp.log(1.0 + jnp.exp(-jnp.abs(x)))


def _mlstm_kernel(qf_ref, kf_ref, vf_ref, gf_ref, qb_ref, kb_ref, vb_ref, gb_ref, gbias_ref,
                  hf_ref, hb_ref, cn_ref, m_ref):
    c = pl.program_id(1)

    @pl.when(c == 0)
    def _():
        cn_ref[...] = jnp.zeros_like(cn_ref)
        m_ref[...] = jnp.zeros_like(m_ref)

    t = CHUNK
    r_i = lax.broadcasted_iota(jnp.int32, (t, t), 0)
    c_i = lax.broadcasted_iota(jnp.int32, (t, t), 1)
    one_col = (lax.broadcasted_iota(jnp.int32, (t, LANES), 1) == 0).astype(BF16)

    for d, (q_ref, k_ref, v_ref, g_ref, h_ref) in enumerate(
            ((qf_ref, kf_ref, vf_ref, gf_ref, hf_ref), (qb_ref, kb_ref, vb_ref, gb_ref, hb_ref))):
        reverse = d == 1
        mask = (c_i >= r_i) if reverse else (c_i <= r_i)
        g = g_ref[0] + gbias_ref[...]
        bc = _cumsum_rows(_log_sigmoid(g), reverse)
        bt = bc.T
        gt = g.T
        end_row = 0 if reverse else t - 1
        for h in range(M_HEADS):
            icol = 2 * M_HEADS * d + h
            fcol = icol + M_HEADS
            hs = slice(h * M_HEAD_DIM, (h + 1) * M_HEAD_DIM)
            b_col = bc[:, fcol:fcol + 1]
            b_row = bt[fcol:fcol + 1, :]
            i_col = g[:, icol:icol + 1]
            i_row = gt[icol:icol + 1, :]
            b_end = bc[end_row:end_row + 1, fcol:fcol + 1]
            m_old = m_ref[d, h][0:1, 0:1]

            q = q_ref[0, :, hs]
            k = k_ref[0, :, hs]
            v_ext = jnp.concatenate([v_ref[0, :, hs], one_col], axis=1)
            cn_old = cn_ref[d, h]

            logd = jnp.where(mask, b_col + (i_row - b_row), NEG)
            m_row = jnp.maximum(b_col + m_old, jnp.max(logd, axis=-1, keepdims=True))
            dmat = jnp.exp(logd - m_row)
            w_inter = jnp.exp(b_col + m_old - m_row)
            s = (_dot_nt(q, k) * dmat).astype(BF16)
            ext = (jnp.dot(s, v_ext, preferred_element_type=F32)
                   + w_inter * jnp.dot(q, cn_old.astype(BF16), preferred_element_type=F32))
            qn = ext[:, M_HEAD_DIM:M_HEAD_DIM + 1]
            den = jnp.maximum(jnp.abs(qn), jnp.exp(-m_row))
            h_ref[0, :, hs] = (ext[:, :M_HEAD_DIM] / den).astype(BF16)

            g_col = b_end - b_col + i_col
            m_new = jnp.maximum(b_end + m_old, jnp.max(g_col, axis=0, keepdims=True))
            w_old = jnp.exp(b_end + m_old - m_new)
            kw = (k.astype(F32) * jnp.exp(g_col - m_new)).T.astype(BF16)
            cn_ref[d, h] = w_old * cn_old + jnp.dot(kw, v_ext, preferred_element_type=F32)
            m_ref[d, h] = jnp.broadcast_to(m_new, (8, LANES))


def _mlstm(q, k, v, g, gate_b, B, S, n_ctx_chunks):
    nc = S // CHUNK
    ncx = n_ctx_chunks

    def bwd_chunk(c):
        return jnp.where(c < ncx, ncx - 1 - c, nc - 1 + ncx - c)

    fw = lambda w: pl.BlockSpec((1, CHUNK, w), lambda b, c: (b, c, 0))
    bw = lambda w: pl.BlockSpec((1, CHUNK, w), lambda b, c: (b, bwd_chunk(c), 0))
    gbias = jnp.zeros((1, G_PAD), F32).at[0, :G_COLS].set(gate_b)
    return pl.pallas_call(
        _mlstm_kernel,
        grid=(B, nc),
        in_specs=[fw(M_WIDTH), fw(M_WIDTH), fw(M_WIDTH), fw(G_PAD),
                  bw(M_WIDTH), bw(M_WIDTH), bw(M_WIDTH), bw(G_PAD),
                  pl.BlockSpec((1, G_PAD), lambda b, c: (0, 0))],
        out_specs=[fw(M_WIDTH), bw(M_WIDTH)],
        out_shape=[jax.ShapeDtypeStruct((B, S, M_WIDTH), BF16)] * 2,
        scratch_shapes=[pltpu.VMEM((2, M_HEADS, M_HEAD_DIM, 2 * LANES), F32),
                        pltpu.VMEM((2, M_HEADS, 8, LANES), F32)],
        compiler_params=_cparams(("arbitrary", "arbitrary")),
        name="mlstm_scan",
    )(q, k, v, g, q, k, v, g, gbias)


def _natten_kernel(typ_ref, st_ref, q_ref, k_ref, v_ref, bias_ref, o_ref, *, n_ctx):
    j = pl.program_id(2)
    sub = NA_QROWS * GRID_W
    lane = lax.broadcasted_iota(jnp.int32, (sub, LANES), 1)
    kctx = k_ref[0, 0:n_ctx, :]
    vctx = v_ref[0, 0:n_ctx, :]
    for i in range(TOK_TILE // sub):
        sb = j * (TOK_TILE // sub) + i
        typ = typ_ref[sb]
        st = pl.multiple_of(st_ref[sb], GRID_W)
        q = q_ref[0, i * sub:(i + 1) * sub, :] * (N_HEAD_DIM ** -0.5)
        kwin = k_ref[0, pl.ds(st, NA_WKEYS), :]
        vwin = v_ref[0, pl.ds(st, NA_WKEYS), :]
        out = jnp.zeros((sub, LANES), F32)
        for hh in range(2):
            mine = (lane >= N_HEAD_DIM) if hh else (lane < N_HEAD_DIM)
            qm = jnp.where(mine, q, jnp.zeros_like(q))
            s_win = _dot_nt(qm, kwin) + bias_ref[typ, hh]
            s_ctx = _dot_nt(qm, kctx)
            m = jnp.maximum(jnp.max(s_win, axis=-1, keepdims=True), jnp.max(s_ctx, axis=-1, keepdims=True))
            p_win = jnp.exp(s_win - m)
            p_ctx = jnp.exp(s_ctx - m)
            l = jnp.sum(p_win, axis=-1, keepdims=True) + jnp.sum(p_ctx, axis=-1, keepdims=True)
            o = (jnp.dot(p_win.astype(BF16), vwin, preferred_element_type=F32)
                 + jnp.dot(p_ctx.astype(BF16), vctx, preferred_element_type=F32))
            out = jnp.where(mine, o / l, out)
        o_ref[0, i * sub:(i + 1) * sub, :] = out.astype(BF16)


def _natten_tables(rpb, rows, n_ctx):
    n_sb = rows // NA_QROWS
    reps = (0, 1, 2, n_sb - 2, n_sb - 1)
    ws_of = lambda sb: int(np.clip(NA_QROWS * sb - WIN_H // 2, 0, rows - NA_WROWS))
    dr = np.zeros((5, NA_QROWS * GRID_W, NA_WKEYS), np.int32)
    dc = np.zeros_like(dr)
    ok = np.zeros(dr.shape, bool)
    kr_w, kc = np.divmod(np.arange(NA_WKEYS), GRID_W)
    for ty, sb in enumerate(reps):
        ws = ws_of(sb)
        for qi in range(NA_QROWS * GRID_W):
            r = NA_QROWS * sb + qi // GRID_W
            c = qi % GRID_W
            rs = int(np.clip(r - WIN_H // 2, 0, rows - WIN_H))
            cs = int(np.clip(c - WIN_W // 2, 0, GRID_W - WIN_W))
            kr = kr_w + ws
            valid = (kr >= rs) & (kr < rs + WIN_H) & (kc >= cs) & (kc < cs + WIN_W)
            dr[ty, qi] = np.where(valid, kr - r + WIN_H - 1, 0)
            dc[ty, qi] = np.where(valid, kc - c + WIN_W - 1, 0)
            ok[ty, qi] = valid
    bias = jnp.where(ok[None], rpb[:, dr, dc], NEG)
    bias = jnp.concatenate([jnp.transpose(bias, (1, 0, 2, 3)),
                            jnp.full((1, N_HEADS) + dr.shape[1:], NEG, F32)], axis=0)
    typ, st = [5] * (n_ctx // (NA_QROWS * GRID_W)), [n_ctx] * (n_ctx // (NA_QROWS * GRID_W))
    for sb in range(n_sb):
        ws = ws_of(sb)
        typ.append((NA_QROWS * sb - ws) // 2)
        st.append(n_ctx + ws * GRID_W)
    return jnp.asarray(typ, jnp.int32), jnp.asarray(st, jnp.int32), bias


def _natten(nq, nk, nv, typ, st, bias, B, S, n_ctx):
    n_pairs = N_HEADS // 2
    grid_spec = pltpu.PrefetchScalarGridSpec(
        num_scalar_prefetch=2,
        grid=(n_pairs, B, S // TOK_TILE),
        in_specs=[pl.BlockSpec((1, TOK_TILE, LANES), lambda p, b, j, *_: (b, j, p)),
                  pl.BlockSpec((1, S, LANES), lambda p, b, j, *_: (b, 0, p)),
                  pl.BlockSpec((1, S, LANES), lambda p, b, j, *_: (b, 0, p)),
                  pl.BlockSpec((6, 2, NA_QROWS * GRID_W, NA_WKEYS), lambda p, b, j, *_: (0, p, 0, 0))],
        out_specs=pl.BlockSpec((1, TOK_TILE, LANES), lambda p, b, j, *_: (b, j, p)),
    )
    return pl.pallas_call(
        functools.partial(_natten_kernel, n_ctx=n_ctx),
        grid_spec=grid_spec,
        out_shape=jax.ShapeDtypeStruct((B, S, N_WIDTH), BF16),
        compiler_params=_cparams(("arbitrary", "arbitrary", "arbitrary")),
        name="natten",
    )(typ, st, nq, nk, nv, bias)


def _route(logits_t, rb):
    e_i = lax.broadcasted_iota(jnp.int32, logits_t.shape, 0)
    z = logits_t - jnp.max(logits_t, axis=0, keepdims=True)
    ez = jnp.exp(z)
    scores = ez / jnp.sum(ez, axis=0, keepdims=True)
    sel = scores + rb
    best = None
    best_score = None
    for gi in range(N_GROUPS):
        r = [sel[gi * EXPERTS_PER_GROUP + u:gi * EXPERTS_PER_GROUP + u + 1, :] for u in range(EXPERTS_PER_GROUP)]
        gs = None
        for u in range(EXPERTS_PER_GROUP):
            for w in range(u + 1, EXPERTS_PER_GROUP):
                pair = r[u] + r[w]
                gs = pair if gs is None else jnp.maximum(gs, pair)
        if best is None:
            best, best_score = jnp.zeros(gs.shape, jnp.int32), gs
        else:
            better = gs > best_score
            best = jnp.where(better, gi, best)
            best_score = jnp.where(better, gs, best_score)
    masked = jnp.where((e_i // EXPERTS_PER_GROUP) == best, sel, -jnp.inf)
    v1 = jnp.max(masked, axis=0, keepdims=True)
    i1 = jnp.min(jnp.where(masked == v1, e_i, N_EXPERTS), axis=0, keepdims=True)
    masked2 = jnp.where(e_i == i1, -jnp.inf, masked)
    v2 = jnp.max(masked2, axis=0, keepdims=True)
    i2 = jnp.min(jnp.where(masked2 == v2, e_i, N_EXPERTS), axis=0, keepdims=True)
    w1 = jnp.sum(jnp.where(e_i == i1, scores, 0.0), axis=0, keepdims=True)
    w2 = jnp.sum(jnp.where(e_i == i2, scores, 0.0), axis=0, keepdims=True)
    tot = w1 + w2
    return jnp.where(e_i == i1, w1 / tot, 0.0) + jnp.where(e_i == i2, w2 / tot, 0.0)


def _outproj_kernel(*refs, split_input):
    n_stream = 2 if split_input else 1
    stream = refs[:n_stream]
    (hf_ref, hb_ref, mo_ref, no_ref, mod_ref, mnw_ref, wo_ref, n2w_ref, rw_ref, rb_ref,
     xo_ref, hx_ref, gate_ref) = refs[n_stream:]
    if split_input:
        xt = jnp.where(pl.program_id(1) == 0, stream[1][0], stream[0][0])
    else:
        xt = stream[0][0]
    hsum = hf_ref[0].astype(F32) + hb_ref[0].astype(F32)
    parts = []
    for h in range(M_HEADS):
        hs = slice(h * M_HEAD_DIM, (h + 1) * M_HEAD_DIM)
        parts.append(_rms(hsum[:, hs], mnw_ref[:, hs]))
    m_out = jnp.concatenate(parts, axis=1) * jax.nn.sigmoid(mo_ref[0].astype(F32))
    y = (jnp.dot(m_out.astype(BF16), wo_ref[0:M_WIDTH, :], preferred_element_type=F32)
         + jnp.dot(no_ref[0], wo_ref[M_WIDTH:, :], preferred_element_type=F32))
    x_new = xt + mod_ref[0, 0, 2:3, :] * y
    xo_ref[0] = x_new
    hx = _rms(x_new, n2w_ref[...]) * (1.0 + mod_ref[0, 0, 4:5, :]) + mod_ref[0, 0, 3:4, :]
    hx_ref[0] = hx.astype(BF16)
    hh, hl = _split_bf16(hx)
    rh, rl = _split_bf16(rw_ref[...])
    logits_t = _dot_nt(rh, hh) + (_dot_nt(rh, hl) + _dot_nt(rl, hh))
    gate_t = _route(logits_t, rb_ref[...])
    pad = jnp.zeros((LANES - N_EXPERTS, gate_t.shape[1]), F32)
    gate_ref[0] = jnp.concatenate([gate_t, pad], axis=0).T


def _outproj(stream, hf, hb, mo, n_out, mod, mnorm_w, w_out, norm2_w, router_wt, router_b, B, S):
    split_input = len(stream) == 2
    tm = TOK_TILE
    tok = lambda w: pl.BlockSpec((1, tm, w), lambda b, j: (b, j, 0))
    const = lambda shape: pl.BlockSpec(shape, lambda b, j: (0,) * len(shape))
    return pl.pallas_call(
        functools.partial(_outproj_kernel, split_input=split_input),
        grid=(B, S // tm),
        in_specs=_stream_specs(split_input) + [
            tok(M_WIDTH), tok(M_WIDTH), tok(M_WIDTH), tok(N_WIDTH), _mod_spec(),
            const((1, M_WIDTH)), const((D_MODEL, D_MODEL)), const((1, D_MODEL)),
            const((N_EXPERTS, D_MODEL)), const((N_EXPERTS, 1))],
        out_specs=[tok(D_MODEL), tok(D_MODEL), tok(LANES)],
        out_shape=[jax.ShapeDtypeStruct((B, S, D_MODEL), F32),
                   jax.ShapeDtypeStruct((B, S, D_MODEL), BF16),
                   jax.ShapeDtypeStruct((B, S, LANES), F32)],
        compiler_params=_cparams(("arbitrary", "arbitrary")),
        name="outproj_router",
    )(*stream, hf, hb, mo, n_out, mod, mnorm_w.reshape(1, -1), w_out, norm2_w.reshape(1, -1),
      router_wt, router_b.reshape(-1, 1))


def _moe_kernel(h_ref, gate_ref, x_ref, mod_ref, w1_ref, w3_ref, w2_ref, fw_ref, o_ref, acc_ref,
                *, n_ctx, final_norm):
    e = pl.program_id(2)

    @pl.when(e == 0)
    def _():
        acc_ref[...] = jnp.zeros_like(acc_ref)

    h = h_ref[0]
    a = jnp.dot(h, w1_ref[0], preferred_element_type=F32)
    b3 = jnp.dot(h, w3_ref[0], preferred_element_type=F32)
    lane = lax.broadcasted_iota(jnp.int32, gate_ref.shape[1:], 1)
    g_e = jnp.sum(jnp.where(lane == e, gate_ref[0], 0.0), axis=-1, keepdims=True)
    act = (a * jax.nn.sigmoid(a) * b3 * g_e).astype(BF16)
    acc_ref[...] += jnp.dot(act, w2_ref[0], preferred_element_type=F32)

    @pl.when(e == N_EXPERTS - 1)
    def _():
        tm = acc_ref.shape[0]
        row = pl.program_id(1) * tm + lax.broadcasted_iota(jnp.int32, (tm, 1), 0)
        gate_mod = jnp.where(row < n_ctx, mod_ref[0, 0, 5:6, :], mod_ref[0, 1, 5:6, :])
        x_new = x_ref[0] + gate_mod * acc_ref[...]
        if final_norm:
            x_new = _rms(x_new, fw_ref[...])
        o_ref[0] = x_new


def _moe(hx, gate, xa, mod, w1, w3, w2, final_w, B, S, n_ctx, final_norm):
    tm = S // 4
    tok = lambda w: pl.BlockSpec((1, tm, w), lambda b, j, e: (b, j, 0))
    return pl.pallas_call(
        functools.partial(_moe_kernel, n_ctx=n_ctx, final_norm=final_norm),
        grid=(B, S // tm, N_EXPERTS),
        in_specs=[tok(D_MODEL), tok(LANES), tok(D_MODEL),
                  pl.BlockSpec((1, 2, 6, D_MODEL), lambda b, j, e: (b, 0, 0, 0)),
                  pl.BlockSpec((1, D_MODEL, MOE_D_FF), lambda b, j, e: (e, 0, 0)),
                  pl.BlockSpec((1, D_MODEL, MOE_D_FF), lambda b, j, e: (e, 0, 0)),
                  pl.BlockSpec((1, MOE_D_FF, D_MODEL), lambda b, j, e: (e, 0, 0)),
                  pl.BlockSpec((1, D_MODEL), lambda b, j, e: (0, 0))],
        out_specs=tok(D_MODEL),
        out_shape=jax.ShapeDtypeStruct((B, S, D_MODEL), F32),
        scratch_shapes=[pltpu.VMEM((tm, D_MODEL), F32)],
        compiler_params=_cparams(("arbitrary", "arbitrary", "arbitrary")),
        name="moe",
    )(hx, gate, xa, mod, w1, w3, w2, final_w.reshape(1, -1))


def _rope_tables(L, n_ctx):
    t = jnp.arange(L)
    row = (t // GRID_W).astype(F32)
    col = (t % GRID_W).astype(F32)
    inv = ROPE_BASE ** (-jnp.arange(0, ROPE_AXIS_DIM, 2, dtype=F32) / ROPE_AXIS_DIM)
    ar = row[:, None] * inv
    ac = col[:, None] * inv
    cos_l = jnp.concatenate([jnp.cos(ar), jnp.cos(ar), jnp.cos(ac), jnp.cos(ac)], axis=1)
    sin_l = jnp.concatenate([-jnp.sin(ar), jnp.sin(ar), -jnp.sin(ac), jnp.sin(ac)], axis=1)
    cos_t = jnp.concatenate([jnp.ones((n_ctx, M_HEAD_DIM), F32), cos_l], axis=0)
    sin_t = jnp.concatenate([jnp.zeros((n_ctx, M_HEAD_DIM), F32), sin_l], axis=0)
    return cos_t, sin_t


def _proj_weight(w_in):
    q_cols = 2 * M_WIDTH + N_WIDTH
    kv0 = q_cols
    g0 = kv0 + 2 * M_WIDTH
    n0 = g0 + G_COLS
    gates = jnp.pad(w_in[:, g0:n0], ((0, 0), (0, G_PAD - G_COLS)))
    return jnp.concatenate([w_in[:, :q_cols], w_in[:, kv0:g0], w_in[:, n0:], gates], axis=1).astype(BF16)


def kernel(x, c, ctx, c_ctx, ada_w, ada_b, norm1_w, w_in, conv_w, conv_b, gate_b, mnorm_w, rpb, w_out,
           norm2_w, router_w, router_b, exp_w1, exp_w3, exp_w2, final_norm_w):
    B, L, _ = x.shape
    n_ctx = ctx.shape[1]
    depth = ada_w.shape[0]
    assert n_ctx == TOK_TILE and L % TOK_TILE == 0 and L % GRID_W == 0
    S = n_ctx + L
    rows = L // GRID_W
    assert rows >= NA_WROWS + 4 and rows % NA_QROWS == 0 and (S // 4) % HALO == 0

    cvec = jnp.concatenate([c, c_ctx[None], jnp.zeros((16 - B - 1, D_MODEL), F32)], axis=0)
    mods = _adaln(cvec, ada_w, ada_b)
    cos_t, sin_t = _rope_tables(L, n_ctx)
    router_wt = router_w.T

    stream = (x, ctx)
    out = None
    for layer in range(depth):
        last = layer == depth - 1
        m6 = mods[layer].reshape(16, 6, D_MODEL)
        mod = jnp.stack([jnp.broadcast_to(m6[B], (B, 6, D_MODEL)), m6[:B]], axis=1)
        mq, mo, nq, mk, mv, nk, nv, g = _inproj(stream, mod, norm1_w[layer], _proj_weight(w_in[layer]), B, S)
        q, k = _prep(mq, mk, conv_w[layer], conv_b[layer], cos_t, sin_t, B, S)
        hf, hb = _mlstm(q, k, mv, g, gate_b[layer], B, S, n_ctx // CHUNK)
        typ, st, bias = _natten_tables(rpb[layer], rows, n_ctx)
        n_out = _natten(nq, nk, nv, typ, st, bias, B, S, n_ctx)
        xa, hx, gate = _outproj(stream, hf, hb, mo, n_out, mod, mnorm_w[layer], w_out[layer].astype(BF16),
                                norm2_w[layer], router_wt, router_b, B, S)
        out = _moe(hx, gate, xa, mod, exp_w1[layer].astype(BF16), exp_w3[layer].astype(BF16),
                   exp_w2[layer].astype(BF16), final_norm_w, B, S, n_ctx, last)
        stream = (out,)
    return out[:, n_ctx:]
```

```python
import functools

import numpy as np
import jax
import jax.numpy as jnp
from jax import lax
from jax.experimental import pallas as pl
from jax.experimental.pallas import tpu as pltpu

D_MODEL = 1024
M_WIDTH = 512
M_HEADS = 4
M_HEAD_DIM = 128
N_WIDTH = 512
N_HEADS = 8
N_HEAD_DIM = 64
G_COLS = 16
GRID_W = 64
WIN_H = 8
WIN_W = 16
CHUNK = 128
ROPE_AXIS_DIM = 64
ROPE_BASE = 10000.0
N_EXPERTS = 16
N_GROUPS = 4
EXPERTS_PER_GROUP = 4
MOE_D_FF = 512
NORM_EPS = 1e-6

LANES = 128
TOK_TILE = 256
HALO = 16
NA_QROWS = 2
NA_WROWS = 10
NA_WKEYS = NA_WROWS * GRID_W
NEG = -1e30
G_PAD = LANES
N_PROJ = 7 * M_WIDTH + G_PAD
VMEM_LIMIT = 56 * 1024 * 1024

F32 = jnp.float32
BF16 = jnp.bfloat16


def _cparams(sem):
    return pltpu.CompilerParams(dimension_semantics=sem, vmem_limit_bytes=VMEM_LIMIT)


def _split_bf16(a):
    hi = a.astype(BF16)
    lo = (a - hi.astype(F32)).astype(BF16)
    return hi, lo


def _dot3(a, b):
    ah, al = _split_bf16(a)
    bh, bl = _split_bf16(b)
    d = functools.partial(jnp.dot, preferred_element_type=F32)
    return d(ah, bh) + (d(al, bh) + d(ah, bl))


def _dot_nt(a, b):
    return lax.dot_general(a, b, (((1,), (1,)), ((), ())), preferred_element_type=F32)


def _rms(x, w):
    ms = jnp.mean(x * x, axis=-1, keepdims=True)
    return x * lax.rsqrt(ms + NORM_EPS) * w


def _adaln_kernel(c_ref, w_ref, b_ref, o_ref):
    c = c_ref[...]
    s = c * jax.nn.sigmoid(c)
    o_ref[0] = _dot3(s, w_ref[0]) + b_ref[0]


def _adaln(cvec, ada_w, ada_b):
    depth = ada_w.shape[0]
    n = ada_w.shape[2]
    tn = D_MODEL
    return pl.pallas_call(
        _adaln_kernel,
        grid=(depth, n // tn),
        in_specs=[pl.BlockSpec((16, D_MODEL), lambda l, j: (0, 0)),
                  pl.BlockSpec((1, D_MODEL, tn), lambda l, j: (l, 0, j)),
                  pl.BlockSpec((1, 1, tn), lambda l, j: (l, 0, j))],
        out_specs=pl.BlockSpec((1, 16, tn), lambda l, j: (l, 0, j)),
        out_shape=jax.ShapeDtypeStruct((depth, 16, n), F32),
        compiler_params=_cparams(("arbitrary", "arbitrary")),
        name="adaln",
    )(cvec, ada_w, ada_b.reshape(depth, 1, n))


def _inproj_kernel(*refs, split_input):
    if split_input:
        x_ref, ctx_ref, mod_ref, nw_ref, w_ref = refs[:5]
        outs = refs[5:]
        xt = jnp.where(pl.program_id(1) == 0, ctx_ref[0], x_ref[0])
    else:
        x_ref, mod_ref, nw_ref, w_ref = refs[:4]
        outs = refs[4:]
        xt = x_ref[0]
    h = _rms(xt, nw_ref[...]) * (1.0 + mod_ref[0, 0, 1:2, :]) + mod_ref[0, 0, 0:1, :]
    hb = h.astype(BF16)
    for k in range(7):
        outs[k][0] = jnp.dot(hb, w_ref[:, k * M_WIDTH:(k + 1) * M_WIDTH],
                             preferred_element_type=F32).astype(BF16)
    outs[7][0] = jnp.dot(hb, w_ref[:, 7 * M_WIDTH:], preferred_element_type=F32)


def _stream_specs(split_input):
    tm = TOK_TILE
    if split_input:
        return [pl.BlockSpec((1, tm, D_MODEL), lambda b, j: (b, jnp.maximum(j - 1, 0), 0)),
                pl.BlockSpec((1, tm, D_MODEL), lambda b, j: (b, 0, 0))]
    return [pl.BlockSpec((1, tm, D_MODEL), lambda b, j: (b, j, 0))]


def _mod_spec():
    return pl.BlockSpec((1, 1, 6, D_MODEL), lambda b, j: (b, jnp.minimum(j, 1), 0, 0))


def _inproj(stream, mod, norm_w, w_proj, B, S):
    split_input = len(stream) == 2
    tm = TOK_TILE
    tok = lambda w: pl.BlockSpec((1, tm, w), lambda b, j: (b, j, 0))
    out_shape = [jax.ShapeDtypeStruct((B, S, M_WIDTH), BF16)] * 7 + [jax.ShapeDtypeStruct((B, S, G_PAD), F32)]
    return pl.pallas_call(
        functools.partial(_inproj_kernel, split_input=split_input),
        grid=(B, S // tm),
        in_specs=_stream_specs(split_input) + [
            _mod_spec(),
            pl.BlockSpec((1, D_MODEL), lambda b, j: (0, 0)),
            pl.BlockSpec((D_MODEL, N_PROJ), lambda b, j: (0, 0))],
        out_specs=[tok(M_WIDTH)] * 7 + [tok(G_PAD)],
        out_shape=out_shape,
        compiler_params=_cparams(("arbitrary", "arbitrary")),
        name="inproj",
    )(*stream, mod, norm_w.reshape(1, D_MODEL), w_proj)


def _prep_kernel(q_ref, qp_ref, qn_ref, k_ref, kp_ref, kn_ref, cw_ref, cb_ref, cos_ref, sin_ref,
                 qo_ref, ko_ref, *, n_tiles):
    j = pl.program_id(1)
    tp = TOK_TILE
    has_prev = jnp.logical_and(j != 0, j != 1).astype(F32)
    has_next = jnp.logical_and(j != 0, j != n_tiles - 1).astype(F32)
    rows = lax.broadcasted_iota(jnp.int32, (tp, M_WIDTH), 0)
    lanes = lax.broadcasted_iota(jnp.int32, (tp, M_WIDTH), 1)
    low_half = (lanes % (ROPE_AXIS_DIM)) < (ROPE_AXIS_DIM // 2)
    cosv = jnp.concatenate([cos_ref[...]] * M_HEADS, axis=1)
    sinv = jnp.concatenate([sin_ref[...]] * M_HEADS, axis=1)

    def branch(x_ref, p_ref, n_ref, col0):
        x = x_ref[0].astype(F32)
        prev_row = p_ref[0, HALO - 1:HALO, :].astype(F32) * has_prev
        next_row = n_ref[0, 0:1, :].astype(F32) * has_next
        xm = jnp.where(rows == 0, prev_row, pltpu.roll(x, 1, 0))
        xp = jnp.where(rows == tp - 1, next_row, pltpu.roll(x, tp - 1, 0))
        w = cw_ref[:, col0:col0 + M_WIDTH]
        y = cb_ref[:, col0:col0 + M_WIDTH] + xm * w[0:1] + x * w[1:2] + xp * w[2:3]
        y = y * jax.nn.sigmoid(y)
        half = ROPE_AXIS_DIM // 2
        partner = jnp.where(low_half, pltpu.roll(y, M_WIDTH - half, 1), pltpu.roll(y, half, 1))
        return y * cosv + partner * sinv

    qo_ref[0] = branch(q_ref, qp_ref, qn_ref, 0).astype(BF16)
    ko_ref[0] = (branch(k_ref, kp_ref, kn_ref, M_WIDTH) * (M_HEAD_DIM ** -0.5)).astype(BF16)


def _prep(mq, mk, conv_w, conv_b, cos_t, sin_t, B, S):
    tp = TOK_TILE
    n_tiles = S // tp
    per = tp // HALO
    n_halo = S // HALO
    main = pl.BlockSpec((1, tp, M_WIDTH), lambda b, j: (b, j, 0))
    prev = pl.BlockSpec((1, HALO, M_WIDTH), lambda b, j: (b, jnp.maximum(j * per - 1, 0), 0))
    nxt = pl.BlockSpec((1, HALO, M_WIDTH), lambda b, j: (b, jnp.minimum((j + 1) * per, n_halo - 1), 0))
    return pl.pallas_call(
        functools.partial(_prep_kernel, n_tiles=n_tiles),
        grid=(B, n_tiles),
        in_specs=[main, prev, nxt, main, prev, nxt,
                  pl.BlockSpec((3, 2 * M_WIDTH), lambda b, j: (0, 0)),
                  pl.BlockSpec((1, 2 * M_WIDTH), lambda b, j: (0, 0)),
                  pl.BlockSpec((tp, M_HEAD_DIM), lambda b, j: (j, 0)),
                  pl.BlockSpec((tp, M_HEAD_DIM), lambda b, j: (j, 0))],
        out_specs=[main, main],
        out_shape=[jax.ShapeDtypeStruct((B, S, M_WIDTH), BF16)] * 2,
        compiler_params=_cparams(("arbitrary", "arbitrary")),
        name="mlstm_prep",
    )(mq, mq, mq, mk, mk, mk, conv_w, conv_b.reshape(1, -1), cos_t, sin_t)


def _cumsum_rows(x, reverse):
    n = x.shape[0]
    rows = lax.broadcasted_iota(jnp.int32, x.shape, 0)
    sh = 1
    while sh < n:
        if reverse:
            x = x + jnp.where(rows < n - sh, pltpu.roll(x, n - sh, 0), 0.0)
        else:
            x = x + jnp.where(rows >= sh, pltpu.roll(x, sh, 0), 0.0)
        sh *= 2
    return x


def _log_sigmoid(x):
    return jnp.minimum(x, 0.0) - jnp.log(1.0 + jnp.exp(-jnp.abs(x)))


def _mlstm_kernel(qf_ref, kf_ref, vf_ref, gf_ref, qb_ref, kb_ref, vb_ref, gb_ref, gbias_ref,
                  hf_ref, hb_ref, cn_ref, m_ref):
    c = pl.program_id(1)

    @pl.when(c == 0)
    def _():
        cn_ref[...] = jnp.zeros_like(cn_ref)
        m_ref[...] = jnp.zeros_like(m_ref)

    t = CHUNK
    r_i = lax.broadcasted_iota(jnp.int32, (t, t), 0)
    c_i = lax.broadcasted_iota(jnp.int32, (t, t), 1)
    one_col = (lax.broadcasted_iota(jnp.int32, (t, LANES), 1) == 0).astype(BF16)

    for d, (q_ref, k_ref, v_ref, g_ref, h_ref) in enumerate(
            ((qf_ref, kf_ref, vf_ref, gf_ref, hf_ref), (qb_ref, kb_ref, vb_ref, gb_ref, hb_ref))):
        reverse = d == 1
        mask = (c_i >= r_i) if reverse else (c_i <= r_i)
        g = g_ref[0] + gbias_ref[...]
        bc = _cumsum_rows(_log_sigmoid(g), reverse)
        bt = bc.T
        gt = g.T
        end_row = 0 if reverse else t - 1
        for h in range(M_HEADS):
            icol = 2 * M_HEADS * d + h
            fcol = icol + M_HEADS
            hs = slice(h * M_HEAD_DIM, (h + 1) * M_HEAD_DIM)
            b_col = bc[:, fcol:fcol + 1]
            b_row = bt[fcol:fcol + 1, :]
            i_col = g[:, icol:icol + 1]
            i_row = gt[icol:icol + 1, :]
            b_end = bc[end_row:end_row + 1, fcol:fcol + 1]
            m_old = m_ref[d, h][0:1, 0:1]

            q = q_ref[0, :, hs]
            k = k_ref[0, :, hs]
            v_ext = jnp.concatenate([v_ref[0, :, hs], one_col], axis=1)
            cn_old = cn_ref[d, h]

            logd = jnp.where(mask, b_col + (i_row - b_row), NEG)
            m_row = jnp.maximum(b_col + m_old, jnp.max(logd, axis=-1, keepdims=True))
            dmat = jnp.exp(logd - m_row)
            w_inter = jnp.exp(b_col + m_old - m_row)
            s = (_dot_nt(q, k) * dmat).astype(BF16)
            ext = (jnp.dot(s, v_ext, preferred_element_type=F32)
                   + w_inter * jnp.dot(q, cn_old.astype(BF16), preferred_element_type=F32))
            qn = ext[:, M_HEAD_DIM:M_HEAD_DIM + 1]
            den = jnp.maximum(jnp.abs(qn), jnp.exp(-m_row))
            h_ref[0, :, hs] = (ext[:, :M_HEAD_DIM] / den).astype(BF16)

            g_col = b_end - b_col + i_col
            m_new = jnp.maximum(b_end + m_old, jnp.max(g_col, axis=0, keepdims=True))
            w_old = jnp.exp(b_end + m_old - m_new)
            kw = (k.astype(F32) * jnp.exp(g_col - m_new)).T.astype(BF16)
            cn_ref[d, h] = w_old * cn_old + jnp.dot(kw, v_ext, preferred_element_type=F32)
            m_ref[d, h] = jnp.broadcast_to(m_new, (8, LANES))


def _mlstm(q, k, v, g, gate_b, B, S, n_ctx_chunks):
    nc = S // CHUNK
    ncx = n_ctx_chunks

    def bwd_chunk(c):
        return jnp.where(c < ncx, ncx - 1 - c, nc - 1 + ncx - c)

    fw = lambda w: pl.BlockSpec((1, CHUNK, w), lambda b, c: (b, c, 0))
    bw = lambda w: pl.BlockSpec((1, CHUNK, w), lambda b, c: (b, bwd_chunk(c), 0))
    gbias = jnp.zeros((1, G_PAD), F32).at[0, :G_COLS].set(gate_b)
    return pl.pallas_call(
        _mlstm_kernel,
        grid=(B, nc),
        in_specs=[fw(M_WIDTH), fw(M_WIDTH), fw(M_WIDTH), fw(G_PAD),
                  bw(M_WIDTH), bw(M_WIDTH), bw(M_WIDTH), bw(G_PAD),
                  pl.BlockSpec((1, G_PAD), lambda b, c: (0, 0))],
        out_specs=[fw(M_WIDTH), bw(M_WIDTH)],
        out_shape=[jax.ShapeDtypeStruct((B, S, M_WIDTH), BF16)] * 2,
        scratch_shapes=[pltpu.VMEM((2, M_HEADS, M_HEAD_DIM, 2 * LANES), F32),
                        pltpu.VMEM((2, M_HEADS, 8, LANES), F32)],
        compiler_params=_cparams(("arbitrary", "arbitrary")),
        name="mlstm_scan",
    )(q, k, v, g, q, k, v, g, gbias)


def _natten_kernel(typ_ref, st_ref, q_ref, k_ref, v_ref, bias_ref, o_ref, *, n_ctx):
    j = pl.program_id(2)
    sub = NA_QROWS * GRID_W
    lane = lax.broadcasted_iota(jnp.int32, (sub, LANES), 1)
    kctx = k_ref[0, 0:n_ctx, :]
    vctx = v_ref[0, 0:n_ctx, :]
    for i in range(TOK_TILE // sub):
        sb = j * (TOK_TILE // sub) + i
        typ = typ_ref[sb]
        st = pl.multiple_of(st_ref[sb], GRID_W)
        q = q_ref[0, i * sub:(i + 1) * sub, :] * (N_HEAD_DIM ** -0.5)
        kwin = k_ref[0, pl.ds(st, NA_WKEYS), :]
        vwin = v_ref[0, pl.ds(st, NA_WKEYS), :]
        out = jnp.zeros((sub, LANES), F32)
        for hh in range(2):
            mine = (lane >= N_HEAD_DIM) if hh else (lane < N_HEAD_DIM)
            qm = jnp.where(mine, q, jnp.zeros_like(q))
            s_win = _dot_nt(qm, kwin) + bias_ref[typ, hh]
            s_ctx = _dot_nt(qm, kctx)
            m = jnp.maximum(jnp.max(s_win, axis=-1, keepdims=True), jnp.max(s_ctx, axis=-1, keepdims=True))
            p_win = jnp.exp(s_win - m)
            p_ctx = jnp.exp(s_ctx - m)
            l = jnp.sum(p_win, axis=-1, keepdims=True) + jnp.sum(p_ctx, axis=-1, keepdims=True)
            o = (jnp.dot(p_win.astype(BF16), vwin, preferred_element_type=F32)
                 + jnp.dot(p_ctx.astype(BF16), vctx, preferred_element_type=F32))
            out = jnp.where(mine, o / l, out)
        o_ref[0, i * sub:(i + 1) * sub, :] = out.astype(BF16)


def _natten_tables(rpb, rows, n_ctx):
    n_sb = rows // NA_QROWS
    reps = (0, 1, 2, n_sb - 2, n_sb - 1)
    ws_of = lambda sb: int(np.clip(NA_QROWS * sb - WIN_H // 2, 0, rows - NA_WROWS))
    n_dr = 2 * WIN_H - 1
    n_dc = 2 * WIN_W - 1
    lead = GRID_W - WIN_W
    vec = jnp.pad(rpb, ((0, 0), (0, 0), (lead, 2 * GRID_W - lead - n_dc)), constant_values=NEG)
    skew = jnp.broadcast_to(vec[:, :, None, :], (N_HEADS, n_dr, GRID_W, 2 * GRID_W))
    skew = skew.reshape(N_HEADS, n_dr, -1)[:, :, :GRID_W * (2 * GRID_W - 1)]
    toep = skew.reshape(N_HEADS, n_dr, GRID_W, 2 * GRID_W - 1)[..., GRID_W - 1:]
    cq = np.arange(GRID_W)[:, None]
    ck = np.arange(GRID_W)[None, :]
    cs = np.clip(cq - WIN_W // 2, 0, GRID_W - WIN_W)
    toep = jnp.where((ck >= cs) & (ck < cs + WIN_W), toep, NEG)
    masked_tile = jnp.full((N_HEADS, GRID_W, GRID_W), NEG, F32)
    types = []
    for sb in reps:
        ws = ws_of(sb)
        q_rows = []
        for qr in range(NA_QROWS):
            r = NA_QROWS * sb + qr
            rs = int(np.clip(r - WIN_H // 2, 0, rows - WIN_H))
            tiles = []
            for kw in range(NA_WROWS):
                kr = ws + kw
                tiles.append(toep[:, kr - r + WIN_H - 1] if rs <= kr < rs + WIN_H else masked_tile)
            q_rows.append(jnp.concatenate(tiles, axis=-1))
        types.append(jnp.concatenate(q_rows, axis=-2))
    types.append(jnp.full_like(types[0], NEG))
    bias = jnp.stack(types, axis=0)
    typ, st = [5] * (n_ctx // (NA_QROWS * GRID_W)), [n_ctx] * (n_ctx // (NA_QROWS * GRID_W))
    for sb in range(n_sb):
        ws = ws_of(sb)
        typ.append((NA_QROWS * sb - ws) // 2)
        st.append(n_ctx + ws * GRID_W)
    return jnp.asarray(typ, jnp.int32), jnp.asarray(st, jnp.int32), bias


def _natten(nq, nk, nv, typ, st, bias, B, S, n_ctx):
    n_pairs = N_HEADS // 2
    grid_spec = pltpu.PrefetchScalarGridSpec(
        num_scalar_prefetch=2,
        grid=(n_pairs, B, S // TOK_TILE),
        in_specs=[pl.BlockSpec((1, TOK_TILE, LANES), lambda p, b, j, *_: (b, j, p)),
                  pl.BlockSpec((1, S, LANES), lambda p, b, j, *_: (b, 0, p)),
                  pl.BlockSpec((1, S, LANES), lambda p, b, j, *_: (b, 0, p)),
                  pl.BlockSpec((6, 2, NA_QROWS * GRID_W, NA_WKEYS), lambda p, b, j, *_: (0, p, 0, 0))],
        out_specs=pl.BlockSpec((1, TOK_TILE, LANES), lambda p, b, j, *_: (b, j, p)),
    )
    return pl.pallas_call(
        functools.partial(_natten_kernel, n_ctx=n_ctx),
        grid_spec=grid_spec,
        out_shape=jax.ShapeDtypeStruct((B, S, N_WIDTH), BF16),
        compiler_params=_cparams(("arbitrary", "arbitrary", "arbitrary")),
        name="natten",
    )(typ, st, nq, nk, nv, bias)


def _route(logits_t, rb):
    e_i = lax.broadcasted_iota(jnp.int32, logits_t.shape, 0)
    z = logits_t - jnp.max(logits_t, axis=0, keepdims=True)
    ez = jnp.exp(z)
    scores = ez / jnp.sum(ez, axis=0, keepdims=True)
    sel = scores + rb
    best = None
    best_score = None
    for gi in range(N_GROUPS):
        r = [sel[gi * EXPERTS_PER_GROUP + u:gi * EXPERTS_PER_GROUP + u + 1, :] for u in range(EXPERTS_PER_GROUP)]
        gs = None
        for u in range(EXPERTS_PER_GROUP):
            for w in range(u + 1, EXPERTS_PER_GROUP):
                pair = r[u] + r[w]
                gs = pair if gs is None else jnp.maximum(gs, pair)
        if best is None:
            best, best_score = jnp.zeros(gs.shape, jnp.int32), gs
        else:
            better = gs > best_score
            best = jnp.where(better, gi, best)
            best_score = jnp.where(better, gs, best_score)
    masked = jnp.where((e_i // EXPERTS_PER_GROUP) == best, sel, -jnp.inf)
    v1 = jnp.max(masked, axis=0, keepdims=True)
    i1 = jnp.min(jnp.where(masked == v1, e_i, N_EXPERTS), axis=0, keepdims=True)
    masked2 = jnp.where(e_i == i1, -jnp.inf, masked)
    v2 = jnp.max(masked2, axis=0, keepdims=True)
    i2 = jnp.min(jnp.where(masked2 == v2, e_i, N_EXPERTS), axis=0, keepdims=True)
    w1 = jnp.sum(jnp.where(e_i == i1, scores, 0.0), axis=0, keepdims=True)
    w2 = jnp.sum(jnp.where(e_i == i2, scores, 0.0), axis=0, keepdims=True)
    tot = w1 + w2
    return jnp.where(e_i == i1, w1 / tot, 0.0) + jnp.where(e_i == i2, w2 / tot, 0.0)


def _outproj_kernel(*refs, split_input):
    n_stream = 2 if split_input else 1
    stream = refs[:n_stream]
    (hf_ref, hb_ref, mo_ref, no_ref, mod_ref, mnw_ref, wo_ref, n2w_ref, rw_ref, rb_ref,
     xo_ref, hx_ref, gate_ref) = refs[n_stream:]
    if split_input:
        xt = jnp.where(pl.program_id(1) == 0, stream[1][0], stream[0][0])
    else:
        xt = stream[0][0]
    hsum = hf_ref[0].astype(F32) + hb_ref[0].astype(F32)
    parts = []
    for h in range(M_HEADS):
        hs = slice(h * M_HEAD_DIM, (h + 1) * M_HEAD_DIM)
        parts.append(_rms(hsum[:, hs], mnw_ref[:, hs]))
    m_out = jnp.concatenate(parts, axis=1) * jax.nn.sigmoid(mo_ref[0].astype(F32))
    y = (jnp.dot(m_out.astype(BF16), wo_ref[0:M_WIDTH, :], preferred_element_type=F32)
         + jnp.dot(no_ref[0], wo_ref[M_WIDTH:, :], preferred_element_type=F32))
    x_new = xt + mod_ref[0, 0, 2:3, :] * y
    xo_ref[0] = x_new
    hx = _rms(x_new, n2w_ref[...]) * (1.0 + mod_ref[0, 0, 4:5, :]) + mod_ref[0, 0, 3:4, :]
    hx_ref[0] = hx.astype(BF16)
    hh, hl = _split_bf16(hx)
    rh, rl = _split_bf16(rw_ref[...])
    logits_t = _dot_nt(rh, hh) + (_dot_nt(rh, hl) + _dot_nt(rl, hh))
    gate_t = _route(logits_t, rb_ref[...])
    pad = jnp.zeros((LANES - N_EXPERTS, gate_t.shape[1]), F32)
    gate_ref[0] = jnp.concatenate([gate_t, pad], axis=0).T


def _outproj(stream, hf, hb, mo, n_out, mod, mnorm_w, w_out, norm2_w, router_wt, router_b, B, S):
    split_input = len(stream) == 2
    tm = TOK_TILE
    tok = lambda w: pl.BlockSpec((1, tm, w), lambda b, j: (b, j, 0))
    const = lambda shape: pl.BlockSpec(shape, lambda b, j: (0,) * len(shape))
    return pl.pallas_call(
        functools.partial(_outproj_kernel, split_input=split_input),
        grid=(B, S // tm),
        in_specs=_stream_specs(split_input) + [
            tok(M_WIDTH), tok(M_WIDTH), tok(M_WIDTH), tok(N_WIDTH), _mod_spec(),
            const((1, M_WIDTH)), const((D_MODEL, D_MODEL)), const((1, D_MODEL)),
            const((N_EXPERTS, D_MODEL)), const((N_EXPERTS, 1))],
        out_specs=[tok(D_MODEL), tok(D_MODEL), tok(LANES)],
        out_shape=[jax.ShapeDtypeStruct((B, S, D_MODEL), F32),
                   jax.ShapeDtypeStruct((B, S, D_MODEL), BF16),
                   jax.ShapeDtypeStruct((B, S, LANES), F32)],
        compiler_params=_cparams(("arbitrary", "arbitrary")),
        name="outproj_router",
    )(*stream, hf, hb, mo, n_out, mod, mnorm_w.reshape(1, -1), w_out, norm2_w.reshape(1, -1),
      router_wt, router_b.reshape(-1, 1))


def _moe_kernel(h_ref, gate_ref, x_ref, mod_ref, w1_ref, w3_ref, w2_ref, fw_ref, o_ref, acc_ref,
                *, n_ctx, final_norm):
    e = pl.program_id(2)

    @pl.when(e == 0)
    def _():
        acc_ref[...] = jnp.zeros_like(acc_ref)

    h = h_ref[0]
    a = jnp.dot(h, w1_ref[0], preferred_element_type=F32)
    b3 = jnp.dot(h, w3_ref[0], preferred_element_type=F32)
    lane = lax.broadcasted_iota(jnp.int32, gate_ref.shape[1:], 1)
    g_e = jnp.sum(jnp.where(lane == e, gate_ref[0], 0.0), axis=-1, keepdims=True)
    act = (a * jax.nn.sigmoid(a) * b3 * g_e).astype(BF16)
    acc_ref[...] += jnp.dot(act, w2_ref[0], preferred_element_type=F32)

    @pl.when(e == N_EXPERTS - 1)
    def _():
        tm = acc_ref.shape[0]
        row = pl.program_id(1) * tm + lax.broadcasted_iota(jnp.int32, (tm, 1), 0)
        gate_mod = jnp.where(row < n_ctx, mod_ref[0, 0, 5:6, :], mod_ref[0, 1, 5:6, :])
        x_new = x_ref[0] + gate_mod * acc_ref[...]
        if final_norm:
            x_new = _rms(x_new, fw_ref[...])
        o_ref[0] = x_new


def _moe(hx, gate, xa, mod, w1, w3, w2, final_w, B, S, n_ctx, final_norm):
    tm = S // 4
    tok = lambda w: pl.BlockSpec((1, tm, w), lambda b, j, e: (b, j, 0))
    return pl.pallas_call(
        functools.partial(_moe_kernel, n_ctx=n_ctx, final_norm=final_norm),
        grid=(B, S // tm, N_EXPERTS),
        in_specs=[tok(D_MODEL), tok(LANES), tok(D_MODEL),
                  pl.BlockSpec((1, 2, 6, D_MODEL), lambda b, j, e: (b, 0, 0, 0)),
                  pl.BlockSpec((1, D_MODEL, MOE_D_FF), lambda b, j, e: (e, 0, 0)),
                  pl.BlockSpec((1, D_MODEL, MOE_D_FF), lambda b, j, e: (e, 0, 0)),
                  pl.BlockSpec((1, MOE_D_FF, D_MODEL), lambda b, j, e: (e, 0, 0)),
                  pl.BlockSpec((1, D_MODEL), lambda b, j, e: (0, 0))],
        out_specs=tok(D_MODEL),
        out_shape=jax.ShapeDtypeStruct((B, S, D_MODEL), F32),
        scratch_shapes=[pltpu.VMEM((tm, D_MODEL), F32)],
        compiler_params=_cparams(("arbitrary", "arbitrary", "arbitrary")),
        name="moe",
    )(hx, gate, xa, mod, w1, w3, w2, final_w.reshape(1, -1))


def _rope_tables(L, n_ctx):
    t = jnp.arange(L)
    row = (t // GRID_W).astype(F32)
    col = (t % GRID_W).astype(F32)
    inv = ROPE_BASE ** (-jnp.arange(0, ROPE_AXIS_DIM, 2, dtype=F32) / ROPE_AXIS_DIM)
    ar = row[:, None] * inv
    ac = col[:, None] * inv
    cos_l = jnp.concatenate([jnp.cos(ar), jnp.cos(ar), jnp.cos(ac), jnp.cos(ac)], axis=1)
    sin_l = jnp.concatenate([-jnp.sin(ar), jnp.sin(ar), -jnp.sin(ac), jnp.sin(ac)], axis=1)
    cos_t = jnp.concatenate([jnp.ones((n_ctx, M_HEAD_DIM), F32), cos_l], axis=0)
    sin_t = jnp.concatenate([jnp.zeros((n_ctx, M_HEAD_DIM), F32), sin_l], axis=0)
    return cos_t, sin_t


def _proj_weight(w_in):
    q_cols = 2 * M_WIDTH + N_WIDTH
    kv0 = q_cols
    g0 = kv0 + 2 * M_WIDTH
    n0 = g0 + G_COLS
    gates = jnp.pad(w_in[:, g0:n0], ((0, 0), (0, G_PAD - G_COLS)))
    return jnp.concatenate([w_in[:, :q_cols], w_in[:, kv0:g0], w_in[:, n0:], gates], axis=1).astype(BF16)


def kernel(x, c, ctx, c_ctx, ada_w, ada_b, norm1_w, w_in, conv_w, conv_b, gate_b, mnorm_w, rpb, w_out,
           norm2_w, router_w, router_b, exp_w1, exp_w3, exp_w2, final_norm_w):
    B, L, _ = x.shape
    n_ctx = ctx.shape[1]
    depth = ada_w.shape[0]
    assert n_ctx == TOK_TILE and L % TOK_TILE == 0 and L % GRID_W == 0
    S = n_ctx + L
    rows = L // GRID_W
    assert rows >= NA_WROWS + 4 and rows % NA_QROWS == 0 and (S // 4) % HALO == 0

    cvec = jnp.concatenate([c, c_ctx[None], jnp.zeros((16 - B - 1, D_MODEL), F32)], axis=0)
    mods = _adaln(cvec, ada_w, ada_b)
    cos_t, sin_t = _rope_tables(L, n_ctx)
    router_wt = router_w.T

    stream = (x, ctx)
    out = None
    for layer in range(depth):
        last = layer == depth - 1
        m6 = mods[layer].reshape(16, 6, D_MODEL)
        mod = jnp.stack([jnp.broadcast_to(m6[B], (B, 6, D_MODEL)), m6[:B]], axis=1)
        mq, mo, nq, mk, mv, nk, nv, g = _inproj(stream, mod, norm1_w[layer], _proj_weight(w_in[layer]), B, S)
        q, k = _prep(mq, mk, conv_w[layer], conv_b[layer], cos_t, sin_t, B, S)
        hf, hb = _mlstm(q, k, mv, g, gate_b[layer], B, S, n_ctx // CHUNK)
        typ, st, bias = _natten_tables(rpb[layer], rows, n_ctx)
        n_out = _natten(nq, nk, nv, typ, st, bias, B, S, n_ctx)
        xa, hx, gate = _outproj(stream, hf, hb, mo, n_out, mod, mnorm_w[layer], w_out[layer].astype(BF16),
                                norm2_w[layer], router_wt, router_b, B, S)
        out = _moe(hx, gate, xa, mod, exp_w1[layer].astype(BF16), exp_w3[layer].astype(BF16),
                   exp_w2[layer].astype(BF16), final_norm_w, B, S, n_ctx, last)
        stream = (out,)
    return out[:, n_ctx:]
```

```python
import functools

import numpy as np
import jax
import jax.numpy as jnp
from jax import lax
from jax.experimental import pallas as pl
from jax.experimental.pallas import tpu as pltpu

D_MODEL = 1024
M_WIDTH = 512
M_HEADS = 4
M_HEAD_DIM = 128
N_WIDTH = 512
N_HEADS = 8
N_HEAD_DIM = 64
G_COLS = 16
GRID_W = 64
WIN_H = 8
WIN_W = 16
CHUNK = 128
ROPE_AXIS_DIM = 64
ROPE_BASE = 10000.0
N_EXPERTS = 16
N_GROUPS = 4
EXPERTS_PER_GROUP = 4
MOE_D_FF = 512
NORM_EPS = 1e-6

LANES = 128
TOK_TILE = 256
HALO = 16
NA_QROWS = 2
NA_WROWS = 10
NA_WKEYS = NA_WROWS * GRID_W
NEG = -1e30
G_PAD = LANES
N_PROJ = 7 * M_WIDTH + G_PAD
VMEM_LIMIT = 56 * 1024 * 1024

F32 = jnp.float32
BF16 = jnp.bfloat16


def _cparams(sem):
    return pltpu.CompilerParams(dimension_semantics=sem, vmem_limit_bytes=VMEM_LIMIT)


def _split_bf16(a):
    hi = a.astype(BF16)
    lo = (a - hi.astype(F32)).astype(BF16)
    return hi, lo


def _dot3(a, b):
    ah, al = _split_bf16(a)
    bh, bl = _split_bf16(b)
    d = functools.partial(jnp.dot, preferred_element_type=F32)
    return d(ah, bh) + (d(al, bh) + d(ah, bl))


def _dot_nt(a, b):
    return lax.dot_general(a, b, (((1,), (1,)), ((), ())), preferred_element_type=F32)


def _rms(x, w):
    ms = jnp.mean(x * x, axis=-1, keepdims=True)
    return x * lax.rsqrt(ms + NORM_EPS) * w


def _adaln_kernel(c_ref, w_ref, b_ref, o_ref):
    c = c_ref[...]
    s = c * jax.nn.sigmoid(c)
    o_ref[0] = _dot3(s, w_ref[0]) + b_ref[0]


def _adaln(cvec, ada_w, ada_b):
    depth = ada_w.shape[0]
    n = ada_w.shape[2]
    tn = D_MODEL
    return pl.pallas_call(
        _adaln_kernel,
        grid=(depth, n // tn),
        in_specs=[pl.BlockSpec((16, D_MODEL), lambda l, j: (0, 0)),
                  pl.BlockSpec((1, D_MODEL, tn), lambda l, j: (l, 0, j)),
                  pl.BlockSpec((1, 1, tn), lambda l, j: (l, 0, j))],
        out_specs=pl.BlockSpec((1, 16, tn), lambda l, j: (l, 0, j)),
        out_shape=jax.ShapeDtypeStruct((depth, 16, n), F32),
        compiler_params=_cparams(("arbitrary", "arbitrary")),
        name="adaln",
    )(cvec, ada_w, ada_b.reshape(depth, 1, n))


def _inproj_kernel(*refs, split_input):
    if split_input:
        x_ref, ctx_ref, mod_ref, nw_ref, w_ref = refs[:5]
        outs = refs[5:]
        xt = jnp.where(pl.program_id(1) == 0, ctx_ref[0], x_ref[0])
    else:
        x_ref, mod_ref, nw_ref, w_ref = refs[:4]
        outs = refs[4:]
        xt = x_ref[0]
    h = _rms(xt, nw_ref[...]) * (1.0 + mod_ref[0, 0, 1:2, :]) + mod_ref[0, 0, 0:1, :]
    hb = h.astype(BF16)
    for k in range(7):
        outs[k][0] = jnp.dot(hb, w_ref[:, k * M_WIDTH:(k + 1) * M_WIDTH],
                             preferred_element_type=F32).astype(BF16)
    outs[7][0] = jnp.dot(hb, w_ref[:, 7 * M_WIDTH:], preferred_element_type=F32)


def _stream_specs(split_input):
    tm = TOK_TILE
    if split_input:
        return [pl.BlockSpec((1, tm, D_MODEL), lambda b, j: (b, jnp.maximum(j - 1, 0), 0)),
                pl.BlockSpec((1, tm, D_MODEL), lambda b, j: (b, 0, 0))]
    return [pl.BlockSpec((1, tm, D_MODEL), lambda b, j: (b, j, 0))]


def _mod_spec():
    return pl.BlockSpec((1, 1, 6, D_MODEL), lambda b, j: (b, jnp.minimum(j, 1), 0, 0))


def _inproj(stream, mod, norm_w, w_proj, B, S):
    split_input = len(stream) == 2
    tm = TOK_TILE
    tok = lambda w: pl.BlockSpec((1, tm, w), lambda b, j: (b, j, 0))
    out_shape = [jax.ShapeDtypeStruct((B, S, M_WIDTH), BF16)] * 7 + [jax.ShapeDtypeStruct((B, S, G_PAD), F32)]
    return pl.pallas_call(
        functools.partial(_inproj_kernel, split_input=split_input),
        grid=(B, S // tm),
        in_specs=_stream_specs(split_input) + [
            _mod_spec(),
            pl.BlockSpec((1, D_MODEL), lambda b, j: (0, 0)),
            pl.BlockSpec((D_MODEL, N_PROJ), lambda b, j: (0, 0))],
        out_specs=[tok(M_WIDTH)] * 7 + [tok(G_PAD)],
        out_shape=out_shape,
        compiler_params=_cparams(("arbitrary", "arbitrary")),
        name="inproj",
    )(*stream, mod, norm_w.reshape(1, D_MODEL), w_proj)


def _prep_kernel(q_ref, qp_ref, qn_ref, k_ref, kp_ref, kn_ref, cw_ref, cb_ref, cos_ref, sin_ref,
                 qo_ref, ko_ref, *, n_tiles):
    j = pl.program_id(1)
    tp = TOK_TILE
    has_prev = jnp.logical_and(j != 0, j != 1).astype(F32)
    has_next = jnp.logical_and(j != 0, j != n_tiles - 1).astype(F32)
    rows = lax.broadcasted_iota(jnp.int32, (tp, M_WIDTH), 0)
    lanes = lax.broadcasted_iota(jnp.int32, (tp, M_WIDTH), 1)
    low_half = (lanes % (ROPE_AXIS_DIM)) < (ROPE_AXIS_DIM // 2)
    cosv = jnp.concatenate([cos_ref[...]] * M_HEADS, axis=1)
    sinv = jnp.concatenate([sin_ref[...]] * M_HEADS, axis=1)

    def branch(x_ref, p_ref, n_ref, col0):
        x = x_ref[0].astype(F32)
        prev_row = p_ref[0, HALO - 1:HALO, :].astype(F32) * has_prev
        next_row = n_ref[0, 0:1, :].astype(F32) * has_next
        xm = jnp.where(rows == 0, prev_row, pltpu.roll(x, 1, 0))
        xp = jnp.where(rows == tp - 1, next_row, pltpu.roll(x, tp - 1, 0))
        w = cw_ref[:, col0:col0 + M_WIDTH]
        y = cb_ref[:, col0:col0 + M_WIDTH] + xm * w[0:1] + x * w[1:2] + xp * w[2:3]
        y = y * jax.nn.sigmoid(y)
        half = ROPE_AXIS_DIM // 2
        partner = jnp.where(low_half, pltpu.roll(y, M_WIDTH - half, 1), pltpu.roll(y, half, 1))
        return y * cosv + partner * sinv

    qo_ref[0] = branch(q_ref, qp_ref, qn_ref, 0).astype(BF16)
    ko_ref[0] = (branch(k_ref, kp_ref, kn_ref, M_WIDTH) * (M_HEAD_DIM ** -0.5)).T.astype(BF16)


def _prep(mq, mk, conv_w, conv_b, cos_t, sin_t, B, S):
    tp = TOK_TILE
    n_tiles = S // tp
    per = tp // HALO
    n_halo = S // HALO
    main = pl.BlockSpec((1, tp, M_WIDTH), lambda b, j: (b, j, 0))
    prev = pl.BlockSpec((1, HALO, M_WIDTH), lambda b, j: (b, jnp.maximum(j * per - 1, 0), 0))
    nxt = pl.BlockSpec((1, HALO, M_WIDTH), lambda b, j: (b, jnp.minimum((j + 1) * per, n_halo - 1), 0))
    return pl.pallas_call(
        functools.partial(_prep_kernel, n_tiles=n_tiles),
        grid=(B, n_tiles),
        in_specs=[main, prev, nxt, main, prev, nxt,
                  pl.BlockSpec((3, 2 * M_WIDTH), lambda b, j: (0, 0)),
                  pl.BlockSpec((1, 2 * M_WIDTH), lambda b, j: (0, 0)),
                  pl.BlockSpec((tp, M_HEAD_DIM), lambda b, j: (j, 0)),
                  pl.BlockSpec((tp, M_HEAD_DIM), lambda b, j: (j, 0))],
        out_specs=[main, pl.BlockSpec((1, M_WIDTH, tp), lambda b, j: (b, 0, j))],
        out_shape=[jax.ShapeDtypeStruct((B, S, M_WIDTH), BF16), jax.ShapeDtypeStruct((B, M_WIDTH, S), BF16)],
        compiler_params=_cparams(("arbitrary", "arbitrary")),
        name="mlstm_prep",
    )(mq, mq, mq, mk, mk, mk, conv_w, conv_b.reshape(1, -1), cos_t, sin_t)


def _scan_rows(x, reverse, op, fill):
    n = x.shape[0]
    rows = lax.broadcasted_iota(jnp.int32, x.shape, 0)
    sh = 1
    while sh < n:
        if reverse:
            x = op(x, jnp.where(rows < n - sh, pltpu.roll(x, n - sh, 0), fill))
        else:
            x = op(x, jnp.where(rows >= sh, pltpu.roll(x, sh, 0), fill))
        sh *= 2
    return x


def _log_sigmoid(x):
    return jnp.minimum(x, 0.0) - jnp.log(1.0 + jnp.exp(-jnp.abs(x)))


def _mlstm_kernel(qf_ref, kf_ref, vf_ref, gf_ref, qb_ref, kb_ref, vb_ref, gb_ref, gbias_ref,
                  hf_ref, hb_ref, cn_ref, m_ref):
    c = pl.program_id(1)

    @pl.when(c == 0)
    def _():
        cn_ref[...] = jnp.zeros_like(cn_ref)
        m_ref[...] = jnp.zeros_like(m_ref)

    t = CHUNK
    r_i = lax.broadcasted_iota(jnp.int32, (t, t), 0)
    c_i = lax.broadcasted_iota(jnp.int32, (t, t), 1)
    ones_blk = jnp.ones((t, LANES), BF16)

    for d, (q_ref, k_ref, v_ref, g_ref, h_ref) in enumerate(
            ((qf_ref, kf_ref, vf_ref, gf_ref, hf_ref), (qb_ref, kb_ref, vb_ref, gb_ref, hb_ref))):
        reverse = d == 1
        mask = (c_i >= r_i) if reverse else (c_i <= r_i)
        end_row = 0 if reverse else t - 1
        g = g_ref[0] + gbias_ref[...]
        bc = _scan_rows(_log_sigmoid(g), reverse, jnp.add, 0.0)
        b_al = pltpu.roll(bc, LANES - M_HEADS, 1)
        a = g - b_al
        m_old = m_ref[d][0:1, :]
        mx = jnp.maximum(m_old, _scan_rows(a, reverse, jnp.maximum, NEG))
        m_end = mx[end_row:end_row + 1, :]
        c_t = -mx
        e_den = jnp.exp(-(b_al + mx))
        w_old = jnp.exp(m_old - m_end)
        a_t = a.T
        w_tok_t = jnp.exp(a - m_end).T
        m_ref[d] = jnp.broadcast_to(b_al[end_row:end_row + 1, :] + m_end, (8, LANES))
        for h in range(M_HEADS):
            icol = 2 * M_HEADS * d + h
            hs = slice(h * M_HEAD_DIM, (h + 1) * M_HEAD_DIM)
            q = q_ref[0, :, hs]
            k_t = k_ref[0, hs, :]
            v_ext = jnp.concatenate([v_ref[0, :, hs], ones_blk], axis=1)
            cn_old = cn_ref[d, h]

            c_b = jnp.broadcast_to(c_t[:, icol:icol + 1], (t, t))
            expo = jnp.concatenate([jnp.where(mask, c_b + a_t[icol:icol + 1, :], NEG),
                                    c_b + m_old[:, icol:icol + 1]], axis=1)
            qk = jnp.dot(q, k_t, preferred_element_type=F32)
            s_ext = (jnp.concatenate([qk, q.astype(F32)], axis=1) * jnp.exp(expo)).astype(BF16)
            rhs = jnp.concatenate([v_ext, cn_old.astype(BF16)], axis=0)
            ext = jnp.dot(s_ext, rhs, preferred_element_type=F32)
            den = jnp.maximum(jnp.abs(ext[:, M_HEAD_DIM:]),
                              jnp.broadcast_to(e_den[:, icol:icol + 1], (t, LANES)))
            h_ref[0, :, hs] = (ext[:, :M_HEAD_DIM] / den).astype(BF16)

            kw_t = (k_t.astype(F32) * w_tok_t[icol:icol + 1, :]).astype(BF16)
            cn_ref[d, h] = (w_old[:, icol:icol + 1] * cn_old
                            + jnp.dot(kw_t, v_ext, preferred_element_type=F32))


def _mlstm(q, k, v, g, gate_b, B, S, n_ctx_chunks):
    nc = S // CHUNK
    ncx = n_ctx_chunks

    def bwd_chunk(c):
        return jnp.where(c < ncx, ncx - 1 - c, nc - 1 + ncx - c)

    fw = lambda w: pl.BlockSpec((1, CHUNK, w), lambda b, c: (b, c, 0))
    bw = lambda w: pl.BlockSpec((1, CHUNK, w), lambda b, c: (b, bwd_chunk(c), 0))
    fw_t = pl.BlockSpec((1, M_WIDTH, CHUNK), lambda b, c: (b, 0, c))
    bw_t = pl.BlockSpec((1, M_WIDTH, CHUNK), lambda b, c: (b, 0, bwd_chunk(c)))
    gbias = jnp.zeros((1, G_PAD), F32).at[0, :G_COLS].set(gate_b)
    return pl.pallas_call(
        _mlstm_kernel,
        grid=(B, nc),
        in_specs=[fw(M_WIDTH), fw_t, fw(M_WIDTH), fw(G_PAD),
                  bw(M_WIDTH), bw_t, bw(M_WIDTH), bw(G_PAD),
                  pl.BlockSpec((1, G_PAD), lambda b, c: (0, 0))],
        out_specs=[fw(M_WIDTH), bw(M_WIDTH)],
        out_shape=[jax.ShapeDtypeStruct((B, S, M_WIDTH), BF16)] * 2,
        scratch_shapes=[pltpu.VMEM((2, M_HEADS, M_HEAD_DIM, 2 * LANES), F32),
                        pltpu.VMEM((2, 8, LANES), F32)],
        compiler_params=_cparams(("arbitrary", "arbitrary")),
        name="mlstm_scan",
    )(q, k, v, g, q, k, v, g, gbias)


def _natten_kernel(typ_ref, st_ref, q_ref, k_ref, v_ref, bias_ref, o_ref, *, n_ctx):
    j = pl.program_id(2)
    sub = NA_QROWS * GRID_W
    lane = lax.broadcasted_iota(jnp.int32, (sub, LANES), 1)
    kctx = k_ref[0, 0:n_ctx, :]
    vctx = v_ref[0, 0:n_ctx, :]
    for i in range(TOK_TILE // sub):
        sb = j * (TOK_TILE // sub) + i
        typ = typ_ref[sb]
        st = pl.multiple_of(st_ref[sb], GRID_W)
        q = q_ref[0, i * sub:(i + 1) * sub, :] * (N_HEAD_DIM ** -0.5)
        kwin = k_ref[0, pl.ds(st, NA_WKEYS), :]
        vwin = v_ref[0, pl.ds(st, NA_WKEYS), :]
        first = lane < N_HEAD_DIM
        zero = jnp.zeros_like(q)
        q2 = jnp.concatenate([jnp.where(first, q, zero), jnp.where(first, zero, q)], axis=0)
        s_win = _dot_nt(q2, kwin) + bias_ref[typ].reshape(2 * sub, NA_WKEYS)
        s_ctx = _dot_nt(q2, kctx)
        m = jnp.maximum(jnp.max(s_win, axis=-1, keepdims=True), jnp.max(s_ctx, axis=-1, keepdims=True))
        p_win = jnp.exp(s_win - m)
        p_ctx = jnp.exp(s_ctx - m)
        l = jnp.sum(p_win, axis=-1, keepdims=True) + jnp.sum(p_ctx, axis=-1, keepdims=True)
        o = (jnp.dot(p_win.astype(BF16), vwin, preferred_element_type=F32)
             + jnp.dot(p_ctx.astype(BF16), vctx, preferred_element_type=F32)) / l
        o_ref[0, i * sub:(i + 1) * sub, :] = jnp.where(first, o[:sub], o[sub:]).astype(BF16)


def _natten_tables(rpb, rows, n_ctx):
    n_sb = rows // NA_QROWS
    reps = (0, 1, 2, n_sb - 2, n_sb - 1)
    ws_of = lambda sb: int(np.clip(NA_QROWS * sb - WIN_H // 2, 0, rows - NA_WROWS))
    n_dr = 2 * WIN_H - 1
    n_dc = 2 * WIN_W - 1
    lead = GRID_W - WIN_W
    vec = jnp.pad(rpb, ((0, 0), (0, 0), (lead, 2 * GRID_W - lead - n_dc)), constant_values=NEG)
    skew = jnp.broadcast_to(vec[:, :, None, :], (N_HEADS, n_dr, GRID_W, 2 * GRID_W))
    skew = skew.reshape(N_HEADS, n_dr, -1)[:, :, :GRID_W * (2 * GRID_W - 1)]
    toep = skew.reshape(N_HEADS, n_dr, GRID_W, 2 * GRID_W - 1)[..., GRID_W - 1:]
    cq = np.arange(GRID_W)[:, None]
    ck = np.arange(GRID_W)[None, :]
    cs = np.clip(cq - WIN_W // 2, 0, GRID_W - WIN_W)
    toep = jnp.where((ck >= cs) & (ck < cs + WIN_W), toep, NEG)
    masked_tile = jnp.full((N_HEADS, GRID_W, GRID_W), NEG, F32)
    types = []
    for sb in reps:
        ws = ws_of(sb)
        q_rows = []
        for qr in range(NA_QROWS):
            r = NA_QROWS * sb + qr
            rs = int(np.clip(r - WIN_H // 2, 0, rows - WIN_H))
            tiles = []
            for kw in range(NA_WROWS):
                kr = ws + kw
                tiles.append(toep[:, kr - r + WIN_H - 1] if rs <= kr < rs + WIN_H else masked_tile)
            q_rows.append(jnp.concatenate(tiles, axis=-1))
        types.append(jnp.concatenate(q_rows, axis=-2))
    types.append(jnp.full_like(types[0], NEG))
    bias = jnp.stack(types, axis=0)
    typ, st = [5] * (n_ctx // (NA_QROWS * GRID_W)), [n_ctx] * (n_ctx // (NA_QROWS * GRID_W))
    for sb in range(n_sb):
        ws = ws_of(sb)
        typ.append((NA_QROWS * sb - ws) // 2)
        st.append(n_ctx + ws * GRID_W)
    return jnp.asarray(typ, jnp.int32), jnp.asarray(st, jnp.int32), bias


def _natten(nq, nk, nv, typ, st, bias, B, S, n_ctx):
    n_pairs = N_HEADS // 2
    grid_spec = pltpu.PrefetchScalarGridSpec(
        num_scalar_prefetch=2,
        grid=(n_pairs, B, S // TOK_TILE),
        in_specs=[pl.BlockSpec((1, TOK_TILE, LANES), lambda p, b, j, *_: (b, j, p)),
                  pl.BlockSpec((1, S, LANES), lambda p, b, j, *_: (b, 0, p)),
                  pl.BlockSpec((1, S, LANES), lambda p, b, j, *_: (b, 0, p)),
                  pl.BlockSpec((6, 2, NA_QROWS * GRID_W, NA_WKEYS), lambda p, b, j, *_: (0, p, 0, 0))],
        out_specs=pl.BlockSpec((1, TOK_TILE, LANES), lambda p, b, j, *_: (b, j, p)),
    )
    return pl.pallas_call(
        functools.partial(_natten_kernel, n_ctx=n_ctx),
        grid_spec=grid_spec,
        out_shape=jax.ShapeDtypeStruct((B, S, N_WIDTH), BF16),
        compiler_params=_cparams(("arbitrary", "arbitrary", "arbitrary")),
        name="natten",
    )(typ, st, nq, nk, nv, bias)


def _route(logits_t, rb):
    e_i = lax.broadcasted_iota(jnp.int32, logits_t.shape, 0)
    z = logits_t - jnp.max(logits_t, axis=0, keepdims=True)
    ez = jnp.exp(z)
    scores = ez / jnp.sum(ez, axis=0, keepdims=True)
    sel = scores + rb
    best = None
    best_score = None
    for gi in range(N_GROUPS):
        r = [sel[gi * EXPERTS_PER_GROUP + u:gi * EXPERTS_PER_GROUP + u + 1, :] for u in range(EXPERTS_PER_GROUP)]
        gs = None
        for u in range(EXPERTS_PER_GROUP):
            for w in range(u + 1, EXPERTS_PER_GROUP):
                pair = r[u] + r[w]
                gs = pair if gs is None else jnp.maximum(gs, pair)
        if best is None:
            best, best_score = jnp.zeros(gs.shape, jnp.int32), gs
        else:
            better = gs > best_score
            best = jnp.where(better, gi, best)
            best_score = jnp.where(better, gs, best_score)
    masked = jnp.where((e_i // EXPERTS_PER_GROUP) == best, sel, -jnp.inf)
    v1 = jnp.max(masked, axis=0, keepdims=True)
    i1 = jnp.min(jnp.where(masked == v1, e_i, N_EXPERTS), axis=0, keepdims=True)
    masked2 = jnp.where(e_i == i1, -jnp.inf, masked)
    v2 = jnp.max(masked2, axis=0, keepdims=True)
    i2 = jnp.min(jnp.where(masked2 == v2, e_i, N_EXPERTS), axis=0, keepdims=True)
    w1 = jnp.sum(jnp.where(e_i == i1, scores, 0.0), axis=0, keepdims=True)
    w2 = jnp.sum(jnp.where(e_i == i2, scores, 0.0), axis=0, keepdims=True)
    tot = w1 + w2
    return jnp.where(e_i == i1, w1 / tot, 0.0) + jnp.where(e_i == i2, w2 / tot, 0.0)


def _outproj_kernel(*refs, split_input):
    n_stream = 2 if split_input else 1
    stream = refs[:n_stream]
    (hf_ref, hb_ref, mo_ref, no_ref, mod_ref, mnw_ref, wo_ref, n2w_ref, rw_ref, rb_ref,
     xo_ref, hx_ref, gate_ref) = refs[n_stream:]
    if split_input:
        xt = jnp.where(pl.program_id(1) == 0, stream[1][0], stream[0][0])
    else:
        xt = stream[0][0]
    hsum = hf_ref[0].astype(F32) + hb_ref[0].astype(F32)
    parts = []
    for h in range(M_HEADS):
        hs = slice(h * M_HEAD_DIM, (h + 1) * M_HEAD_DIM)
        parts.append(_rms(hsum[:, hs], mnw_ref[:, hs]))
    m_out = jnp.concatenate(parts, axis=1) * jax.nn.sigmoid(mo_ref[0].astype(F32))
    y = (jnp.dot(m_out.astype(BF16), wo_ref[0:M_WIDTH, :], preferred_element_type=F32)
         + jnp.dot(no_ref[0], wo_ref[M_WIDTH:, :], preferred_element_type=F32))
    x_new = xt + mod_ref[0, 0, 2:3, :] * y
    xo_ref[0] = x_new
    hx = _rms(x_new, n2w_ref[...]) * (1.0 + mod_ref[0, 0, 4:5, :]) + mod_ref[0, 0, 3:4, :]
    hx_ref[0] = hx.astype(BF16)
    hh, hl = _split_bf16(hx)
    rh, rl = _split_bf16(rw_ref[...])
    logits_t = _dot_nt(rh, hh) + (_dot_nt(rh, hl) + _dot_nt(rl, hh))
    gate_t = _route(logits_t, rb_ref[...])
    pad = jnp.zeros((LANES - N_EXPERTS, gate_t.shape[1]), F32)
    gate_ref[0] = jnp.concatenate([gate_t, pad], axis=0).T


def _outproj(stream, hf, hb, mo, n_out, mod, mnorm_w, w_out, norm2_w, router_wt, router_b, B, S):
    split_input = len(stream) == 2
    tm = TOK_TILE
    tok = lambda w: pl.BlockSpec((1, tm, w), lambda b, j: (b, j, 0))
    const = lambda shape: pl.BlockSpec(shape, lambda b, j: (0,) * len(shape))
    return pl.pallas_call(
        functools.partial(_outproj_kernel, split_input=split_input),
        grid=(B, S // tm),
        in_specs=_stream_specs(split_input) + [
            tok(M_WIDTH), tok(M_WIDTH), tok(M_WIDTH), tok(N_WIDTH), _mod_spec(),
            const((1, M_WIDTH)), const((D_MODEL, D_MODEL)), const((1, D_MODEL)),
            const((N_EXPERTS, D_MODEL)), const((N_EXPERTS, 1))],
        out_specs=[tok(D_MODEL), tok(D_MODEL), tok(LANES)],
        out_shape=[jax.ShapeDtypeStruct((B, S, D_MODEL), F32),
                   jax.ShapeDtypeStruct((B, S, D_MODEL), BF16),
                   jax.ShapeDtypeStruct((B, S, LANES), F32)],
        compiler_params=_cparams(("arbitrary", "arbitrary")),
        name="outproj_router",
    )(*stream, hf, hb, mo, n_out, mod, mnorm_w.reshape(1, -1), w_out, norm2_w.reshape(1, -1),
      router_wt, router_b.reshape(-1, 1))


def _moe_kernel(h_ref, gate_ref, x_ref, mod_ref, w1_ref, w3_ref, w2_ref, fw_ref, o_ref, acc_ref,
                *, n_ctx, final_norm):
    e = pl.program_id(2)

    @pl.when(e == 0)
    def _():
        acc_ref[...] = jnp.zeros_like(acc_ref)

    h = h_ref[0]
    a = jnp.dot(h, w1_ref[0], preferred_element_type=F32)
    b3 = jnp.dot(h, w3_ref[0], preferred_element_type=F32)
    lane = lax.broadcasted_iota(jnp.int32, gate_ref.shape[1:], 1)
    g_e = jnp.sum(jnp.where(lane == e, gate_ref[0], 0.0), axis=-1, keepdims=True)
    act = (a * jax.nn.sigmoid(a) * b3 * g_e).astype(BF16)
    acc_ref[...] += jnp.dot(act, w2_ref[0], preferred_element_type=F32)

    @pl.when(e == N_EXPERTS - 1)
    def _():
        tm = acc_ref.shape[0]
        row = pl.program_id(1) * tm + lax.broadcasted_iota(jnp.int32, (tm, 1), 0)
        gate_mod = jnp.where(row < n_ctx, mod_ref[0, 0, 5:6, :], mod_ref[0, 1, 5:6, :])
        x_new = x_ref[0] + gate_mod * acc_ref[...]
        if final_norm:
            x_new = _rms(x_new, fw_ref[...])
        o_ref[0] = x_new


def _moe(hx, gate, xa, mod, w1, w3, w2, final_w, B, S, n_ctx, final_norm):
    tm = S // 4
    tok = lambda w: pl.BlockSpec((1, tm, w), lambda b, j, e: (b, j, 0))
    return pl.pallas_call(
        functools.partial(_moe_kernel, n_ctx=n_ctx, final_norm=final_norm),
        grid=(B, S // tm, N_EXPERTS),
        in_specs=[tok(D_MODEL), tok(LANES), tok(D_MODEL),
                  pl.BlockSpec((1, 2, 6, D_MODEL), lambda b, j, e: (b, 0, 0, 0)),
                  pl.BlockSpec((1, D_MODEL, MOE_D_FF), lambda b, j, e: (e, 0, 0)),
                  pl.BlockSpec((1, D_MODEL, MOE_D_FF), lambda b, j, e: (e, 0, 0)),
                  pl.BlockSpec((1, MOE_D_FF, D_MODEL), lambda b, j, e: (e, 0, 0)),
                  pl.BlockSpec((1, D_MODEL), lambda b, j, e: (0, 0))],
        out_specs=tok(D_MODEL),
        out_shape=jax.ShapeDtypeStruct((B, S, D_MODEL), F32),
        scratch_shapes=[pltpu.VMEM((tm, D_MODEL), F32)],
        compiler_params=_cparams(("arbitrary", "arbitrary", "arbitrary")),
        name="moe",
    )(hx, gate, xa, mod, w1, w3, w2, final_w.reshape(1, -1))


def _rope_tables(L, n_ctx):
    t = jnp.arange(L)
    row = (t // GRID_W).astype(F32)
    col = (t % GRID_W).astype(F32)
    inv = ROPE_BASE ** (-jnp.arange(0, ROPE_AXIS_DIM, 2, dtype=F32) / ROPE_AXIS_DIM)
    ar = row[:, None] * inv
    ac = col[:, None] * inv
    cos_l = jnp.concatenate([jnp.cos(ar), jnp.cos(ar), jnp.cos(ac), jnp.cos(ac)], axis=1)
    sin_l = jnp.concatenate([-jnp.sin(ar), jnp.sin(ar), -jnp.sin(ac), jnp.sin(ac)], axis=1)
    cos_t = jnp.concatenate([jnp.ones((n_ctx, M_HEAD_DIM), F32), cos_l], axis=0)
    sin_t = jnp.concatenate([jnp.zeros((n_ctx, M_HEAD_DIM), F32), sin_l], axis=0)
    return cos_t, sin_t


def _proj_weight(w_in):
    q_cols = 2 * M_WIDTH + N_WIDTH
    kv0 = q_cols
    g0 = kv0 + 2 * M_WIDTH
    n0 = g0 + G_COLS
    gates = jnp.pad(w_in[:, g0:n0], ((0, 0), (0, G_PAD - G_COLS)))
    return jnp.concatenate([w_in[:, :q_cols], w_in[:, kv0:g0], w_in[:, n0:], gates], axis=1).astype(BF16)


def kernel(x, c, ctx, c_ctx, ada_w, ada_b, norm1_w, w_in, conv_w, conv_b, gate_b, mnorm_w, rpb, w_out,
           norm2_w, router_w, router_b, exp_w1, exp_w3, exp_w2, final_norm_w):
    B, L, _ = x.shape
    n_ctx = ctx.shape[1]
    depth = ada_w.shape[0]
    assert n_ctx == TOK_TILE and L % TOK_TILE == 0 and L % GRID_W == 0
    S = n_ctx + L
    rows = L // GRID_W
    assert rows >= NA_WROWS + 4 and rows % NA_QROWS == 0 and (S // 4) % HALO == 0

    cvec = jnp.concatenate([c, c_ctx[None], jnp.zeros((16 - B - 1, D_MODEL), F32)], axis=0)
    mods = _adaln(cvec, ada_w, ada_b)
    cos_t, sin_t = _rope_tables(L, n_ctx)
    router_wt = router_w.T

    stream = (x, ctx)
    out = None
    for layer in range(depth):
        last = layer == depth - 1
        m6 = mods[layer].reshape(16, 6, D_MODEL)
        mod = jnp.stack([jnp.broadcast_to(m6[B], (B, 6, D_MODEL)), m6[:B]], axis=1)
        mq, mo, nq, mk, mv, nk, nv, g = _inproj(stream, mod, norm1_w[layer], _proj_weight(w_in[layer]), B, S)
        q, k = _prep(mq, mk, conv_w[layer], conv_b[layer], cos_t, sin_t, B, S)
        hf, hb = _mlstm(q, k, mv, g, gate_b[layer], B, S, n_ctx // CHUNK)
        typ, st, bias = _natten_tables(rpb[layer], rows, n_ctx)
        n_out = _natten(nq, nk, nv, typ, st, bias, B, S, n_ctx)
        xa, hx, gate = _outproj(stream, hf, hb, mo, n_out, mod, mnorm_w[layer], w_out[layer].astype(BF16),
                                norm2_w[layer], router_wt, router_b, B, S)
        out = _moe(hx, gate, xa, mod, exp_w1[layer].astype(BF16), exp_w3[layer].astype(BF16),
                   exp_w2[layer].astype(BF16), final_norm_w, B, S, n_ctx, last)
        stream = (out,)
    return out[:, n_ctx:]
```

```python
import functools

import numpy as np
import jax
import jax.numpy as jnp
from jax import lax
from jax.experimental import pallas as pl
from jax.experimental.pallas import tpu as pltpu

D_MODEL = 1024
M_WIDTH = 512
M_HEADS = 4
M_HEAD_DIM = 128
N_WIDTH = 512
N_HEADS = 8
N_HEAD_DIM = 64
G_COLS = 16
GRID_W = 64
WIN_H = 8
WIN_W = 16
CHUNK = 128
ROPE_AXIS_DIM = 64
ROPE_BASE = 10000.0
N_EXPERTS = 16
N_GROUPS = 4
EXPERTS_PER_GROUP = 4
MOE_D_FF = 512
NORM_EPS = 1e-6

LANES = 128
TOK_TILE = 256
HALO = 16
NA_QROWS = 2
NA_WROWS = 10
NA_WKEYS = NA_WROWS * GRID_W
NEG = -1e30
G_PAD = LANES
N_PROJ = 7 * M_WIDTH + G_PAD
REC_W = D_MODEL + LANES
MOE_TILE = 1024
VMEM_LIMIT = 56 * 1024 * 1024

F32 = jnp.float32
BF16 = jnp.bfloat16


def _cparams(sem):
    return pltpu.CompilerParams(dimension_semantics=sem, vmem_limit_bytes=VMEM_LIMIT)


def _split_bf16(a):
    hi = a.astype(BF16)
    lo = (a - hi.astype(F32)).astype(BF16)
    return hi, lo


def _dot3(a, b):
    ah, al = _split_bf16(a)
    bh, bl = _split_bf16(b)
    d = functools.partial(jnp.dot, preferred_element_type=F32)
    return d(ah, bh) + (d(al, bh) + d(ah, bl))


def _dot_nt(a, b):
    return lax.dot_general(a, b, (((1,), (1,)), ((), ())), preferred_element_type=F32)


def _rms(x, w):
    ms = jnp.mean(x * x, axis=-1, keepdims=True)
    return x * lax.rsqrt(ms + NORM_EPS) * w


def _adaln_kernel(c_ref, w_ref, b_ref, o_ref):
    c = c_ref[...]
    s = c * jax.nn.sigmoid(c)
    o_ref[0] = _dot3(s, w_ref[0]) + b_ref[0]


def _adaln(cvec, ada_w, ada_b):
    depth = ada_w.shape[0]
    n = ada_w.shape[2]
    tn = D_MODEL
    return pl.pallas_call(
        _adaln_kernel,
        grid=(depth, n // tn),
        in_specs=[pl.BlockSpec((16, D_MODEL), lambda l, j: (0, 0)),
                  pl.BlockSpec((1, D_MODEL, tn), lambda l, j: (l, 0, j)),
                  pl.BlockSpec((1, 1, tn), lambda l, j: (l, 0, j))],
        out_specs=pl.BlockSpec((1, 16, tn), lambda l, j: (l, 0, j)),
        out_shape=jax.ShapeDtypeStruct((depth, 16, n), F32),
        compiler_params=_cparams(("arbitrary", "arbitrary")),
        name="adaln",
    )(cvec, ada_w, ada_b.reshape(depth, 1, n))


def _inproj_kernel(*refs, split_input):
    if split_input:
        x_ref, ctx_ref, mod_ref, nw_ref, w_ref = refs[:5]
        outs = refs[5:]
        xt = jnp.where(pl.program_id(1) == 0, ctx_ref[0], x_ref[0])
    else:
        x_ref, mod_ref, nw_ref, w_ref = refs[:4]
        outs = refs[4:]
        xt = x_ref[0]
    h = _rms(xt, nw_ref[...]) * (1.0 + mod_ref[0, 0, 1:2, :]) + mod_ref[0, 0, 0:1, :]
    hb = h.astype(BF16)
    for k in range(7):
        outs[k][0] = jnp.dot(hb, w_ref[:, k * M_WIDTH:(k + 1) * M_WIDTH],
                             preferred_element_type=F32).astype(BF16)
    outs[7][0] = jnp.dot(hb, w_ref[:, 7 * M_WIDTH:], preferred_element_type=F32)


def _stream_specs(split_input):
    tm = TOK_TILE
    if split_input:
        return [pl.BlockSpec((1, tm, D_MODEL), lambda b, j: (b, jnp.maximum(j - 1, 0), 0)),
                pl.BlockSpec((1, tm, D_MODEL), lambda b, j: (b, 0, 0))]
    return [pl.BlockSpec((1, tm, D_MODEL), lambda b, j: (b, j, 0))]


def _mod_spec():
    return pl.BlockSpec((1, 1, 6, D_MODEL), lambda b, j: (b, jnp.minimum(j, 1), 0, 0))


def _inproj(stream, mod, norm_w, w_proj, B, S):
    split_input = len(stream) == 2
    tm = TOK_TILE
    tok = lambda w: pl.BlockSpec((1, tm, w), lambda b, j: (b, j, 0))
    out_shape = [jax.ShapeDtypeStruct((B, S, M_WIDTH), BF16)] * 7 + [jax.ShapeDtypeStruct((B, S, G_PAD), F32)]
    return pl.pallas_call(
        functools.partial(_inproj_kernel, split_input=split_input),
        grid=(B, S // tm),
        in_specs=_stream_specs(split_input) + [
            _mod_spec(),
            pl.BlockSpec((1, D_MODEL), lambda b, j: (0, 0)),
            pl.BlockSpec((D_MODEL, N_PROJ), lambda b, j: (0, 0))],
        out_specs=[tok(M_WIDTH)] * 7 + [tok(G_PAD)],
        out_shape=out_shape,
        compiler_params=_cparams(("arbitrary", "arbitrary")),
        name="inproj",
    )(*stream, mod, norm_w.reshape(1, D_MODEL), w_proj)


def _prep_kernel(q_ref, qp_ref, qn_ref, k_ref, kp_ref, kn_ref, cw_ref, cb_ref, cos_ref, sin_ref,
                 qo_ref, ko_ref, *, n_tiles):
    j = pl.program_id(1)
    tp = TOK_TILE
    has_prev = jnp.logical_and(j != 0, j != 1).astype(F32)
    has_next = jnp.logical_and(j != 0, j != n_tiles - 1).astype(F32)
    rows = lax.broadcasted_iota(jnp.int32, (tp, M_WIDTH), 0)
    lanes = lax.broadcasted_iota(jnp.int32, (tp, M_WIDTH), 1)
    low_half = (lanes % (ROPE_AXIS_DIM)) < (ROPE_AXIS_DIM // 2)
    cosv = jnp.concatenate([cos_ref[...]] * M_HEADS, axis=1)
    sinv = jnp.concatenate([sin_ref[...]] * M_HEADS, axis=1)

    def branch(x_ref, p_ref, n_ref, col0):
        x = x_ref[0].astype(F32)
        prev_row = p_ref[0, HALO - 1:HALO, :].astype(F32) * has_prev
        next_row = n_ref[0, 0:1, :].astype(F32) * has_next
        xm = jnp.where(rows == 0, prev_row, pltpu.roll(x, 1, 0))
        xp = jnp.where(rows == tp - 1, next_row, pltpu.roll(x, tp - 1, 0))
        w = cw_ref[:, col0:col0 + M_WIDTH]
        y = cb_ref[:, col0:col0 + M_WIDTH] + xm * w[0:1] + x * w[1:2] + xp * w[2:3]
        y = y * jax.nn.sigmoid(y)
        half = ROPE_AXIS_DIM // 2
        partner = jnp.where(low_half, pltpu.roll(y, M_WIDTH - half, 1), pltpu.roll(y, half, 1))
        return y * cosv + partner * sinv

    qo_ref[0] = branch(q_ref, qp_ref, qn_ref, 0).astype(BF16)
    ko_ref[0] = (branch(k_ref, kp_ref, kn_ref, M_WIDTH) * (M_HEAD_DIM ** -0.5)).T.astype(BF16)


def _prep(mq, mk, conv_w, conv_b, cos_t, sin_t, B, S):
    tp = TOK_TILE
    n_tiles = S // tp
    per = tp // HALO
    n_halo = S // HALO
    main = pl.BlockSpec((1, tp, M_WIDTH), lambda b, j: (b, j, 0))
    prev = pl.BlockSpec((1, HALO, M_WIDTH), lambda b, j: (b, jnp.maximum(j * per - 1, 0), 0))
    nxt = pl.BlockSpec((1, HALO, M_WIDTH), lambda b, j: (b, jnp.minimum((j + 1) * per, n_halo - 1), 0))
    return pl.pallas_call(
        functools.partial(_prep_kernel, n_tiles=n_tiles),
        grid=(B, n_tiles),
        in_specs=[main, prev, nxt, main, prev, nxt,
                  pl.BlockSpec((3, 2 * M_WIDTH), lambda b, j: (0, 0)),
                  pl.BlockSpec((1, 2 * M_WIDTH), lambda b, j: (0, 0)),
                  pl.BlockSpec((tp, M_HEAD_DIM), lambda b, j: (j, 0)),
                  pl.BlockSpec((tp, M_HEAD_DIM), lambda b, j: (j, 0))],
        out_specs=[main, pl.BlockSpec((1, M_WIDTH, tp), lambda b, j: (b, 0, j))],
        out_shape=[jax.ShapeDtypeStruct((B, S, M_WIDTH), BF16), jax.ShapeDtypeStruct((B, M_WIDTH, S), BF16)],
        compiler_params=_cparams(("arbitrary", "arbitrary")),
        name="mlstm_prep",
    )(mq, mq, mq, mk, mk, mk, conv_w, conv_b.reshape(1, -1), cos_t, sin_t)


def _scan_rows(x, reverse, op, fill):
    n = x.shape[0]
    rows = lax.broadcasted_iota(jnp.int32, x.shape, 0)
    sh = 1
    while sh < n:
        if reverse:
            x = op(x, jnp.where(rows < n - sh, pltpu.roll(x, n - sh, 0), fill))
        else:
            x = op(x, jnp.where(rows >= sh, pltpu.roll(x, sh, 0), fill))
        sh *= 2
    return x


def _log_sigmoid(x):
    return jnp.minimum(x, 0.0) - jnp.log(1.0 + jnp.exp(-jnp.abs(x)))


def _mlstm_kernel(qf_ref, kf_ref, vf_ref, gf_ref, qb_ref, kb_ref, vb_ref, gb_ref, gbias_ref,
                  hf_ref, hb_ref, cn_ref, m_ref):
    c = pl.program_id(1)

    @pl.when(c == 0)
    def _():
        cn_ref[...] = jnp.zeros_like(cn_ref)
        m_ref[...] = jnp.zeros_like(m_ref)

    t = CHUNK
    r_i = lax.broadcasted_iota(jnp.int32, (t, t), 0)
    c_i = lax.broadcasted_iota(jnp.int32, (t, t), 1)
    ones_blk = jnp.ones((t, LANES), BF16)

    for d, (q_ref, k_ref, v_ref, g_ref, h_ref) in enumerate(
            ((qf_ref, kf_ref, vf_ref, gf_ref, hf_ref), (qb_ref, kb_ref, vb_ref, gb_ref, hb_ref))):
        reverse = d == 1
        mask = (c_i >= r_i) if reverse else (c_i <= r_i)
        end_row = 0 if reverse else t - 1
        g = g_ref[0] + gbias_ref[...]
        bc = _scan_rows(_log_sigmoid(g), reverse, jnp.add, 0.0)
        b_al = pltpu.roll(bc, LANES - M_HEADS, 1)
        a = g - b_al
        m_old = m_ref[d][0:1, :]
        mx = jnp.maximum(m_old, _scan_rows(a, reverse, jnp.maximum, NEG))
        m_end = mx[end_row:end_row + 1, :]
        c_t = -mx
        e_den = jnp.exp(-(b_al + mx))
        w_old = jnp.exp(m_old - m_end)
        a_t = a.T
        w_tok_t = jnp.exp(a - m_end).T
        m_ref[d] = jnp.broadcast_to(b_al[end_row:end_row + 1, :] + m_end, (8, LANES))
        for h in range(M_HEADS):
            icol = 2 * M_HEADS * d + h
            hs = slice(h * M_HEAD_DIM, (h + 1) * M_HEAD_DIM)
            q = q_ref[0, :, hs]
            k_t = k_ref[0, hs, :]
            v_ext = jnp.concatenate([v_ref[0, :, hs], ones_blk], axis=1)
            cn_old = cn_ref[d, h]

            c_b = jnp.broadcast_to(c_t[:, icol:icol + 1], (t, t))
            expo = jnp.concatenate([jnp.where(mask, c_b + a_t[icol:icol + 1, :], NEG),
                                    c_b + m_old[:, icol:icol + 1]], axis=1)
            qk = jnp.dot(q, k_t, preferred_element_type=F32)
            s_ext = (jnp.concatenate([qk, q.astype(F32)], axis=1) * jnp.exp(expo)).astype(BF16)
            rhs = jnp.concatenate([v_ext, cn_old.astype(BF16)], axis=0)
            ext = jnp.dot(s_ext, rhs, preferred_element_type=F32)
            den = jnp.maximum(jnp.abs(ext[:, M_HEAD_DIM:]),
                              jnp.broadcast_to(e_den[:, icol:icol + 1], (t, LANES)))
            h_ref[0, :, hs] = (ext[:, :M_HEAD_DIM] / den).astype(BF16)

            kw_t = (k_t.astype(F32) * w_tok_t[icol:icol + 1, :]).astype(BF16)
            cn_ref[d, h] = (w_old[:, icol:icol + 1] * cn_old
                            + jnp.dot(kw_t, v_ext, preferred_element_type=F32))


def _mlstm(q, k, v, g, gate_b, B, S, n_ctx_chunks):
    nc = S // CHUNK
    ncx = n_ctx_chunks

    def bwd_chunk(c):
        return jnp.where(c < ncx, ncx - 1 - c, nc - 1 + ncx - c)

    fw = lambda w: pl.BlockSpec((1, CHUNK, w), lambda b, c: (b, c, 0))
    bw = lambda w: pl.BlockSpec((1, CHUNK, w), lambda b, c: (b, bwd_chunk(c), 0))
    fw_t = pl.BlockSpec((1, M_WIDTH, CHUNK), lambda b, c: (b, 0, c))
    bw_t = pl.BlockSpec((1, M_WIDTH, CHUNK), lambda b, c: (b, 0, bwd_chunk(c)))
    gbias = jnp.zeros((1, G_PAD), F32).at[0, :G_COLS].set(gate_b)
    return pl.pallas_call(
        _mlstm_kernel,
        grid=(B, nc),
        in_specs=[fw(M_WIDTH), fw_t, fw(M_WIDTH), fw(G_PAD),
                  bw(M_WIDTH), bw_t, bw(M_WIDTH), bw(G_PAD),
                  pl.BlockSpec((1, G_PAD), lambda b, c: (0, 0))],
        out_specs=[fw(M_WIDTH), bw(M_WIDTH)],
        out_shape=[jax.ShapeDtypeStruct((B, S, M_WIDTH), BF16)] * 2,
        scratch_shapes=[pltpu.VMEM((2, M_HEADS, M_HEAD_DIM, 2 * LANES), F32),
                        pltpu.VMEM((2, 8, LANES), F32)],
        compiler_params=_cparams(("arbitrary", "arbitrary")),
        name="mlstm_scan",
    )(q, k, v, g, q, k, v, g, gbias)


def _natten_kernel(typ_ref, st_ref, q_ref, k_ref, v_ref, bias_ref, o_ref, *, n_ctx):
    j = pl.program_id(2)
    sub = NA_QROWS * GRID_W
    lane = lax.broadcasted_iota(jnp.int32, (sub, LANES), 1)
    kctx = k_ref[0, 0:n_ctx, :]
    vctx = v_ref[0, 0:n_ctx, :]
    for i in range(TOK_TILE // sub):
        sb = j * (TOK_TILE // sub) + i
        typ = typ_ref[sb]
        st = pl.multiple_of(st_ref[sb], GRID_W)
        q = q_ref[0, i * sub:(i + 1) * sub, :] * (N_HEAD_DIM ** -0.5)
        kwin = k_ref[0, pl.ds(st, NA_WKEYS), :]
        vwin = v_ref[0, pl.ds(st, NA_WKEYS), :]
        first = lane < N_HEAD_DIM
        zero = jnp.zeros_like(q)
        q2 = jnp.concatenate([jnp.where(first, q, zero), jnp.where(first, zero, q)], axis=0)
        s_win = _dot_nt(q2, kwin) + bias_ref[typ].reshape(2 * sub, NA_WKEYS)
        s_ctx = _dot_nt(q2, kctx)
        m = jnp.maximum(jnp.max(s_win, axis=-1, keepdims=True), jnp.max(s_ctx, axis=-1, keepdims=True))
        p_win = jnp.exp(s_win - m)
        p_ctx = jnp.exp(s_ctx - m)
        l = jnp.sum(p_win, axis=-1, keepdims=True) + jnp.sum(p_ctx, axis=-1, keepdims=True)
        o = (jnp.dot(p_win.astype(BF16), vwin, preferred_element_type=F32)
             + jnp.dot(p_ctx.astype(BF16), vctx, preferred_element_type=F32)) / l
        o_ref[0, i * sub:(i + 1) * sub, :] = jnp.where(first, o[:sub], o[sub:]).astype(BF16)


def _natten_tables(rpb, rows, n_ctx):
    n_sb = rows // NA_QROWS
    reps = (0, 1, 2, n_sb - 2, n_sb - 1)
    ws_of = lambda sb: int(np.clip(NA_QROWS * sb - WIN_H // 2, 0, rows - NA_WROWS))
    n_dr = 2 * WIN_H - 1
    n_dc = 2 * WIN_W - 1
    lead = GRID_W - WIN_W
    vec = jnp.pad(rpb, ((0, 0), (0, 0), (lead, 2 * GRID_W - lead - n_dc)), constant_values=NEG)
    skew = jnp.broadcast_to(vec[:, :, None, :], (N_HEADS, n_dr, GRID_W, 2 * GRID_W))
    skew = skew.reshape(N_HEADS, n_dr, -1)[:, :, :GRID_W * (2 * GRID_W - 1)]
    toep = skew.reshape(N_HEADS, n_dr, GRID_W, 2 * GRID_W - 1)[..., GRID_W - 1:]
    cq = np.arange(GRID_W)[:, None]
    ck = np.arange(GRID_W)[None, :]
    cs = np.clip(cq - WIN_W // 2, 0, GRID_W - WIN_W)
    toep = jnp.where((ck >= cs) & (ck < cs + WIN_W), toep, NEG)
    masked_tile = jnp.full((N_HEADS, GRID_W, GRID_W), NEG, F32)
    types = []
    for sb in reps:
        ws = ws_of(sb)
        q_rows = []
        for qr in range(NA_QROWS):
            r = NA_QROWS * sb + qr
            rs = int(np.clip(r - WIN_H // 2, 0, rows - WIN_H))
            tiles = []
            for kw in range(NA_WROWS):
                kr = ws + kw
                tiles.append(toep[:, kr - r + WIN_H - 1] if rs <= kr < rs + WIN_H else masked_tile)
            q_rows.append(jnp.concatenate(tiles, axis=-1))
        types.append(jnp.concatenate(q_rows, axis=-2))
    types.append(jnp.full_like(types[0], NEG))
    bias = jnp.stack(types, axis=0)
    typ, st = [5] * (n_ctx // (NA_QROWS * GRID_W)), [n_ctx] * (n_ctx // (NA_QROWS * GRID_W))
    for sb in range(n_sb):
        ws = ws_of(sb)
        typ.append((NA_QROWS * sb - ws) // 2)
        st.append(n_ctx + ws * GRID_W)
    return jnp.asarray(typ, jnp.int32), jnp.asarray(st, jnp.int32), bias


def _natten(nq, nk, nv, typ, st, bias, B, S, n_ctx):
    n_pairs = N_HEADS // 2
    grid_spec = pltpu.PrefetchScalarGridSpec(
        num_scalar_prefetch=2,
        grid=(n_pairs, B, S // TOK_TILE),
        in_specs=[pl.BlockSpec((1, TOK_TILE, LANES), lambda p, b, j, *_: (b, j, p)),
                  pl.BlockSpec((1, S, LANES), lambda p, b, j, *_: (b, 0, p)),
                  pl.BlockSpec((1, S, LANES), lambda p, b, j, *_: (b, 0, p)),
                  pl.BlockSpec((6, 2, NA_QROWS * GRID_W, NA_WKEYS), lambda p, b, j, *_: (0, p, 0, 0))],
        out_specs=pl.BlockSpec((1, TOK_TILE, LANES), lambda p, b, j, *_: (b, j, p)),
    )
    return pl.pallas_call(
        functools.partial(_natten_kernel, n_ctx=n_ctx),
        grid_spec=grid_spec,
        out_shape=jax.ShapeDtypeStruct((B, S, N_WIDTH), BF16),
        compiler_params=_cparams(("arbitrary", "arbitrary", "arbitrary")),
        name="natten",
    )(typ, st, nq, nk, nv, bias)


def _route(logits_t, rb):
    e_i = lax.broadcasted_iota(jnp.int32, logits_t.shape, 0)
    z = logits_t - jnp.max(logits_t, axis=0, keepdims=True)
    ez = jnp.exp(z)
    scores = ez / jnp.sum(ez, axis=0, keepdims=True)
    sel = scores + rb
    best = None
    best_score = None
    for gi in range(N_GROUPS):
        r = [sel[gi * EXPERTS_PER_GROUP + u:gi * EXPERTS_PER_GROUP + u + 1, :] for u in range(EXPERTS_PER_GROUP)]
        gs = None
        for u in range(EXPERTS_PER_GROUP):
            for w in range(u + 1, EXPERTS_PER_GROUP):
                pair = r[u] + r[w]
                gs = pair if gs is None else jnp.maximum(gs, pair)
        if best is None:
            best, best_score = jnp.zeros(gs.shape, jnp.int32), gs
        else:
            better = gs > best_score
            best = jnp.where(better, gi, best)
            best_score = jnp.where(better, gs, best_score)
    masked = jnp.where((e_i // EXPERTS_PER_GROUP) == best, sel, -jnp.inf)
    v1 = jnp.max(masked, axis=0, keepdims=True)
    i1 = jnp.min(jnp.where(masked == v1, e_i, N_EXPERTS), axis=0, keepdims=True)
    masked2 = jnp.where(e_i == i1, -jnp.inf, masked)
    v2 = jnp.max(masked2, axis=0, keepdims=True)
    i2 = jnp.min(jnp.where(masked2 == v2, e_i, N_EXPERTS), axis=0, keepdims=True)
    w1 = jnp.sum(jnp.where(e_i == i1, scores, 0.0), axis=0, keepdims=True)
    w2 = jnp.sum(jnp.where(e_i == i2, scores, 0.0), axis=0, keepdims=True)
    tot = w1 + w2
    return jnp.where(e_i == i1, w1 / tot, 0.0) + jnp.where(e_i == i2, w2 / tot, 0.0), best


def _outproj_kernel(*refs, split_input):
    n_stream = 2 if split_input else 1
    stream = refs[:n_stream]
    (hf_ref, hb_ref, mo_ref, no_ref, mod_ref, mnw_ref, wo_ref, n2w_ref, rw_ref, rb_ref,
     xo_ref, rec_ref, grp_ref, rank_ref, cnt_ref, base_ref) = refs[n_stream:]

    @pl.when(jnp.logical_and(pl.program_id(0) == 0, pl.program_id(1) == 0))
    def _():
        base_ref[...] = jnp.zeros_like(base_ref)

    if split_input:
        xt = jnp.where(pl.program_id(1) == 0, stream[1][0], stream[0][0])
    else:
        xt = stream[0][0]
    hsum = hf_ref[0].astype(F32) + hb_ref[0].astype(F32)
    parts = []
    for h in range(M_HEADS):
        hs = slice(h * M_HEAD_DIM, (h + 1) * M_HEAD_DIM)
        parts.append(_rms(hsum[:, hs], mnw_ref[:, hs]))
    m_out = jnp.concatenate(parts, axis=1) * jax.nn.sigmoid(mo_ref[0].astype(F32))
    y = (jnp.dot(m_out.astype(BF16), wo_ref[0:M_WIDTH, :], preferred_element_type=F32)
         + jnp.dot(no_ref[0], wo_ref[M_WIDTH:, :], preferred_element_type=F32))
    x_new = xt + mod_ref[0, 0, 2:3, :] * y
    xo_ref[0] = x_new
    hx = _rms(x_new, n2w_ref[...]) * (1.0 + mod_ref[0, 0, 4:5, :]) + mod_ref[0, 0, 3:4, :]
    rec_ref[0, :, :D_MODEL] = hx
    hh, hl = _split_bf16(hx)
    rh, rl = _split_bf16(rw_ref[...])
    logits_t = _dot_nt(rh, hh) + (_dot_nt(rh, hl) + _dot_nt(rl, hh))
    gate_t, best = _route(logits_t, rb_ref[...])
    pad = jnp.zeros((LANES - N_EXPERTS, gate_t.shape[1]), F32)
    rec_ref[0, :, D_MODEL:] = jnp.concatenate([gate_t, pad], axis=0).T
    tm = best.shape[1]
    onehot = lax.broadcasted_iota(jnp.int32, (8, tm), 0) == best
    upper = (lax.broadcasted_iota(jnp.int32, (tm, tm), 0) <= lax.broadcasted_iota(jnp.int32, (tm, tm), 1))
    csum = jnp.dot(onehot.astype(BF16), upper.astype(BF16), preferred_element_type=F32)
    base = base_ref[:, 0:1]
    rank = jnp.sum(jnp.where(onehot, csum - 1.0 + base, 0.0), axis=0, keepdims=True)
    grp_ref[0, 0] = best
    rank_ref[0, 0] = rank.astype(jnp.int32)
    total = base_ref[...] + csum[:, tm - 1:tm]
    base_ref[...] = total
    cnt_ref[...] = total


def _outproj(stream, hf, hb, mo, n_out, mod, mnorm_w, w_out, norm2_w, router_wt, router_b, B, S):
    split_input = len(stream) == 2
    tm = TOK_TILE
    nt = S // tm
    tok = lambda w: pl.BlockSpec((1, tm, w), lambda b, j: (b, j, 0))
    idx = pl.BlockSpec((1, 1, 1, tm), lambda b, j: (b, j, 0, 0))
    const = lambda shape: pl.BlockSpec(shape, lambda b, j: (0,) * len(shape))
    return pl.pallas_call(
        functools.partial(_outproj_kernel, split_input=split_input),
        grid=(B, nt),
        in_specs=_stream_specs(split_input) + [
            tok(M_WIDTH), tok(M_WIDTH), tok(M_WIDTH), tok(N_WIDTH), _mod_spec(),
            const((1, M_WIDTH)), const((D_MODEL, D_MODEL)), const((1, D_MODEL)),
            const((N_EXPERTS, D_MODEL)), const((N_EXPERTS, 1))],
        out_specs=[tok(D_MODEL), tok(REC_W), idx, idx, const((8, LANES))],
        out_shape=[jax.ShapeDtypeStruct((B, S, D_MODEL), F32),
                   jax.ShapeDtypeStruct((B, S, REC_W), F32),
                   jax.ShapeDtypeStruct((B, nt, 1, tm), jnp.int32),
                   jax.ShapeDtypeStruct((B, nt, 1, tm), jnp.int32),
                   jax.ShapeDtypeStruct((8, LANES), F32)],
        scratch_shapes=[pltpu.VMEM((8, LANES), F32)],
        compiler_params=_cparams(("arbitrary", "arbitrary")),
        name="outproj_router",
    )(*stream, hf, hb, mo, n_out, mod, mnorm_w.reshape(1, -1), w_out, norm2_w.reshape(1, -1),
      router_wt, router_b.reshape(-1, 1))


def _row_copy_wait(src, dst, sem, n_rows):
    pltpu.make_async_copy(src.at[pl.ds(0, n_rows)], dst.at[pl.ds(0, n_rows)], sem).wait()


def _dispatch_kernel(pos_ref, rec_ref, init_ref, xs_ref, sem):
    del init_ref
    n = rec_ref.shape[0]

    def issue(r, carry):
        pltpu.make_async_copy(rec_ref.at[pl.ds(r, 1)], xs_ref.at[pl.ds(pos_ref[r], 1)], sem).start()
        return carry

    lax.fori_loop(0, n, issue, 0, unroll=8)
    _row_copy_wait(rec_ref, xs_ref, sem, n)


def _dispatch(pos, rec, n_sorted):
    T, w = rec.shape
    tp = _flat_tile(T)
    return pl.pallas_call(
        _dispatch_kernel,
        grid=(T // tp,),
        in_specs=[pl.BlockSpec((tp,), lambda i: (i,), memory_space=pltpu.SMEM),
                  pl.BlockSpec((tp, w), lambda i: (i, 0)),
                  pl.BlockSpec(memory_space=pl.ANY)],
        out_specs=pl.BlockSpec(memory_space=pl.ANY),
        out_shape=jax.ShapeDtypeStruct((n_sorted, w), F32),
        scratch_shapes=[pltpu.SemaphoreType.DMA(())],
        input_output_aliases={2: 0},
        compiler_params=_cparams(("arbitrary",)),
        name="moe_dispatch",
    )(pos, rec, jnp.zeros((n_sorted, w), F32))


def _experts_kernel(tg_ref, nt_ref, xs_ref, w1_ref, w3_ref, w2_ref, ys_ref, acc_ref):
    i = pl.program_id(0)
    k = pl.program_id(1)
    valid = i < nt_ref[0]

    @pl.when(k == 0)
    def _():
        acc_ref[...] = jnp.zeros_like(acc_ref)

    @pl.when(valid)
    def _():
        e = tg_ref[i] * EXPERTS_PER_GROUP + k
        h = xs_ref[:, :D_MODEL].astype(BF16)
        gate = xs_ref[:, D_MODEL:]
        lane = lax.broadcasted_iota(jnp.int32, gate.shape, 1)
        g_e = jnp.sum(jnp.where(lane == e, gate, 0.0), axis=-1, keepdims=True)
        a = jnp.dot(h, w1_ref[0], preferred_element_type=F32)
        b3 = jnp.dot(h, w3_ref[0], preferred_element_type=F32)
        act = (a * jax.nn.sigmoid(a) * b3 * g_e).astype(BF16)
        acc_ref[...] += jnp.dot(act, w2_ref[0], preferred_element_type=F32)

    @pl.when(k == EXPERTS_PER_GROUP - 1)
    def _():
        ys_ref[...] = acc_ref[...]


def _experts(tile_group, n_tiles, xs, w1, w3, w2):
    n_sorted, w = xs.shape
    tm = MOE_TILE
    expert = lambda i, k, tg, nt: (tg[i] * EXPERTS_PER_GROUP + k, 0, 0)
    grid_spec = pltpu.PrefetchScalarGridSpec(
        num_scalar_prefetch=2,
        grid=(n_sorted // tm, EXPERTS_PER_GROUP),
        in_specs=[pl.BlockSpec((tm, w), lambda i, k, *_: (i, 0)),
                  pl.BlockSpec((1, D_MODEL, MOE_D_FF), expert),
                  pl.BlockSpec((1, D_MODEL, MOE_D_FF), expert),
                  pl.BlockSpec((1, MOE_D_FF, D_MODEL), expert)],
        out_specs=pl.BlockSpec((tm, D_MODEL), lambda i, k, *_: (i, 0)),
        scratch_shapes=[pltpu.VMEM((tm, D_MODEL), F32)],
    )
    return pl.pallas_call(
        _experts_kernel,
        grid_spec=grid_spec,
        out_shape=jax.ShapeDtypeStruct((n_sorted, D_MODEL), F32),
        compiler_params=_cparams(("arbitrary", "arbitrary")),
        name="moe_experts",
    )(tile_group, n_tiles, xs, w1, w3, w2)


def _combine_kernel(pos_ref, ys_ref, x_ref, mod_ref, fw_ref, o_ref, y_buf, sem, *, S, n_ctx, final_norm):
    n = y_buf.shape[0]

    def issue(r, carry):
        pltpu.make_async_copy(ys_ref.at[pl.ds(pos_ref[r], 1)], y_buf.at[pl.ds(r, 1)], sem).start()
        return carry

    lax.fori_loop(0, n, issue, 0, unroll=8)
    t0 = pl.program_id(0) * n
    b0 = t0 // S
    b1 = jnp.minimum(b0 + 1, mod_ref.shape[0] - 1)
    tok = t0 + lax.broadcasted_iota(jnp.int32, (n, 1), 0)
    second = tok >= (b0 + 1) * S
    is_ctx = (tok - jnp.where(second, b0 + 1, b0) * S) < n_ctx
    gate_mod = jnp.where(second,
                         jnp.where(is_ctx, mod_ref[b1, 0, 5:6, :], mod_ref[b1, 1, 5:6, :]),
                         jnp.where(is_ctx, mod_ref[b0, 0, 5:6, :], mod_ref[b0, 1, 5:6, :]))
    _row_copy_wait(ys_ref, y_buf, sem, n)
    x_new = x_ref[...] + gate_mod * y_buf[...]
    if final_norm:
        x_new = _rms(x_new, fw_ref[...])
    o_ref[...] = x_new


def _combine(pos, ys, xa, mod, final_w, S, n_ctx, final_norm):
    T = xa.shape[0]
    tu = _flat_tile(T)
    assert tu <= S
    return pl.pallas_call(
        functools.partial(_combine_kernel, S=S, n_ctx=n_ctx, final_norm=final_norm),
        grid=(T // tu,),
        in_specs=[pl.BlockSpec((tu,), lambda i: (i,), memory_space=pltpu.SMEM),
                  pl.BlockSpec(memory_space=pl.ANY),
                  pl.BlockSpec((tu, D_MODEL), lambda i: (i, 0)),
                  pl.BlockSpec(mod.shape, lambda i: (0, 0, 0, 0)),
                  pl.BlockSpec((1, D_MODEL), lambda i: (0, 0))],
        out_specs=pl.BlockSpec((tu, D_MODEL), lambda i: (i, 0)),
        out_shape=jax.ShapeDtypeStruct((T, D_MODEL), F32),
        scratch_shapes=[pltpu.VMEM((tu, D_MODEL), F32), pltpu.SemaphoreType.DMA(())],
        compiler_params=_cparams(("arbitrary",)),
        name="moe_combine",
    )(pos, ys, xa, mod, final_w.reshape(1, -1))


def _flat_tile(T):
    return MOE_TILE if T % MOE_TILE == 0 else TOK_TILE


def _moe(rec, grp, rank, counts, xa, mod, w1, w3, w2, final_w, B, S, n_ctx, final_norm):
    T = B * S
    tm = MOE_TILE
    n_tiles_max = -(-T // tm) + N_GROUPS
    cnt = counts[:N_GROUPS, 0].astype(jnp.int32)
    tiles = (cnt + tm - 1) // tm
    tile_end = jnp.cumsum(tiles)
    row0 = (tile_end - tiles) * tm
    grp = grp.reshape(T)
    pos = rank.reshape(T)
    for gi in range(N_GROUPS):
        pos = pos + jnp.where(grp == gi, row0[gi], 0)
    tile_group = jnp.minimum(jnp.sum(jnp.arange(n_tiles_max)[:, None] >= tile_end[None, :], axis=1),
                             N_GROUPS - 1).astype(jnp.int32)
    xs = _dispatch(pos, rec.reshape(T, REC_W), n_tiles_max * tm)
    ys = _experts(tile_group, tile_end[N_GROUPS - 1:].astype(jnp.int32), xs, w1, w3, w2)
    out = _combine(pos, ys, xa.reshape(T, D_MODEL), mod, final_w, S, n_ctx, final_norm)
    return out.reshape(B, S, D_MODEL)


def _rope_tables(L, n_ctx):
    t = jnp.arange(L)
    row = (t // GRID_W).astype(F32)
    col = (t % GRID_W).astype(F32)
    inv = ROPE_BASE ** (-jnp.arange(0, ROPE_AXIS_DIM, 2, dtype=F32) / ROPE_AXIS_DIM)
    ar = row[:, None] * inv
    ac = col[:, None] * inv
    cos_l = jnp.concatenate([jnp.cos(ar), jnp.cos(ar), jnp.cos(ac), jnp.cos(ac)], axis=1)
    sin_l = jnp.concatenate([-jnp.sin(ar), jnp.sin(ar), -jnp.sin(ac), jnp.sin(ac)], axis=1)
    cos_t = jnp.concatenate([jnp.ones((n_ctx, M_HEAD_DIM), F32), cos_l], axis=0)
    sin_t = jnp.concatenate([jnp.zeros((n_ctx, M_HEAD_DIM), F32), sin_l], axis=0)
    return cos_t, sin_t


def _proj_weight(w_in):
    q_cols = 2 * M_WIDTH + N_WIDTH
    kv0 = q_cols
    g0 = kv0 + 2 * M_WIDTH
    n0 = g0 + G_COLS
    gates = jnp.pad(w_in[:, g0:n0], ((0, 0), (0, G_PAD - G_COLS)))
    return jnp.concatenate([w_in[:, :q_cols], w_in[:, kv0:g0], w_in[:, n0:], gates], axis=1).astype(BF16)


def kernel(x, c, ctx, c_ctx, ada_w, ada_b, norm1_w, w_in, conv_w, conv_b, gate_b, mnorm_w, rpb, w_out,
           norm2_w, router_w, router_b, exp_w1, exp_w3, exp_w2, final_norm_w):
    B, L, _ = x.shape
    n_ctx = ctx.shape[1]
    depth = ada_w.shape[0]
    assert n_ctx == TOK_TILE and L % TOK_TILE == 0 and L % GRID_W == 0
    S = n_ctx + L
    rows = L // GRID_W
    assert rows >= NA_WROWS + 4 and rows % NA_QROWS == 0 and (S // 4) % HALO == 0

    cvec = jnp.concatenate([c, c_ctx[None], jnp.zeros((16 - B - 1, D_MODEL), F32)], axis=0)
    mods = _adaln(cvec, ada_w, ada_b)
    cos_t, sin_t = _rope_tables(L, n_ctx)
    router_wt = router_w.T

    stream = (x, ctx)
    out = None
    for layer in range(depth):
        last = layer == depth - 1
        m6 = mods[layer].reshape(16, 6, D_MODEL)
        mod = jnp.stack([jnp.broadcast_to(m6[B], (B, 6, D_MODEL)), m6[:B]], axis=1)
        mq, mo, nq, mk, mv, nk, nv, g = _inproj(stream, mod, norm1_w[layer], _proj_weight(w_in[layer]), B, S)
        q, k = _prep(mq, mk, conv_w[layer], conv_b[layer], cos_t, sin_t, B, S)
        hf, hb = _mlstm(q, k, mv, g, gate_b[layer], B, S, n_ctx // CHUNK)
        typ, st, bias = _natten_tables(rpb[layer], rows, n_ctx)
        n_out = _natten(nq, nk, nv, typ, st, bias, B, S, n_ctx)
        xa, rec, grp, rank, counts = _outproj(stream, hf, hb, mo, n_out, mod, mnorm_w[layer],
                                              w_out[layer].astype(BF16), norm2_w[layer], router_wt, router_b, B, S)
        out = _moe(rec, grp, rank, counts, xa, mod, exp_w1[layer].astype(BF16), exp_w3[layer].astype(BF16),
                   exp_w2[layer].astype(BF16), final_norm_w, B, S, n_ctx, last)
        stream = (out,)
    return out[:, n_ctx:]
```

```python
import functools

import numpy as np
import jax
import jax.numpy as jnp
from jax import lax
from jax.experimental import pallas as pl
from jax.experimental.pallas import tpu as pltpu

D_MODEL = 1024
M_WIDTH = 512
M_HEADS = 4
M_HEAD_DIM = 128
N_WIDTH = 512
N_HEADS = 8
N_HEAD_DIM = 64
G_COLS = 16
GRID_W = 64
WIN_H = 8
WIN_W = 16
CHUNK = 128
ROPE_AXIS_DIM = 64
ROPE_BASE = 10000.0
N_EXPERTS = 16
N_GROUPS = 4
EXPERTS_PER_GROUP = 4
MOE_D_FF = 512
NORM_EPS = 1e-6

LANES = 128
TOK_TILE = 256
HALO = 16
NA_QROWS = 2
NA_WROWS = 10
NA_WKEYS = NA_WROWS * GRID_W
NEG = -1e30
G_PAD = LANES
N_PROJ = 7 * M_WIDTH + G_PAD
REC_W = D_MODEL + LANES
MOE_TILE = 1024
PAIRS_PER_GROUP = EXPERTS_PER_GROUP * (EXPERTS_PER_GROUP - 1) // 2
N_CLASSES = N_GROUPS * PAIRS_PER_GROUP
CLS_ROWS = 32
PAIR_LO = (0, 0, 0, 1, 1, 2)
PAIR_HI = (1, 2, 3, 2, 3, 3)
VMEM_LIMIT = 56 * 1024 * 1024

F32 = jnp.float32
BF16 = jnp.bfloat16


def _cparams(sem):
    return pltpu.CompilerParams(dimension_semantics=sem, vmem_limit_bytes=VMEM_LIMIT)


def _split_bf16(a):
    hi = a.astype(BF16)
    lo = (a - hi.astype(F32)).astype(BF16)
    return hi, lo


def _dot3(a, b):
    ah, al = _split_bf16(a)
    bh, bl = _split_bf16(b)
    d = functools.partial(jnp.dot, preferred_element_type=F32)
    return d(ah, bh) + (d(al, bh) + d(ah, bl))


def _dot_nt(a, b):
    return lax.dot_general(a, b, (((1,), (1,)), ((), ())), preferred_element_type=F32)


def _rms(x, w):
    ms = jnp.mean(x * x, axis=-1, keepdims=True)
    return x * lax.rsqrt(ms + NORM_EPS) * w


def _adaln_kernel(c_ref, w_ref, b_ref, o_ref):
    c = c_ref[...]
    s = c * jax.nn.sigmoid(c)
    o_ref[0] = _dot3(s, w_ref[0]) + b_ref[0]


def _adaln(cvec, ada_w, ada_b):
    depth = ada_w.shape[0]
    n = ada_w.shape[2]
    tn = D_MODEL
    return pl.pallas_call(
        _adaln_kernel,
        grid=(depth, n // tn),
        in_specs=[pl.BlockSpec((16, D_MODEL), lambda l, j: (0, 0)),
                  pl.BlockSpec((1, D_MODEL, tn), lambda l, j: (l, 0, j)),
                  pl.BlockSpec((1, 1, tn), lambda l, j: (l, 0, j))],
        out_specs=pl.BlockSpec((1, 16, tn), lambda l, j: (l, 0, j)),
        out_shape=jax.ShapeDtypeStruct((depth, 16, n), F32),
        compiler_params=_cparams(("arbitrary", "arbitrary")),
        name="adaln",
    )(cvec, ada_w, ada_b.reshape(depth, 1, n))


def _inproj_kernel(*refs, split_input):
    if split_input:
        x_ref, ctx_ref, mod_ref, nw_ref, w_ref = refs[:5]
        outs = refs[5:]
        xt = jnp.where(pl.program_id(1) == 0, ctx_ref[0], x_ref[0])
    else:
        x_ref, mod_ref, nw_ref, w_ref = refs[:4]
        outs = refs[4:]
        xt = x_ref[0]
    h = _rms(xt, nw_ref[...]) * (1.0 + mod_ref[0, 0, 1:2, :]) + mod_ref[0, 0, 0:1, :]
    hb = h.astype(BF16)
    for k in range(7):
        outs[k][0] = jnp.dot(hb, w_ref[:, k * M_WIDTH:(k + 1) * M_WIDTH],
                             preferred_element_type=F32).astype(BF16)
    outs[7][0] = jnp.dot(hb, w_ref[:, 7 * M_WIDTH:], preferred_element_type=F32)


def _stream_specs(split_input):
    tm = TOK_TILE
    if split_input:
        return [pl.BlockSpec((1, tm, D_MODEL), lambda b, j: (b, jnp.maximum(j - 1, 0), 0)),
                pl.BlockSpec((1, tm, D_MODEL), lambda b, j: (b, 0, 0))]
    return [pl.BlockSpec((1, tm, D_MODEL), lambda b, j: (b, j, 0))]


def _mod_spec():
    return pl.BlockSpec((1, 1, 6, D_MODEL), lambda b, j: (b, jnp.minimum(j, 1), 0, 0))


def _inproj(stream, mod, norm_w, w_proj, B, S):
    split_input = len(stream) == 2
    tm = TOK_TILE
    tok = lambda w: pl.BlockSpec((1, tm, w), lambda b, j: (b, j, 0))
    out_shape = [jax.ShapeDtypeStruct((B, S, M_WIDTH), BF16)] * 7 + [jax.ShapeDtypeStruct((B, S, G_PAD), F32)]
    return pl.pallas_call(
        functools.partial(_inproj_kernel, split_input=split_input),
        grid=(B, S // tm),
        in_specs=_stream_specs(split_input) + [
            _mod_spec(),
            pl.BlockSpec((1, D_MODEL), lambda b, j: (0, 0)),
            pl.BlockSpec((D_MODEL, N_PROJ), lambda b, j: (0, 0))],
        out_specs=[tok(M_WIDTH)] * 7 + [tok(G_PAD)],
        out_shape=out_shape,
        compiler_params=_cparams(("arbitrary", "arbitrary")),
        name="inproj",
    )(*stream, mod, norm_w.reshape(1, D_MODEL), w_proj)


def _prep_kernel(q_ref, qp_ref, qn_ref, k_ref, kp_ref, kn_ref, cw_ref, cb_ref, cos_ref, sin_ref,
                 qo_ref, ko_ref, *, n_tiles):
    j = pl.program_id(1)
    tp = TOK_TILE
    has_prev = jnp.logical_and(j != 0, j != 1).astype(F32)
    has_next = jnp.logical_and(j != 0, j != n_tiles - 1).astype(F32)
    rows = lax.broadcasted_iota(jnp.int32, (tp, M_WIDTH), 0)
    lanes = lax.broadcasted_iota(jnp.int32, (tp, M_WIDTH), 1)
    low_half = (lanes % (ROPE_AXIS_DIM)) < (ROPE_AXIS_DIM // 2)
    cosv = jnp.concatenate([cos_ref[...]] * M_HEADS, axis=1)
    sinv = jnp.concatenate([sin_ref[...]] * M_HEADS, axis=1)

    def branch(x_ref, p_ref, n_ref, col0):
        x = x_ref[0].astype(F32)
        prev_row = p_ref[0, HALO - 1:HALO, :].astype(F32) * has_prev
        next_row = n_ref[0, 0:1, :].astype(F32) * has_next
        xm = jnp.where(rows == 0, prev_row, pltpu.roll(x, 1, 0))
        xp = jnp.where(rows == tp - 1, next_row, pltpu.roll(x, tp - 1, 0))
        w = cw_ref[:, col0:col0 + M_WIDTH]
        y = cb_ref[:, col0:col0 + M_WIDTH] + xm * w[0:1] + x * w[1:2] + xp * w[2:3]
        y = y * jax.nn.sigmoid(y)
        half = ROPE_AXIS_DIM // 2
        partner = jnp.where(low_half, pltpu.roll(y, M_WIDTH - half, 1), pltpu.roll(y, half, 1))
        return y * cosv + partner * sinv

    qo_ref[0] = branch(q_ref, qp_ref, qn_ref, 0).astype(BF16)
    ko_ref[0] = (branch(k_ref, kp_ref, kn_ref, M_WIDTH) * (M_HEAD_DIM ** -0.5)).T.astype(BF16)


def _prep(mq, mk, conv_w, conv_b, cos_t, sin_t, B, S):
    tp = TOK_TILE
    n_tiles = S // tp
    per = tp // HALO
    n_halo = S // HALO
    main = pl.BlockSpec((1, tp, M_WIDTH), lambda b, j: (b, j, 0))
    prev = pl.BlockSpec((1, HALO, M_WIDTH), lambda b, j: (b, jnp.maximum(j * per - 1, 0), 0))
    nxt = pl.BlockSpec((1, HALO, M_WIDTH), lambda b, j: (b, jnp.minimum((j + 1) * per, n_halo - 1), 0))
    return pl.pallas_call(
        functools.partial(_prep_kernel, n_tiles=n_tiles),
        grid=(B, n_tiles),
        in_specs=[main, prev, nxt, main, prev, nxt,
                  pl.BlockSpec((3, 2 * M_WIDTH), lambda b, j: (0, 0)),
                  pl.BlockSpec((1, 2 * M_WIDTH), lambda b, j: (0, 0)),
                  pl.BlockSpec((tp, M_HEAD_DIM), lambda b, j: (j, 0)),
                  pl.BlockSpec((tp, M_HEAD_DIM), lambda b, j: (j, 0))],
        out_specs=[main, pl.BlockSpec((1, M_WIDTH, tp), lambda b, j: (b, 0, j))],
        out_shape=[jax.ShapeDtypeStruct((B, S, M_WIDTH), BF16), jax.ShapeDtypeStruct((B, M_WIDTH, S), BF16)],
        compiler_params=_cparams(("arbitrary", "arbitrary")),
        name="mlstm_prep",
    )(mq, mq, mq, mk, mk, mk, conv_w, conv_b.reshape(1, -1), cos_t, sin_t)


def _scan_rows(x, reverse, op, fill):
    n = x.shape[0]
    rows = lax.broadcasted_iota(jnp.int32, x.shape, 0)
    sh = 1
    while sh < n:
        if reverse:
            x = op(x, jnp.where(rows < n - sh, pltpu.roll(x, n - sh, 0), fill))
        else:
            x = op(x, jnp.where(rows >= sh, pltpu.roll(x, sh, 0), fill))
        sh *= 2
    return x


def _log_sigmoid(x):
    return jnp.minimum(x, 0.0) - jnp.log(1.0 + jnp.exp(-jnp.abs(x)))


def _mlstm_kernel(qf_ref, kf_ref, vf_ref, gf_ref, qb_ref, kb_ref, vb_ref, gb_ref, gbias_ref,
                  hf_ref, hb_ref, cn_ref, m_ref):
    c = pl.program_id(1)

    @pl.when(c == 0)
    def _():
        cn_ref[...] = jnp.zeros_like(cn_ref)
        m_ref[...] = jnp.zeros_like(m_ref)

    t = CHUNK
    r_i = lax.broadcasted_iota(jnp.int32, (t, t), 0)
    c_i = lax.broadcasted_iota(jnp.int32, (t, t), 1)
    ones_blk = jnp.ones((t, LANES), BF16)

    for d, (q_ref, k_ref, v_ref, g_ref, h_ref) in enumerate(
            ((qf_ref, kf_ref, vf_ref, gf_ref, hf_ref), (qb_ref, kb_ref, vb_ref, gb_ref, hb_ref))):
        reverse = d == 1
        mask = (c_i >= r_i) if reverse else (c_i <= r_i)
        end_row = 0 if reverse else t - 1
        g = g_ref[0] + gbias_ref[...]
        bc = _scan_rows(_log_sigmoid(g), reverse, jnp.add, 0.0)
        b_al = pltpu.roll(bc, LANES - M_HEADS, 1)
        a = g - b_al
        m_old = m_ref[d][0:1, :]
        mx = jnp.maximum(m_old, _scan_rows(a, reverse, jnp.maximum, NEG))
        m_end = mx[end_row:end_row + 1, :]
        c_t = -mx
        e_den = jnp.exp(-(b_al + mx))
        w_old = jnp.exp(m_old - m_end)
        a_t = a.T
        w_tok_t = jnp.exp(a - m_end).T
        m_ref[d] = jnp.broadcast_to(b_al[end_row:end_row + 1, :] + m_end, (8, LANES))
        for h in range(M_HEADS):
            icol = 2 * M_HEADS * d + h
            hs = slice(h * M_HEAD_DIM, (h + 1) * M_HEAD_DIM)
            q = q_ref[0, :, hs]
            k_t = k_ref[0, hs, :]
            v_ext = jnp.concatenate([v_ref[0, :, hs], ones_blk], axis=1)
            cn_old = cn_ref[d, h]

            c_b = jnp.broadcast_to(c_t[:, icol:icol + 1], (t, t))
            expo = jnp.concatenate([jnp.where(mask, c_b + a_t[icol:icol + 1, :], NEG),
                                    c_b + m_old[:, icol:icol + 1]], axis=1)
            qk = jnp.dot(q, k_t, preferred_element_type=F32)
            s_ext = (jnp.concatenate([qk, q.astype(F32)], axis=1) * jnp.exp(expo)).astype(BF16)
            rhs = jnp.concatenate([v_ext, cn_old.astype(BF16)], axis=0)
            ext = jnp.dot(s_ext, rhs, preferred_element_type=F32)
            den = jnp.maximum(jnp.abs(ext[:, M_HEAD_DIM:]),
                              jnp.broadcast_to(e_den[:, icol:icol + 1], (t, LANES)))
            h_ref[0, :, hs] = (ext[:, :M_HEAD_DIM] / den).astype(BF16)

            kw_t = (k_t.astype(F32) * w_tok_t[icol:icol + 1, :]).astype(BF16)
            cn_ref[d, h] = (w_old[:, icol:icol + 1] * cn_old
                            + jnp.dot(kw_t, v_ext, preferred_element_type=F32))


def _mlstm(q, k, v, g, gate_b, B, S, n_ctx_chunks):
    nc = S // CHUNK
    ncx = n_ctx_chunks

    def bwd_chunk(c):
        return jnp.where(c < ncx, ncx - 1 - c, nc - 1 + ncx - c)

    fw = lambda w: pl.BlockSpec((1, CHUNK, w), lambda b, c: (b, c, 0))
    bw = lambda w: pl.BlockSpec((1, CHUNK, w), lambda b, c: (b, bwd_chunk(c), 0))
    fw_t = pl.BlockSpec((1, M_WIDTH, CHUNK), lambda b, c: (b, 0, c))
    bw_t = pl.BlockSpec((1, M_WIDTH, CHUNK), lambda b, c: (b, 0, bwd_chunk(c)))
    gbias = jnp.zeros((1, G_PAD), F32).at[0, :G_COLS].set(gate_b)
    return pl.pallas_call(
        _mlstm_kernel,
        grid=(B, nc),
        in_specs=[fw(M_WIDTH), fw_t, fw(M_WIDTH), fw(G_PAD),
                  bw(M_WIDTH), bw_t, bw(M_WIDTH), bw(G_PAD),
                  pl.BlockSpec((1, G_PAD), lambda b, c: (0, 0))],
        out_specs=[fw(M_WIDTH), bw(M_WIDTH)],
        out_shape=[jax.ShapeDtypeStruct((B, S, M_WIDTH), BF16)] * 2,
        scratch_shapes=[pltpu.VMEM((2, M_HEADS, M_HEAD_DIM, 2 * LANES), F32),
                        pltpu.VMEM((2, 8, LANES), F32)],
        compiler_params=_cparams(("arbitrary", "arbitrary")),
        name="mlstm_scan",
    )(q, k, v, g, q, k, v, g, gbias)


def _natten_kernel(typ_ref, st_ref, q_ref, k_ref, v_ref, bias_ref, o_ref, *, n_ctx):
    j = pl.program_id(2)
    sub = NA_QROWS * GRID_W
    lane = lax.broadcasted_iota(jnp.int32, (sub, LANES), 1)
    kctx = k_ref[0, 0:n_ctx, :]
    vctx = v_ref[0, 0:n_ctx, :]
    for i in range(TOK_TILE // sub):
        sb = j * (TOK_TILE // sub) + i
        typ = typ_ref[sb]
        st = pl.multiple_of(st_ref[sb], GRID_W)
        q = q_ref[0, i * sub:(i + 1) * sub, :] * (N_HEAD_DIM ** -0.5)
        kwin = k_ref[0, pl.ds(st, NA_WKEYS), :]
        vwin = v_ref[0, pl.ds(st, NA_WKEYS), :]
        first = lane < N_HEAD_DIM
        zero = jnp.zeros_like(q)
        q2 = jnp.concatenate([jnp.where(first, q, zero), jnp.where(first, zero, q)], axis=0)
        s_win = _dot_nt(q2, kwin) + bias_ref[typ].reshape(2 * sub, NA_WKEYS)
        s_ctx = _dot_nt(q2, kctx)
        m = jnp.maximum(jnp.max(s_win, axis=-1, keepdims=True), jnp.max(s_ctx, axis=-1, keepdims=True))
        p_win = jnp.exp(s_win - m)
        p_ctx = jnp.exp(s_ctx - m)
        l = jnp.sum(p_win, axis=-1, keepdims=True) + jnp.sum(p_ctx, axis=-1, keepdims=True)
        o = (jnp.dot(p_win.astype(BF16), vwin, preferred_element_type=F32)
             + jnp.dot(p_ctx.astype(BF16), vctx, preferred_element_type=F32)) / l
        o_ref[0, i * sub:(i + 1) * sub, :] = jnp.where(first, o[:sub], o[sub:]).astype(BF16)


def _natten_tables(rpb, rows, n_ctx):
    n_sb = rows // NA_QROWS
    reps = (0, 1, 2, n_sb - 2, n_sb - 1)
    ws_of = lambda sb: int(np.clip(NA_QROWS * sb - WIN_H // 2, 0, rows - NA_WROWS))
    n_dr = 2 * WIN_H - 1
    n_dc = 2 * WIN_W - 1
    lead = GRID_W - WIN_W
    vec = jnp.pad(rpb, ((0, 0), (0, 0), (lead, 2 * GRID_W - lead - n_dc)), constant_values=NEG)
    skew = jnp.broadcast_to(vec[:, :, None, :], (N_HEADS, n_dr, GRID_W, 2 * GRID_W))
    skew = skew.reshape(N_HEADS, n_dr, -1)[:, :, :GRID_W * (2 * GRID_W - 1)]
    toep = skew.reshape(N_HEADS, n_dr, GRID_W, 2 * GRID_W - 1)[..., GRID_W - 1:]
    cq = np.arange(GRID_W)[:, None]
    ck = np.arange(GRID_W)[None, :]
    cs = np.clip(cq - WIN_W // 2, 0, GRID_W - WIN_W)
    toep = jnp.where((ck >= cs) & (ck < cs + WIN_W), toep, NEG)
    masked_tile = jnp.full((N_HEADS, GRID_W, GRID_W), NEG, F32)
    types = []
    for sb in reps:
        ws = ws_of(sb)
        q_rows = []
        for qr in range(NA_QROWS):
            r = NA_QROWS * sb + qr
            rs = int(np.clip(r - WIN_H // 2, 0, rows - WIN_H))
            tiles = []
            for kw in range(NA_WROWS):
                kr = ws + kw
                tiles.append(toep[:, kr - r + WIN_H - 1] if rs <= kr < rs + WIN_H else masked_tile)
            q_rows.append(jnp.concatenate(tiles, axis=-1))
        types.append(jnp.concatenate(q_rows, axis=-2))
    types.append(jnp.full_like(types[0], NEG))
    bias = jnp.stack(types, axis=0)
    typ, st = [5] * (n_ctx // (NA_QROWS * GRID_W)), [n_ctx] * (n_ctx // (NA_QROWS * GRID_W))
    for sb in range(n_sb):
        ws = ws_of(sb)
        typ.append((NA_QROWS * sb - ws) // 2)
        st.append(n_ctx + ws * GRID_W)
    return jnp.asarray(typ, jnp.int32), jnp.asarray(st, jnp.int32), bias


def _natten(nq, nk, nv, typ, st, bias, B, S, n_ctx):
    n_pairs = N_HEADS // 2
    grid_spec = pltpu.PrefetchScalarGridSpec(
        num_scalar_prefetch=2,
        grid=(n_pairs, B, S // TOK_TILE),
        in_specs=[pl.BlockSpec((1, TOK_TILE, LANES), lambda p, b, j, *_: (b, j, p)),
                  pl.BlockSpec((1, S, LANES), lambda p, b, j, *_: (b, 0, p)),
                  pl.BlockSpec((1, S, LANES), lambda p, b, j, *_: (b, 0, p)),
                  pl.BlockSpec((6, 2, NA_QROWS * GRID_W, NA_WKEYS), lambda p, b, j, *_: (0, p, 0, 0))],
        out_specs=pl.BlockSpec((1, TOK_TILE, LANES), lambda p, b, j, *_: (b, j, p)),
    )
    return pl.pallas_call(
        functools.partial(_natten_kernel, n_ctx=n_ctx),
        grid_spec=grid_spec,
        out_shape=jax.ShapeDtypeStruct((B, S, N_WIDTH), BF16),
        compiler_params=_cparams(("arbitrary", "arbitrary", "arbitrary")),
        name="natten",
    )(typ, st, nq, nk, nv, bias)


def _route(logits_t, rb):
    e_i = lax.broadcasted_iota(jnp.int32, logits_t.shape, 0)
    z = logits_t - jnp.max(logits_t, axis=0, keepdims=True)
    ez = jnp.exp(z)
    scores = ez / jnp.sum(ez, axis=0, keepdims=True)
    sel = scores + rb
    best = None
    best_score = None
    for gi in range(N_GROUPS):
        r = [sel[gi * EXPERTS_PER_GROUP + u:gi * EXPERTS_PER_GROUP + u + 1, :] for u in range(EXPERTS_PER_GROUP)]
        gs = None
        for u in range(EXPERTS_PER_GROUP):
            for w in range(u + 1, EXPERTS_PER_GROUP):
                pair = r[u] + r[w]
                gs = pair if gs is None else jnp.maximum(gs, pair)
        if best is None:
            best, best_score = jnp.zeros(gs.shape, jnp.int32), gs
        else:
            better = gs > best_score
            best = jnp.where(better, gi, best)
            best_score = jnp.where(better, gs, best_score)
    masked = jnp.where((e_i // EXPERTS_PER_GROUP) == best, sel, -jnp.inf)
    v1 = jnp.max(masked, axis=0, keepdims=True)
    i1 = jnp.min(jnp.where(masked == v1, e_i, N_EXPERTS), axis=0, keepdims=True)
    masked2 = jnp.where(e_i == i1, -jnp.inf, masked)
    v2 = jnp.max(masked2, axis=0, keepdims=True)
    i2 = jnp.min(jnp.where(masked2 == v2, e_i, N_EXPERTS), axis=0, keepdims=True)
    w1 = jnp.sum(jnp.where(e_i == i1, scores, 0.0), axis=0, keepdims=True)
    w2 = jnp.sum(jnp.where(e_i == i2, scores, 0.0), axis=0, keepdims=True)
    tot = w1 + w2
    gate = jnp.where(e_i == i1, w1 / tot, 0.0) + jnp.where(e_i == i2, w2 / tot, 0.0)
    lo = jnp.minimum(i1, i2) - best * EXPERTS_PER_GROUP
    hi = jnp.maximum(i1, i2) - best * EXPERTS_PER_GROUP
    pair = ((lo * (2 * EXPERTS_PER_GROUP - 1 - lo)) >> 1) + (hi - lo - 1)
    return gate, best * PAIRS_PER_GROUP + pair


def _outproj_kernel(*refs, split_input):
    n_stream = 2 if split_input else 1
    stream = refs[:n_stream]
    (hf_ref, hb_ref, mo_ref, no_ref, mod_ref, mnw_ref, wo_ref, n2w_ref, rw_ref, rb_ref,
     xo_ref, rec_ref, grp_ref, rank_ref, cnt_ref, base_ref) = refs[n_stream:]

    @pl.when(jnp.logical_and(pl.program_id(0) == 0, pl.program_id(1) == 0))
    def _():
        base_ref[...] = jnp.zeros_like(base_ref)

    rh, rl = _split_bf16(rw_ref[...])
    n_rows = xo_ref.shape[1]
    half = n_rows // 2
    logits = []
    for r0 in (0, half):
        rs = slice(r0, r0 + half)
        if split_input:
            xt = jnp.where(pl.program_id(1) == 0, stream[1][0, rs], stream[0][0, rs])
        else:
            xt = stream[0][0, rs]
        hsum = hf_ref[0, rs].astype(F32) + hb_ref[0, rs].astype(F32)
        parts = []
        for h in range(M_HEADS):
            hs = slice(h * M_HEAD_DIM, (h + 1) * M_HEAD_DIM)
            parts.append(_rms(hsum[:, hs], mnw_ref[:, hs]))
        m_out = jnp.concatenate(parts, axis=1) * jax.nn.sigmoid(mo_ref[0, rs].astype(F32))
        y = (jnp.dot(m_out.astype(BF16), wo_ref[0:M_WIDTH, :], preferred_element_type=F32)
             + jnp.dot(no_ref[0, rs], wo_ref[M_WIDTH:, :], preferred_element_type=F32))
        x_new = xt + mod_ref[0, 0, 2:3, :] * y
        xo_ref[0, rs] = x_new
        hx = _rms(x_new, n2w_ref[...]) * (1.0 + mod_ref[0, 0, 4:5, :]) + mod_ref[0, 0, 3:4, :]
        rec_ref[0, rs, :D_MODEL] = hx
        hh, hl = _split_bf16(hx)
        logits.append(_dot_nt(rh, hh) + (_dot_nt(rh, hl) + _dot_nt(rl, hh)))
    logits_t = jnp.concatenate(logits, axis=1)
    gate_t, cls = _route(logits_t, rb_ref[...])
    pad = jnp.zeros((LANES - N_EXPERTS, gate_t.shape[1]), F32)
    rec_ref[0, :, D_MODEL:] = jnp.concatenate([gate_t, pad], axis=0).T
    tm = cls.shape[1]
    onehot = lax.broadcasted_iota(jnp.int32, (CLS_ROWS, tm), 0) == cls
    upper = (lax.broadcasted_iota(jnp.int32, (tm, tm), 0) <= lax.broadcasted_iota(jnp.int32, (tm, tm), 1))
    csum = jnp.dot(onehot.astype(BF16), upper.astype(BF16), preferred_element_type=F32)
    base = base_ref[:, 0:1]
    rank = jnp.sum(jnp.where(onehot, csum - 1.0 + base, 0.0), axis=0, keepdims=True)
    grp_ref[0, 0] = cls
    rank_ref[0, 0] = rank.astype(jnp.int32)
    total = base_ref[...] + csum[:, tm - 1:tm]
    base_ref[...] = total
    cnt_ref[...] = total


def _outproj(stream, hf, hb, mo, n_out, mod, mnorm_w, w_out, norm2_w, router_wt, router_b, B, S):
    split_input = len(stream) == 2
    tm = TOK_TILE
    nt = S // tm
    tok = lambda w: pl.BlockSpec((1, tm, w), lambda b, j: (b, j, 0))
    idx = pl.BlockSpec((1, 1, 1, tm), lambda b, j: (b, j, 0, 0))
    const = lambda shape: pl.BlockSpec(shape, lambda b, j: (0,) * len(shape))
    return pl.pallas_call(
        functools.partial(_outproj_kernel, split_input=split_input),
        grid=(B, nt),
        in_specs=_stream_specs(split_input) + [
            tok(M_WIDTH), tok(M_WIDTH), tok(M_WIDTH), tok(N_WIDTH), _mod_spec(),
            const((1, M_WIDTH)), const((D_MODEL, D_MODEL)), const((1, D_MODEL)),
            const((N_EXPERTS, D_MODEL)), const((N_EXPERTS, 1))],
        out_specs=[tok(D_MODEL), tok(REC_W), idx, idx, const((CLS_ROWS, LANES))],
        out_shape=[jax.ShapeDtypeStruct((B, S, D_MODEL), F32),
                   jax.ShapeDtypeStruct((B, S, REC_W), F32),
                   jax.ShapeDtypeStruct((B, nt, 1, tm), jnp.int32),
                   jax.ShapeDtypeStruct((B, nt, 1, tm), jnp.int32),
                   jax.ShapeDtypeStruct((CLS_ROWS, LANES), F32)],
        scratch_shapes=[pltpu.VMEM((CLS_ROWS, LANES), F32)],
        compiler_params=_cparams(("arbitrary", "arbitrary")),
        name="outproj_router",
    )(*stream, hf, hb, mo, n_out, mod, mnorm_w.reshape(1, -1), w_out, norm2_w.reshape(1, -1),
      router_wt, router_b.reshape(-1, 1))


def _row_copy_wait(src, dst, sem, n_rows):
    pltpu.make_async_copy(src.at[pl.ds(0, n_rows)], dst.at[pl.ds(0, n_rows)], sem).wait()


def _dispatch_kernel(pos_ref, rec_ref, init_ref, xs_ref, sem):
    del init_ref
    n = rec_ref.shape[0]

    def issue(r, carry):
        pltpu.make_async_copy(rec_ref.at[pl.ds(r, 1)], xs_ref.at[pl.ds(pos_ref[r], 1)], sem).start()
        return carry

    lax.fori_loop(0, n, issue, 0, unroll=8)
    _row_copy_wait(rec_ref, xs_ref, sem, n)


def _dispatch(pos, rec, init):
    T, w = rec.shape
    n_sorted = init.shape[0]
    tp = _flat_tile(T)
    return pl.pallas_call(
        _dispatch_kernel,
        grid=(T // tp,),
        in_specs=[pl.BlockSpec((tp,), lambda i: (i,), memory_space=pltpu.SMEM),
                  pl.BlockSpec((tp, w), lambda i: (i, 0)),
                  pl.BlockSpec(memory_space=pl.ANY)],
        out_specs=pl.BlockSpec(memory_space=pl.ANY),
        out_shape=jax.ShapeDtypeStruct((n_sorted, w), F32),
        scratch_shapes=[pltpu.SemaphoreType.DMA(())],
        input_output_aliases={2: 0},
        compiler_params=_cparams(("arbitrary",)),
        name="moe_dispatch",
    )(pos, rec, init)


def _experts_kernel(te_ref, nt_ref, xs_ref, w1_ref, w3_ref, w2_ref, ys_ref, acc_ref):
    i = pl.program_id(0)
    k = pl.program_id(1)

    @pl.when(k == 0)
    def _():
        acc_ref[...] = jnp.zeros_like(acc_ref)

    @pl.when(i < nt_ref[0])
    def _():
        e = te_ref[2 * i + k]
        h = xs_ref[:, :D_MODEL].astype(BF16)
        gate = xs_ref[:, D_MODEL:]
        lane = lax.broadcasted_iota(jnp.int32, gate.shape, 1)
        g_e = jnp.sum(jnp.where(lane == e, gate, 0.0), axis=-1, keepdims=True)
        a = jnp.dot(h, w1_ref[0].astype(BF16), preferred_element_type=F32)
        b3 = jnp.dot(h, w3_ref[0].astype(BF16), preferred_element_type=F32)
        act = (a * jax.nn.sigmoid(a) * b3 * g_e).astype(BF16)
        acc_ref[...] += jnp.dot(act, w2_ref[0].astype(BF16), preferred_element_type=F32)

    @pl.when(k == 1)
    def _():
        ys_ref[...] = acc_ref[...]


def _experts(tile_experts, n_tiles, xs, w1, w3, w2):
    n_sorted, w = xs.shape
    tm = MOE_TILE
    expert = lambda i, k, te, nt: (te[2 * jnp.minimum(i, nt[0] - 1) + k], 0, 0)
    rows = lambda i, k, te, nt: (jnp.minimum(i, nt[0] - 1), 0)
    grid_spec = pltpu.PrefetchScalarGridSpec(
        num_scalar_prefetch=2,
        grid=(n_sorted // tm, 2),
        in_specs=[pl.BlockSpec((tm, w), rows),
                  pl.BlockSpec((1, D_MODEL, MOE_D_FF), expert),
                  pl.BlockSpec((1, D_MODEL, MOE_D_FF), expert),
                  pl.BlockSpec((1, MOE_D_FF, D_MODEL), expert)],
        out_specs=pl.BlockSpec((tm, D_MODEL), lambda i, k, *_: (i, 0)),
        scratch_shapes=[pltpu.VMEM((tm, D_MODEL), F32)],
    )
    return pl.pallas_call(
        _experts_kernel,
        grid_spec=grid_spec,
        out_shape=jax.ShapeDtypeStruct((n_sorted, D_MODEL), F32),
        compiler_params=_cparams(("arbitrary", "arbitrary")),
        name="moe_experts",
    )(tile_experts, n_tiles, xs, w1, w3, w2)


def _combine_kernel(pos_ref, ys_ref, x_ref, mod_ref, fw_ref, o_ref, y_buf, sem, *, S, n_ctx, final_norm):
    n = y_buf.shape[0]

    def issue(r, carry):
        pltpu.make_async_copy(ys_ref.at[pl.ds(pos_ref[r], 1)], y_buf.at[pl.ds(r, 1)], sem).start()
        return carry

    lax.fori_loop(0, n, issue, 0, unroll=8)
    t0 = pl.program_id(0) * n
    b0 = t0 // S
    b1 = jnp.minimum(b0 + 1, mod_ref.shape[0] - 1)
    tok = t0 + lax.broadcasted_iota(jnp.int32, (n, 1), 0)
    second = tok >= (b0 + 1) * S
    is_ctx = (tok - jnp.where(second, b0 + 1, b0) * S) < n_ctx
    gate_mod = jnp.where(second,
                         jnp.where(is_ctx, mod_ref[b1, 0, 5:6, :], mod_ref[b1, 1, 5:6, :]),
                         jnp.where(is_ctx, mod_ref[b0, 0, 5:6, :], mod_ref[b0, 1, 5:6, :]))
    _row_copy_wait(ys_ref, y_buf, sem, n)
    x_new = x_ref[...] + gate_mod * y_buf[...]
    if final_norm:
        x_new = _rms(x_new, fw_ref[...])
    o_ref[...] = x_new


def _combine(pos, ys, xa, mod, final_w, S, n_ctx, final_norm):
    T = xa.shape[0]
    tu = _flat_tile(T)
    assert tu <= S
    return pl.pallas_call(
        functools.partial(_combine_kernel, S=S, n_ctx=n_ctx, final_norm=final_norm),
        grid=(T // tu,),
        in_specs=[pl.BlockSpec((tu,), lambda i: (i,), memory_space=pltpu.SMEM),
                  pl.BlockSpec(memory_space=pl.ANY),
                  pl.BlockSpec((tu, D_MODEL), lambda i: (i, 0)),
                  pl.BlockSpec(mod.shape, lambda i: (0, 0, 0, 0)),
                  pl.BlockSpec((1, D_MODEL), lambda i: (0, 0))],
        out_specs=pl.BlockSpec((tu, D_MODEL), lambda i: (i, 0)),
        out_shape=jax.ShapeDtypeStruct((T, D_MODEL), F32),
        scratch_shapes=[pltpu.VMEM((tu, D_MODEL), F32), pltpu.SemaphoreType.DMA(())],
        compiler_params=_cparams(("arbitrary",)),
        name="moe_combine",
    )(pos, ys, xa, mod, final_w.reshape(1, -1))


def _combine_latent_kernel(pos_ref, ys_ref, xa_ref, mod_ref, fw_ref, o_ref, y_buf, x_buf, sem, x_sem,
                           *, S, L, n_ctx):
    n = y_buf.shape[0]
    per_batch = L // n
    b = pl.program_id(0) // per_batch
    row0 = b * S + n_ctx + (pl.program_id(0) % per_batch) * n
    x_copy = pltpu.make_async_copy(xa_ref.at[pl.ds(row0, n)], x_buf, x_sem)
    x_copy.start()

    def issue(r, carry):
        pltpu.make_async_copy(ys_ref.at[pl.ds(pos_ref[r], 1)], y_buf.at[pl.ds(r, 1)], sem).start()
        return carry

    lax.fori_loop(0, n, issue, 0, unroll=8)
    x_copy.wait()
    _row_copy_wait(ys_ref, y_buf, sem, n)
    o_ref[...] = _rms(x_buf[...] + mod_ref[b, 1, 5:6, :] * y_buf[...], fw_ref[...])


def _combine_latent(pos, ys, xa, mod, final_w, B, S, L, n_ctx):
    tu = _flat_tile(L)
    pos_lat = pos.reshape(B, S)[:, n_ctx:].reshape(B * L)
    return pl.pallas_call(
        functools.partial(_combine_latent_kernel, S=S, L=L, n_ctx=n_ctx),
        grid=(B * L // tu,),
        in_specs=[pl.BlockSpec((tu,), lambda i: (i,), memory_space=pltpu.SMEM),
                  pl.BlockSpec(memory_space=pl.ANY),
                  pl.BlockSpec(memory_space=pl.ANY),
                  pl.BlockSpec(mod.shape, lambda i: (0, 0, 0, 0)),
                  pl.BlockSpec((1, D_MODEL), lambda i: (0, 0))],
        out_specs=pl.BlockSpec((tu, D_MODEL), lambda i: (i, 0)),
        out_shape=jax.ShapeDtypeStruct((B * L, D_MODEL), F32),
        scratch_shapes=[pltpu.VMEM((tu, D_MODEL), F32), pltpu.VMEM((tu, D_MODEL), F32),
                        pltpu.SemaphoreType.DMA(()), pltpu.SemaphoreType.DMA(())],
        compiler_params=_cparams(("arbitrary",)),
        name="moe_combine_latent",
    )(pos_lat, ys, xa, mod, final_w.reshape(1, -1))


def _flat_tile(T):
    return MOE_TILE if T % MOE_TILE == 0 else TOK_TILE


def _sorted_rows(T):
    return (-(-T // MOE_TILE) + N_CLASSES) * MOE_TILE


def _moe(rec, cls, rank, counts, xa, mod, w1, w3, w2, final_w, sorted_init, B, S, n_ctx, last):
    T = B * S
    tm = MOE_TILE
    n_tiles_max = _sorted_rows(T) // tm
    cnt = counts[:N_CLASSES, 0].astype(jnp.int32)
    tiles = (cnt + tm - 1) // tm
    tile_end = jnp.cumsum(tiles)
    row0 = (tile_end - tiles) * tm
    cls = cls.reshape(T)
    pos = rank.reshape(T)
    for ci in range(N_CLASSES):
        pos = pos + jnp.where(cls == ci, row0[ci], 0)
    n_tiles = tile_end[N_CLASSES - 1:].astype(jnp.int32)
    tile_cls = jnp.sum(jnp.arange(n_tiles_max)[:, None] >= tile_end[None, :], axis=1)
    tile_cls = jnp.minimum(tile_cls, N_CLASSES - 1)
    grp0 = (tile_cls // PAIRS_PER_GROUP) * EXPERTS_PER_GROUP
    pair = tile_cls % PAIRS_PER_GROUP
    lo = sum(jnp.where(pair == p, PAIR_LO[p], 0) for p in range(PAIRS_PER_GROUP))
    hi = sum(jnp.where(pair == p, PAIR_HI[p], 0) for p in range(PAIRS_PER_GROUP))
    tile_experts = jnp.stack([grp0 + lo, grp0 + hi], axis=1).reshape(-1).astype(jnp.int32)
    xs = _dispatch(pos, rec.reshape(T, REC_W), sorted_init)
    ys = _experts(tile_experts, n_tiles, xs, w1, w3, w2)
    if last:
        L = S - n_ctx
        out = _combine_latent(pos, ys, xa.reshape(T, D_MODEL), mod, final_w, B, S, L, n_ctx)
        return out.reshape(B, L, D_MODEL), xs
    out = _combine(pos, ys, xa.reshape(T, D_MODEL), mod, final_w, S, n_ctx, False)
    return out.reshape(B, S, D_MODEL), xs


def _rope_tables(L, n_ctx):
    t = jnp.arange(L)
    row = (t // GRID_W).astype(F32)
    col = (t % GRID_W).astype(F32)
    inv = ROPE_BASE ** (-jnp.arange(0, ROPE_AXIS_DIM, 2, dtype=F32) / ROPE_AXIS_DIM)
    ar = row[:, None] * inv
    ac = col[:, None] * inv
    cos_l = jnp.concatenate([jnp.cos(ar), jnp.cos(ar), jnp.cos(ac), jnp.cos(ac)], axis=1)
    sin_l = jnp.concatenate([-jnp.sin(ar), jnp.sin(ar), -jnp.sin(ac), jnp.sin(ac)], axis=1)
    cos_t = jnp.concatenate([jnp.ones((n_ctx, M_HEAD_DIM), F32), cos_l], axis=0)
    sin_t = jnp.concatenate([jnp.zeros((n_ctx, M_HEAD_DIM), F32), sin_l], axis=0)
    return cos_t, sin_t


def _proj_weight(w_in):
    q_cols = 2 * M_WIDTH + N_WIDTH
    kv0 = q_cols
    g0 = kv0 + 2 * M_WIDTH
    n0 = g0 + G_COLS
    gates = jnp.pad(w_in[:, g0:n0], ((0, 0), (0, G_PAD - G_COLS)))
    return jnp.concatenate([w_in[:, :q_cols], w_in[:, kv0:g0], w_in[:, n0:], gates], axis=1).astype(BF16)


def kernel(x, c, ctx, c_ctx, ada_w, ada_b, norm1_w, w_in, conv_w, conv_b, gate_b, mnorm_w, rpb, w_out,
           norm2_w, router_w, router_b, exp_w1, exp_w3, exp_w2, final_norm_w):
    B, L, _ = x.shape
    n_ctx = ctx.shape[1]
    depth = ada_w.shape[0]
    assert n_ctx == TOK_TILE and L % TOK_TILE == 0 and L % GRID_W == 0
    S = n_ctx + L
    rows = L // GRID_W
    assert rows >= NA_WROWS + 4 and rows % NA_QROWS == 0 and (S // 4) % HALO == 0

    cvec = jnp.concatenate([c, c_ctx[None], jnp.zeros((16 - B - 1, D_MODEL), F32)], axis=0)
    mods = _adaln(cvec, ada_w, ada_b)
    cos_t, sin_t = _rope_tables(L, n_ctx)
    router_wt = router_w.T

    stream = (x, ctx)
    out = None
    sorted_buf = jnp.zeros((_sorted_rows(B * S), REC_W), F32)
    for layer in range(depth):
        last = layer == depth - 1
        m6 = mods[layer].reshape(16, 6, D_MODEL)
        mod = jnp.stack([jnp.broadcast_to(m6[B], (B, 6, D_MODEL)), m6[:B]], axis=1)
        mq, mo, nq, mk, mv, nk, nv, g = _inproj(stream, mod, norm1_w[layer], _proj_weight(w_in[layer]), B, S)
        q, k = _prep(mq, mk, conv_w[layer], conv_b[layer], cos_t, sin_t, B, S)
        hf, hb = _mlstm(q, k, mv, g, gate_b[layer], B, S, n_ctx // CHUNK)
        typ, st, bias = _natten_tables(rpb[layer], rows, n_ctx)
        n_out = _natten(nq, nk, nv, typ, st, bias, B, S, n_ctx)
        xa, rec, cls, rank, counts = _outproj(stream, hf, hb, mo, n_out, mod, mnorm_w[layer],
                                              w_out[layer].astype(BF16), norm2_w[layer], router_wt, router_b, B, S)
        out, sorted_buf = _moe(rec, cls, rank, counts, xa, mod, exp_w1[layer], exp_w3[layer], exp_w2[layer],
                               final_norm_w, sorted_buf, B, S, n_ctx, last)
        stream = (out,)
    return out
```

```python
import functools

import numpy as np
import jax
import jax.numpy as jnp
from jax import lax
from jax.experimental import pallas as pl
from jax.experimental.pallas import tpu as pltpu

D_MODEL = 1024
M_WIDTH = 512
M_HEADS = 4
M_HEAD_DIM = 128
N_WIDTH = 512
N_HEADS = 8
N_HEAD_DIM = 64
G_COLS = 16
GRID_W = 64
WIN_H = 8
WIN_W = 16
CHUNK = 128
SCAN_CHUNKS = 2
ROPE_AXIS_DIM = 64
ROPE_BASE = 10000.0
N_EXPERTS = 16
N_GROUPS = 4
EXPERTS_PER_GROUP = 4
MOE_D_FF = 512
NORM_EPS = 1e-6

LANES = 128
TOK_TILE = 256
HALO = 16
NA_QROWS = 2
NA_WROWS = 10
NA_WKEYS = NA_WROWS * GRID_W
NEG = -1e30
G_PAD = LANES
N_PROJ = 7 * M_WIDTH + G_PAD
REC_W = D_MODEL + LANES
MOE_TILE = 1024
PAIRS_PER_GROUP = EXPERTS_PER_GROUP * (EXPERTS_PER_GROUP - 1) // 2
N_CLASSES = N_GROUPS * PAIRS_PER_GROUP
CLS_ROWS = 32
PAIR_LO = (0, 0, 0, 1, 1, 2)
PAIR_HI = (1, 2, 3, 2, 3, 3)
VMEM_LIMIT = 56 * 1024 * 1024

F32 = jnp.float32
BF16 = jnp.bfloat16


def _cparams(sem):
    return pltpu.CompilerParams(dimension_semantics=sem, vmem_limit_bytes=VMEM_LIMIT)


def _split_bf16(a):
    hi = a.astype(BF16)
    lo = (a - hi.astype(F32)).astype(BF16)
    return hi, lo


def _dot3(a, b):
    ah, al = _split_bf16(a)
    bh, bl = _split_bf16(b)
    d = functools.partial(jnp.dot, preferred_element_type=F32)
    return d(ah, bh) + (d(al, bh) + d(ah, bl))


def _dot_nt(a, b):
    return lax.dot_general(a, b, (((1,), (1,)), ((), ())), preferred_element_type=F32)


def _rms(x, w):
    ms = jnp.mean(x * x, axis=-1, keepdims=True)
    return x * lax.rsqrt(ms + NORM_EPS) * w


def _adaln_kernel(c_ref, w_ref, b_ref, o_ref):
    c = c_ref[...]
    s = c * jax.nn.sigmoid(c)
    o_ref[0] = _dot3(s, w_ref[0]) + b_ref[0]


def _adaln(cvec, ada_w, ada_b):
    depth = ada_w.shape[0]
    n = ada_w.shape[2]
    tn = D_MODEL
    return pl.pallas_call(
        _adaln_kernel,
        grid=(depth, n // tn),
        in_specs=[pl.BlockSpec((16, D_MODEL), lambda l, j: (0, 0)),
                  pl.BlockSpec((1, D_MODEL, tn), lambda l, j: (l, 0, j)),
                  pl.BlockSpec((1, 1, tn), lambda l, j: (l, 0, j))],
        out_specs=pl.BlockSpec((1, 16, tn), lambda l, j: (l, 0, j)),
        out_shape=jax.ShapeDtypeStruct((depth, 16, n), F32),
        compiler_params=_cparams(("arbitrary", "arbitrary")),
        name="adaln",
    )(cvec, ada_w, ada_b.reshape(depth, 1, n))


def _inproj_kernel(*refs, split_input):
    if split_input:
        x_ref, ctx_ref, mod_ref, nw_ref, win_ref = refs[:5]
        outs = refs[5:13]
        xt = jnp.where(pl.program_id(1) == 0, ctx_ref[0], x_ref[0])
    else:
        x_ref, mod_ref, nw_ref, win_ref = refs[:4]
        outs = refs[4:12]
        xt = x_ref[0]
    w_ref = refs[-1]

    @pl.when(jnp.logical_and(pl.program_id(0) == 0, pl.program_id(1) == 0))
    def _():
        g0 = 5 * M_WIDTH
        n0 = g0 + G_COLS
        for k in range(5):
            cs = slice(k * M_WIDTH, (k + 1) * M_WIDTH)
            w_ref[:, cs] = win_ref[:, cs].astype(BF16)
        for k in range(2):
            w_ref[:, g0 + k * M_WIDTH:g0 + (k + 1) * M_WIDTH] = (
                win_ref[:, n0 + k * M_WIDTH:n0 + (k + 1) * M_WIDTH].astype(BF16))
        w_ref[:, 7 * M_WIDTH:] = win_ref[:, g0:g0 + G_PAD].astype(BF16)

    h = _rms(xt, nw_ref[...]) * (1.0 + mod_ref[0, 0, 1:2, :]) + mod_ref[0, 0, 0:1, :]
    hb = h.astype(BF16)
    for k in range(7):
        outs[k][0] = jnp.dot(hb, w_ref[:, k * M_WIDTH:(k + 1) * M_WIDTH],
                             preferred_element_type=F32).astype(BF16)
    outs[7][0] = jnp.dot(hb, w_ref[:, 7 * M_WIDTH:], preferred_element_type=F32)


def _stream_specs(split_input):
    tm = TOK_TILE
    if split_input:
        return [pl.BlockSpec((1, tm, D_MODEL), lambda b, j: (b, jnp.maximum(j - 1, 0), 0)),
                pl.BlockSpec((1, tm, D_MODEL), lambda b, j: (b, 0, 0))]
    return [pl.BlockSpec((1, tm, D_MODEL), lambda b, j: (b, j, 0))]


def _mod_spec():
    return pl.BlockSpec((1, 1, 6, D_MODEL), lambda b, j: (b, jnp.minimum(j, 1), 0, 0))


def _inproj(stream, mod, norm_w, w_in, B, S):
    split_input = len(stream) == 2
    tm = TOK_TILE
    tok = lambda w: pl.BlockSpec((1, tm, w), lambda b, j: (b, j, 0))
    out_shape = [jax.ShapeDtypeStruct((B, S, M_WIDTH), BF16)] * 7 + [jax.ShapeDtypeStruct((B, S, G_PAD), F32)]
    return pl.pallas_call(
        functools.partial(_inproj_kernel, split_input=split_input),
        grid=(B, S // tm),
        in_specs=_stream_specs(split_input) + [
            _mod_spec(),
            pl.BlockSpec((1, D_MODEL), lambda b, j: (0, 0)),
            pl.BlockSpec(w_in.shape, lambda b, j: (0, 0), pipeline_mode=pl.Buffered(1))],
        out_specs=[tok(M_WIDTH)] * 7 + [tok(G_PAD)],
        out_shape=out_shape,
        scratch_shapes=[pltpu.VMEM((D_MODEL, N_PROJ), BF16)],
        compiler_params=_cparams(("arbitrary", "arbitrary")),
        name="inproj",
    )(*stream, mod, norm_w.reshape(1, D_MODEL), w_in)


def _prep_kernel(q_ref, qp_ref, qn_ref, k_ref, kp_ref, kn_ref, cw_ref, cb_ref, cos_ref, sin_ref,
                 qo_ref, ko_ref, *, n_tiles):
    j = pl.program_id(1)
    tp = TOK_TILE
    has_prev = jnp.logical_and(j != 0, j != 1).astype(F32)
    has_next = jnp.logical_and(j != 0, j != n_tiles - 1).astype(F32)
    rows = lax.broadcasted_iota(jnp.int32, (tp, M_WIDTH), 0)
    lanes = lax.broadcasted_iota(jnp.int32, (tp, M_WIDTH), 1)
    low_half = (lanes % (ROPE_AXIS_DIM)) < (ROPE_AXIS_DIM // 2)
    cosv = jnp.concatenate([cos_ref[...]] * M_HEADS, axis=1)
    sinv = jnp.concatenate([sin_ref[...]] * M_HEADS, axis=1)

    def branch(x_ref, p_ref, n_ref, col0):
        x = x_ref[0].astype(F32)
        prev_row = p_ref[0, HALO - 1:HALO, :].astype(F32) * has_prev
        next_row = n_ref[0, 0:1, :].astype(F32) * has_next
        xm = jnp.where(rows == 0, prev_row, pltpu.roll(x, 1, 0))
        xp = jnp.where(rows == tp - 1, next_row, pltpu.roll(x, tp - 1, 0))
        w = cw_ref[:, col0:col0 + M_WIDTH]
        y = cb_ref[:, col0:col0 + M_WIDTH] + xm * w[0:1] + x * w[1:2] + xp * w[2:3]
        y = y * jax.nn.sigmoid(y)
        half = ROPE_AXIS_DIM // 2
        partner = jnp.where(low_half, pltpu.roll(y, M_WIDTH - half, 1), pltpu.roll(y, half, 1))
        return y * cosv + partner * sinv

    qo_ref[0] = branch(q_ref, qp_ref, qn_ref, 0).astype(BF16)
    ko_ref[0] = (branch(k_ref, kp_ref, kn_ref, M_WIDTH) * (M_HEAD_DIM ** -0.5)).T.astype(BF16)


def _prep(mq, mk, conv_w, conv_b, cos_t, sin_t, B, S):
    tp = TOK_TILE
    n_tiles = S // tp
    per = tp // HALO
    n_halo = S // HALO
    main = pl.BlockSpec((1, tp, M_WIDTH), lambda b, j: (b, j, 0))
    prev = pl.BlockSpec((1, HALO, M_WIDTH), lambda b, j: (b, jnp.maximum(j * per - 1, 0), 0))
    nxt = pl.BlockSpec((1, HALO, M_WIDTH), lambda b, j: (b, jnp.minimum((j + 1) * per, n_halo - 1), 0))
    return pl.pallas_call(
        functools.partial(_prep_kernel, n_tiles=n_tiles),
        grid=(B, n_tiles),
        in_specs=[main, prev, nxt, main, prev, nxt,
                  pl.BlockSpec((3, 2 * M_WIDTH), lambda b, j: (0, 0)),
                  pl.BlockSpec((1, 2 * M_WIDTH), lambda b, j: (0, 0)),
                  pl.BlockSpec((tp, M_HEAD_DIM), lambda b, j: (j, 0)),
                  pl.BlockSpec((tp, M_HEAD_DIM), lambda b, j: (j, 0))],
        out_specs=[main, pl.BlockSpec((1, M_WIDTH, tp), lambda b, j: (b, 0, j))],
        out_shape=[jax.ShapeDtypeStruct((B, S, M_WIDTH), BF16), jax.ShapeDtypeStruct((B, M_WIDTH, S), BF16)],
        compiler_params=_cparams(("arbitrary", "arbitrary")),
        name="mlstm_prep",
    )(mq, mq, mq, mk, mk, mk, conv_w, conv_b.reshape(1, -1), cos_t, sin_t)


def _scan_rows(x, reverse, op, fill):
    n = x.shape[0]
    rows = lax.broadcasted_iota(jnp.int32, x.shape, 0)
    sh = 1
    while sh < n:
        if reverse:
            x = op(x, jnp.where(rows < n - sh, pltpu.roll(x, n - sh, 0), fill))
        else:
            x = op(x, jnp.where(rows >= sh, pltpu.roll(x, sh, 0), fill))
        sh *= 2
    return x


def _log_sigmoid(x):
    return jnp.minimum(x, 0.0) - jnp.log(1.0 + jnp.exp(-jnp.abs(x)))


def _mlstm_kernel(qf_ref, kf_ref, vf_ref, gf_ref, qb_ref, kb_ref, vb_ref, gb_ref, gbias_ref,
                  hf_ref, hb_ref, cn_ref, m_ref):
    c = pl.program_id(1)

    @pl.when(c == 0)
    def _():
        cn_ref[...] = jnp.zeros_like(cn_ref)
        m_ref[...] = jnp.zeros_like(m_ref)

    t = CHUNK
    r_i = lax.broadcasted_iota(jnp.int32, (t, t), 0)
    c_i = lax.broadcasted_iota(jnp.int32, (t, t), 1)
    ones_blk = jnp.ones((t, LANES), BF16)

    dirs = ((qf_ref, kf_ref, vf_ref, gf_ref, hf_ref), (qb_ref, kb_ref, vb_ref, gb_ref, hb_ref))
    for step, d in ((s, d) for s in range(SCAN_CHUNKS) for d in range(2)):
        q_ref, k_ref, v_ref, g_ref, h_ref = dirs[d]
        reverse = d == 1
        ts = pl.ds((SCAN_CHUNKS - 1 - step if reverse else step) * t, t)
        mask = (c_i >= r_i) if reverse else (c_i <= r_i)
        end_row = 0 if reverse else t - 1
        g = g_ref[0, ts, :] + gbias_ref[...]
        bc = _scan_rows(_log_sigmoid(g), reverse, jnp.add, 0.0)
        b_al = pltpu.roll(bc, LANES - M_HEADS, 1)
        a = g - b_al
        m_old = m_ref[d][0:1, :]
        mx = jnp.maximum(m_old, _scan_rows(a, reverse, jnp.maximum, NEG))
        m_end = mx[end_row:end_row + 1, :]
        c_t = -mx
        e_den = jnp.exp(-(b_al + mx))
        w_old = jnp.exp(m_old - m_end)
        a_t = a.T
        w_tok_t = jnp.exp(a - m_end).T
        m_ref[d] = jnp.broadcast_to(b_al[end_row:end_row + 1, :] + m_end, (8, LANES))
        for h in range(M_HEADS):
            icol = 2 * M_HEADS * d + h
            hs = slice(h * M_HEAD_DIM, (h + 1) * M_HEAD_DIM)
            q = q_ref[0, ts, hs]
            k_t = k_ref[0, hs, ts]
            v_ext = jnp.concatenate([v_ref[0, ts, hs], ones_blk], axis=1)
            cn_old = cn_ref[d, h]

            c_b = jnp.broadcast_to(c_t[:, icol:icol + 1], (t, t))
            expo = jnp.concatenate([jnp.where(mask, c_b + a_t[icol:icol + 1, :], NEG),
                                    c_b + m_old[:, icol:icol + 1]], axis=1)
            qk = jnp.dot(q, k_t, preferred_element_type=F32)
            s_ext = (jnp.concatenate([qk, q.astype(F32)], axis=1) * jnp.exp(expo)).astype(BF16)
            rhs = jnp.concatenate([v_ext, cn_old.astype(BF16)], axis=0)
            ext = jnp.dot(s_ext, rhs, preferred_element_type=F32)
            den = jnp.maximum(jnp.abs(ext[:, M_HEAD_DIM:]),
                              jnp.broadcast_to(e_den[:, icol:icol + 1], (t, LANES)))
            h_ref[0, ts, hs] = (ext[:, :M_HEAD_DIM] / den).astype(BF16)

            kw_t = (k_t.astype(F32) * w_tok_t[icol:icol + 1, :]).astype(BF16)
            cn_ref[d, h] = (w_old[:, icol:icol + 1] * cn_old
                            + jnp.dot(kw_t, v_ext, preferred_element_type=F32))


def _mlstm(q, k, v, g, gate_b, B, S, n_ctx_chunks):
    rows = SCAN_CHUNKS * CHUNK
    assert n_ctx_chunks % SCAN_CHUNKS == 0 and S % rows == 0
    nc = S // rows
    ncx = n_ctx_chunks // SCAN_CHUNKS

    def bwd_chunk(c):
        return jnp.where(c < ncx, ncx - 1 - c, nc - 1 + ncx - c)

    fw = lambda w: pl.BlockSpec((1, rows, w), lambda b, c: (b, c, 0))
    bw = lambda w: pl.BlockSpec((1, rows, w), lambda b, c: (b, bwd_chunk(c), 0))
    fw_t = pl.BlockSpec((1, M_WIDTH, rows), lambda b, c: (b, 0, c))
    bw_t = pl.BlockSpec((1, M_WIDTH, rows), lambda b, c: (b, 0, bwd_chunk(c)))
    gbias = jnp.zeros((1, G_PAD), F32).at[0, :G_COLS].set(gate_b)
    return pl.pallas_call(
        _mlstm_kernel,
        grid=(B, nc),
        in_specs=[fw(M_WIDTH), fw_t, fw(M_WIDTH), fw(G_PAD),
                  bw(M_WIDTH), bw_t, bw(M_WIDTH), bw(G_PAD),
                  pl.BlockSpec((1, G_PAD), lambda b, c: (0, 0))],
        out_specs=[fw(M_WIDTH), bw(M_WIDTH)],
        out_shape=[jax.ShapeDtypeStruct((B, S, M_WIDTH), BF16)] * 2,
        scratch_shapes=[pltpu.VMEM((2, M_HEADS, M_HEAD_DIM, 2 * LANES), F32),
                        pltpu.VMEM((2, 8, LANES), F32)],
        compiler_params=_cparams(("arbitrary", "arbitrary")),
        name="mlstm_scan",
    )(q, k, v, g, q, k, v, g, gbias)


def _natten_kernel(typ_ref, st_ref, q_ref, k_ref, v_ref, bias_ref, o_ref, *, n_ctx):
    j = pl.program_id(2)
    sub = NA_QROWS * GRID_W
    lane = lax.broadcasted_iota(jnp.int32, (sub, LANES), 1)
    kctx = k_ref[0, 0:n_ctx, :]
    vctx = v_ref[0, 0:n_ctx, :]
    for i in range(TOK_TILE // sub):
        sb = j * (TOK_TILE // sub) + i
        typ = typ_ref[sb]
        st = pl.multiple_of(st_ref[sb], GRID_W)
        q = q_ref[0, i * sub:(i + 1) * sub, :] * (N_HEAD_DIM ** -0.5)
        kwin = k_ref[0, pl.ds(st, NA_WKEYS), :]
        vwin = v_ref[0, pl.ds(st, NA_WKEYS), :]
        first = lane < N_HEAD_DIM
        zero = jnp.zeros_like(q)
        q2 = jnp.concatenate([jnp.where(first, q, zero), jnp.where(first, zero, q)], axis=0)
        s_win = _dot_nt(q2, kwin) + bias_ref[typ].reshape(2 * sub, NA_WKEYS)
        s_ctx = _dot_nt(q2, kctx)
        m = jnp.maximum(jnp.max(s_win, axis=-1, keepdims=True), jnp.max(s_ctx, axis=-1, keepdims=True))
        p_win = jnp.exp(s_win - m)
        p_ctx = jnp.exp(s_ctx - m)
        l = jnp.sum(p_win, axis=-1, keepdims=True) + jnp.sum(p_ctx, axis=-1, keepdims=True)
        o = (jnp.dot(p_win.astype(BF16), vwin, preferred_element_type=F32)
             + jnp.dot(p_ctx.astype(BF16), vctx, preferred_element_type=F32)) / l
        o_ref[0, i * sub:(i + 1) * sub, :] = jnp.where(first, o[:sub], o[sub:]).astype(BF16)


def _natten_tables(rpb, rows, n_ctx):
    n_heads = rpb.shape[0]
    n_sb = rows // NA_QROWS
    ws_of = lambda sb: int(np.clip(NA_QROWS * sb - WIN_H // 2, 0, rows - NA_WROWS))
    rs_of = lambda r: int(np.clip(r - WIN_H // 2, 0, rows - WIN_H))
    shape_of = lambda sb: tuple((NA_QROWS * sb + qr - ws_of(sb), rs_of(NA_QROWS * sb + qr) - ws_of(sb))
                                for qr in range(NA_QROWS))
    shapes = sorted(set(shape_of(sb) for sb in range(n_sb)))
    reps = [next(sb for sb in range(n_sb) if shape_of(sb) == sh) for sh in shapes]
    n_dr = 2 * WIN_H - 1
    n_dc = 2 * WIN_W - 1
    lead = GRID_W - WIN_W
    vec = jnp.pad(rpb, ((0, 0), (0, 0), (lead, 2 * GRID_W - lead - n_dc)), constant_values=NEG)
    skew = jnp.broadcast_to(vec[:, :, None, :], (n_heads, n_dr, GRID_W, 2 * GRID_W))
    skew = skew.reshape(n_heads, n_dr, -1)[:, :, :GRID_W * (2 * GRID_W - 1)]
    toep = skew.reshape(n_heads, n_dr, GRID_W, 2 * GRID_W - 1)[..., GRID_W - 1:]
    cq = np.arange(GRID_W)[:, None]
    ck = np.arange(GRID_W)[None, :]
    cs = np.clip(cq - WIN_W // 2, 0, GRID_W - WIN_W)
    toep = jnp.where((ck >= cs) & (ck < cs + WIN_W), toep, NEG)
    masked_tile = jnp.full((n_heads, GRID_W, GRID_W), NEG, F32)
    types = []
    for sb in reps:
        ws = ws_of(sb)
        q_rows = []
        for qr in range(NA_QROWS):
            r = NA_QROWS * sb + qr
            rs = rs_of(r)
            tiles = []
            for kw in range(NA_WROWS):
                kr = ws + kw
                tiles.append(toep[:, kr - r + WIN_H - 1] if rs <= kr < rs + WIN_H else masked_tile)
            q_rows.append(jnp.concatenate(tiles, axis=-1))
        types.append(jnp.concatenate(q_rows, axis=-2))
    types.append(jnp.full_like(types[0], NEG))
    bias = jnp.stack(types, axis=0)
    n_ctx_sb = n_ctx // (NA_QROWS * GRID_W)
    typ, st = [len(shapes)] * n_ctx_sb, [n_ctx] * n_ctx_sb
    for sb in range(n_sb):
        typ.append(shapes.index(shape_of(sb)))
        st.append(n_ctx + ws_of(sb) * GRID_W)
    return jnp.asarray(typ, jnp.int32), jnp.asarray(st, jnp.int32), bias


def _natten(nq, nk, nv, typ, st, bias, layer, B, S, n_ctx):
    n_pairs = N_HEADS // 2
    grid_spec = pltpu.PrefetchScalarGridSpec(
        num_scalar_prefetch=2,
        grid=(n_pairs, B, S // TOK_TILE),
        in_specs=[pl.BlockSpec((1, TOK_TILE, LANES), lambda p, b, j, *_: (b, j, p)),
                  pl.BlockSpec((1, S, LANES), lambda p, b, j, *_: (b, 0, p)),
                  pl.BlockSpec((1, S, LANES), lambda p, b, j, *_: (b, 0, p)),
                  pl.BlockSpec((bias.shape[0], 2, NA_QROWS * GRID_W, NA_WKEYS),
                               lambda p, b, j, *_: (0, layer * n_pairs + p, 0, 0))],
        out_specs=pl.BlockSpec((1, TOK_TILE, LANES), lambda p, b, j, *_: (b, j, p)),
    )
    return pl.pallas_call(
        functools.partial(_natten_kernel, n_ctx=n_ctx),
        grid_spec=grid_spec,
        out_shape=jax.ShapeDtypeStruct((B, S, N_WIDTH), BF16),
        compiler_params=_cparams(("arbitrary", "arbitrary", "arbitrary")),
        name="natten",
    )(typ, st, nq, nk, nv, bias)


def _route(logits_t, rb):
    e_i = lax.broadcasted_iota(jnp.int32, logits_t.shape, 0)
    z = logits_t - jnp.max(logits_t, axis=0, keepdims=True)
    ez = jnp.exp(z)
    scores = ez / jnp.sum(ez, axis=0, keepdims=True)
    sel = scores + rb
    best = None
    best_score = None
    for gi in range(N_GROUPS):
        r = [sel[gi * EXPERTS_PER_GROUP + u:gi * EXPERTS_PER_GROUP + u + 1, :] for u in range(EXPERTS_PER_GROUP)]
        gs = None
        for u in range(EXPERTS_PER_GROUP):
            for w in range(u + 1, EXPERTS_PER_GROUP):
                pair = r[u] + r[w]
                gs = pair if gs is None else jnp.maximum(gs, pair)
        if best is None:
            best, best_score = jnp.zeros(gs.shape, jnp.int32), gs
        else:
            better = gs > best_score
            best = jnp.where(better, gi, best)
            best_score = jnp.where(better, gs, best_score)
    masked = jnp.where((e_i // EXPERTS_PER_GROUP) == best, sel, -jnp.inf)
    v1 = jnp.max(masked, axis=0, keepdims=True)
    i1 = jnp.min(jnp.where(masked == v1, e_i, N_EXPERTS), axis=0, keepdims=True)
    masked2 = jnp.where(e_i == i1, -jnp.inf, masked)
    v2 = jnp.max(masked2, axis=0, keepdims=True)
    i2 = jnp.min(jnp.where(masked2 == v2, e_i, N_EXPERTS), axis=0, keepdims=True)
    w1 = jnp.sum(jnp.where(e_i == i1, scores, 0.0), axis=0, keepdims=True)
    w2 = jnp.sum(jnp.where(e_i == i2, scores, 0.0), axis=0, keepdims=True)
    tot = w1 + w2
    gate = jnp.where(e_i == i1, w1 / tot, 0.0) + jnp.where(e_i == i2, w2 / tot, 0.0)
    lo = jnp.minimum(i1, i2) - best * EXPERTS_PER_GROUP
    hi = jnp.maximum(i1, i2) - best * EXPERTS_PER_GROUP
    pair = ((lo * (2 * EXPERTS_PER_GROUP - 1 - lo)) >> 1) + (hi - lo - 1)
    return gate, best * PAIRS_PER_GROUP + pair


def _outproj_kernel(*refs, split_input):
    n_stream = 2 if split_input else 1
    stream = refs[:n_stream]
    (hf_ref, hb_ref, mo_ref, no_ref, mod_ref, mnw_ref, wo_ref, n2w_ref, rw_ref, rb_ref,
     xo_ref, rec_ref, grp_ref, rank_ref, cnt_ref, base_ref) = refs[n_stream:]

    @pl.when(jnp.logical_and(pl.program_id(0) == 0, pl.program_id(1) == 0))
    def _():
        base_ref[...] = jnp.zeros_like(base_ref)

    rh, rl = _split_bf16(rw_ref[...])
    n_rows = xo_ref.shape[1]
    half = n_rows // 2
    logits = []
    for r0 in (0, half):
        rs = slice(r0, r0 + half)
        if split_input:
            xt = jnp.where(pl.program_id(1) == 0, stream[1][0, rs], stream[0][0, rs])
        else:
            xt = stream[0][0, rs]
        hsum = hf_ref[0, rs].astype(F32) + hb_ref[0, rs].astype(F32)
        parts = []
        for h in range(M_HEADS):
            hs = slice(h * M_HEAD_DIM, (h + 1) * M_HEAD_DIM)
            parts.append(_rms(hsum[:, hs], mnw_ref[:, hs]))
        m_out = jnp.concatenate(parts, axis=1) * jax.nn.sigmoid(mo_ref[0, rs].astype(F32))
        y = (jnp.dot(m_out.astype(BF16), wo_ref[0:M_WIDTH, :], preferred_element_type=F32)
             + jnp.dot(no_ref[0, rs], wo_ref[M_WIDTH:, :], preferred_element_type=F32))
        x_new = xt + mod_ref[0, 0, 2:3, :] * y
        xo_ref[0, rs] = x_new
        hx = _rms(x_new, n2w_ref[...]) * (1.0 + mod_ref[0, 0, 4:5, :]) + mod_ref[0, 0, 3:4, :]
        rec_ref[0, rs, :D_MODEL] = hx
        hh, hl = _split_bf16(hx)
        logits.append(_dot_nt(rh, hh) + (_dot_nt(rh, hl) + _dot_nt(rl, hh)))
    logits_t = jnp.concatenate(logits, axis=1)
    gate_t, cls = _route(logits_t, rb_ref[...])
    pad = jnp.zeros((LANES - N_EXPERTS, gate_t.shape[1]), F32)
    rec_ref[0, :, D_MODEL:] = jnp.concatenate([gate_t, pad], axis=0).T
    tm = cls.shape[1]
    onehot = lax.broadcasted_iota(jnp.int32, (CLS_ROWS, tm), 0) == cls
    upper = (lax.broadcasted_iota(jnp.int32, (tm, tm), 0) <= lax.broadcasted_iota(jnp.int32, (tm, tm), 1))
    csum = jnp.dot(onehot.astype(BF16), upper.astype(BF16), preferred_element_type=F32)
    base = base_ref[:, 0:1]
    rank = jnp.sum(jnp.where(onehot, csum - 1.0 + base, 0.0), axis=0, keepdims=True)
    grp_ref[0, 0] = cls
    rank_ref[0, 0] = rank.astype(jnp.int32)
    total = base_ref[...] + csum[:, tm - 1:tm]
    base_ref[...] = total
    cnt_ref[...] = total


def _outproj(stream, hf, hb, mo, n_out, mod, mnorm_w, w_out, norm2_w, router_wt, router_b, B, S):
    split_input = len(stream) == 2
    tm = TOK_TILE
    nt = S // tm
    tok = lambda w: pl.BlockSpec((1, tm, w), lambda b, j: (b, j, 0))
    idx = pl.BlockSpec((1, 1, 1, tm), lambda b, j: (b, j, 0, 0))
    const = lambda shape: pl.BlockSpec(shape, lambda b, j: (0,) * len(shape))
    return pl.pallas_call(
        functools.partial(_outproj_kernel, split_input=split_input),
        grid=(B, nt),
        in_specs=_stream_specs(split_input) + [
            tok(M_WIDTH), tok(M_WIDTH), tok(M_WIDTH), tok(N_WIDTH), _mod_spec(),
            const((1, M_WIDTH)), const((D_MODEL, D_MODEL)), const((1, D_MODEL)),
            const((N_EXPERTS, D_MODEL)), const((N_EXPERTS, 1))],
        out_specs=[tok(D_MODEL), tok(REC_W), idx, idx, const((CLS_ROWS, LANES))],
        out_shape=[jax.ShapeDtypeStruct((B, S, D_MODEL), F32),
                   jax.ShapeDtypeStruct((B, S, REC_W), F32),
                   jax.ShapeDtypeStruct((B, nt, 1, tm), jnp.int32),
                   jax.ShapeDtypeStruct((B, nt, 1, tm), jnp.int32),
                   jax.ShapeDtypeStruct((CLS_ROWS, LANES), F32)],
        scratch_shapes=[pltpu.VMEM((CLS_ROWS, LANES), F32)],
        compiler_params=_cparams(("arbitrary", "arbitrary")),
        name="outproj_router",
    )(*stream, hf, hb, mo, n_out, mod, mnorm_w.reshape(1, -1), w_out, norm2_w.reshape(1, -1),
      router_wt, router_b.reshape(-1, 1))


def _row_copy_wait(src, dst, sem, n_rows):
    pltpu.make_async_copy(src.at[pl.ds(0, n_rows)], dst.at[pl.ds(0, n_rows)], sem).wait()


def _dispatch_kernel(pos_ref, rec_ref, init_ref, xs_ref, sem):
    del init_ref
    n = rec_ref.shape[0]

    def issue(r, carry):
        pltpu.make_async_copy(rec_ref.at[pl.ds(r, 1)], xs_ref.at[pl.ds(pos_ref[r], 1)], sem).start()
        return carry

    lax.fori_loop(0, n, issue, 0, unroll=8)
    _row_copy_wait(rec_ref, xs_ref, sem, n)


def _dispatch(pos, rec, init):
    T, w = rec.shape
    n_sorted = init.shape[0]
    tp = _flat_tile(T)
    return pl.pallas_call(
        _dispatch_kernel,
        grid=(T // tp,),
        in_specs=[pl.BlockSpec((tp,), lambda i: (i,), memory_space=pltpu.SMEM),
                  pl.BlockSpec((tp, w), lambda i: (i, 0)),
                  pl.BlockSpec(memory_space=pl.ANY)],
        out_specs=pl.BlockSpec(memory_space=pl.ANY),
        out_shape=jax.ShapeDtypeStruct((n_sorted, w), F32),
        scratch_shapes=[pltpu.SemaphoreType.DMA(())],
        input_output_aliases={2: 0},
        compiler_params=_cparams(("arbitrary",)),
        name="moe_dispatch",
    )(pos, rec, init)


def _experts_kernel(te_ref, nt_ref, xs_ref, w1_ref, w3_ref, w2_ref, ys_ref, acc_ref):
    i = pl.program_id(0)
    k = pl.program_id(1)

    @pl.when(k == 0)
    def _():
        acc_ref[...] = jnp.zeros_like(acc_ref)

    @pl.when(i < nt_ref[0])
    def _():
        e = te_ref[2 * i + k]
        h = xs_ref[:, :D_MODEL].astype(BF16)
        gate = xs_ref[:, D_MODEL:]
        lane = lax.broadcasted_iota(jnp.int32, gate.shape, 1)
        g_e = jnp.sum(jnp.where(lane == e, gate, 0.0), axis=-1, keepdims=True)
        a = jnp.dot(h, w1_ref[0].astype(BF16), preferred_element_type=F32)
        b3 = jnp.dot(h, w3_ref[0].astype(BF16), preferred_element_type=F32)
        act = (a * jax.nn.sigmoid(a) * b3 * g_e).astype(BF16)
        acc_ref[...] += jnp.dot(act, w2_ref[0].astype(BF16), preferred_element_type=F32)

    @pl.when(k == 1)
    def _():
        ys_ref[...] = acc_ref[...]


def _experts(tile_experts, n_tiles, xs, w1, w3, w2):
    n_sorted, w = xs.shape
    tm = MOE_TILE
    expert = lambda i, k, te, nt: (te[2 * jnp.minimum(i, nt[0] - 1) + k], 0, 0)
    rows = lambda i, k, te, nt: (jnp.minimum(i, nt[0] - 1), 0)
    grid_spec = pltpu.PrefetchScalarGridSpec(
        num_scalar_prefetch=2,
        grid=(n_sorted // tm, 2),
        in_specs=[pl.BlockSpec((tm, w), rows),
                  pl.BlockSpec((1, D_MODEL, MOE_D_FF), expert),
                  pl.BlockSpec((1, D_MODEL, MOE_D_FF), expert),
                  pl.BlockSpec((1, MOE_D_FF, D_MODEL), expert)],
        out_specs=pl.BlockSpec((tm, D_MODEL), lambda i, k, *_: (i, 0)),
        scratch_shapes=[pltpu.VMEM((tm, D_MODEL), F32)],
    )
    return pl.pallas_call(
        _experts_kernel,
        grid_spec=grid_spec,
        out_shape=jax.ShapeDtypeStruct((n_sorted, D_MODEL), F32),
        compiler_params=_cparams(("arbitrary", "arbitrary")),
        name="moe_experts",
    )(tile_experts, n_tiles, xs, w1, w3, w2)


def _combine_kernel(pos_ref, ys_ref, x_ref, mod_ref, fw_ref, o_ref, y_buf, sem, *, S, n_ctx, final_norm):
    n = y_buf.shape[0]

    def issue(r, carry):
        pltpu.make_async_copy(ys_ref.at[pl.ds(pos_ref[r], 1)], y_buf.at[pl.ds(r, 1)], sem).start()
        return carry

    lax.fori_loop(0, n, issue, 0, unroll=8)
    t0 = pl.program_id(0) * n
    b0 = t0 // S
    b1 = jnp.minimum(b0 + 1, mod_ref.shape[0] - 1)
    tok = t0 + lax.broadcasted_iota(jnp.int32, (n, 1), 0)
    second = tok >= (b0 + 1) * S
    is_ctx = (tok - jnp.where(second, b0 + 1, b0) * S) < n_ctx
    gate_mod = jnp.where(second,
                         jnp.where(is_ctx, mod_ref[b1, 0, 5:6, :], mod_ref[b1, 1, 5:6, :]),
                         jnp.where(is_ctx, mod_ref[b0, 0, 5:6, :], mod_ref[b0, 1, 5:6, :]))
    _row_copy_wait(ys_ref, y_buf, sem, n)
    x_new = x_ref[...] + gate_mod * y_buf[...]
    if final_norm:
        x_new = _rms(x_new, fw_ref[...])
    o_ref[...] = x_new


def _combine(pos, ys, xa, mod, final_w, S, n_ctx, final_norm):
    T = xa.shape[0]
    tu = _flat_tile(T)
    assert tu <= S
    return pl.pallas_call(
        functools.partial(_combine_kernel, S=S, n_ctx=n_ctx, final_norm=final_norm),
        grid=(T // tu,),
        in_specs=[pl.BlockSpec((tu,), lambda i: (i,), memory_space=pltpu.SMEM),
                  pl.BlockSpec(memory_space=pl.ANY),
                  pl.BlockSpec((tu, D_MODEL), lambda i: (i, 0)),
                  pl.BlockSpec(mod.shape, lambda i: (0, 0, 0, 0)),
                  pl.BlockSpec((1, D_MODEL), lambda i: (0, 0))],
        out_specs=pl.BlockSpec((tu, D_MODEL), lambda i: (i, 0)),
        out_shape=jax.ShapeDtypeStruct((T, D_MODEL), F32),
        scratch_shapes=[pltpu.VMEM((tu, D_MODEL), F32), pltpu.SemaphoreType.DMA(())],
        compiler_params=_cparams(("arbitrary",)),
        name="moe_combine",
    )(pos, ys, xa, mod, final_w.reshape(1, -1))


def _combine_latent_kernel(pos_ref, ys_ref, xa_ref, mod_ref, fw_ref, o_ref, y_buf, x_buf, sem, x_sem,
                           *, S, L, n_ctx):
    n = y_buf.shape[0]
    per_batch = L // n
    b = pl.program_id(0) // per_batch
    row0 = b * S + n_ctx + (pl.program_id(0) % per_batch) * n
    x_copy = pltpu.make_async_copy(xa_ref.at[pl.ds(row0, n)], x_buf, x_sem)
    x_copy.start()

    def issue(r, carry):
        pltpu.make_async_copy(ys_ref.at[pl.ds(pos_ref[r], 1)], y_buf.at[pl.ds(r, 1)], sem).start()
        return carry

    lax.fori_loop(0, n, issue, 0, unroll=8)
    x_copy.wait()
    _row_copy_wait(ys_ref, y_buf, sem, n)
    o_ref[...] = _rms(x_buf[...] + mod_ref[b, 1, 5:6, :] * y_buf[...], fw_ref[...])


def _combine_latent(pos, ys, xa, mod, final_w, B, S, L, n_ctx):
    tu = _flat_tile(L)
    pos_lat = pos.reshape(B, S)[:, n_ctx:].reshape(B * L)
    return pl.pallas_call(
        functools.partial(_combine_latent_kernel, S=S, L=L, n_ctx=n_ctx),
        grid=(B * L // tu,),
        in_specs=[pl.BlockSpec((tu,), lambda i: (i,), memory_space=pltpu.SMEM),
                  pl.BlockSpec(memory_space=pl.ANY),
                  pl.BlockSpec(memory_space=pl.ANY),
                  pl.BlockSpec(mod.shape, lambda i: (0, 0, 0, 0)),
                  pl.BlockSpec((1, D_MODEL), lambda i: (0, 0))],
        out_specs=pl.BlockSpec((tu, D_MODEL), lambda i: (i, 0)),
        out_shape=jax.ShapeDtypeStruct((B * L, D_MODEL), F32),
        scratch_shapes=[pltpu.VMEM((tu, D_MODEL), F32), pltpu.VMEM((tu, D_MODEL), F32),
                        pltpu.SemaphoreType.DMA(()), pltpu.SemaphoreType.DMA(())],
        compiler_params=_cparams(("arbitrary",)),
        name="moe_combine_latent",
    )(pos_lat, ys, xa, mod, final_w.reshape(1, -1))


def _flat_tile(T):
    return MOE_TILE if T % MOE_TILE == 0 else TOK_TILE


def _sorted_rows(T):
    return (-(-T // MOE_TILE) + N_CLASSES) * MOE_TILE


def _moe(rec, cls, rank, counts, xa, mod, w1, w3, w2, final_w, sorted_init, B, S, n_ctx, last):
    T = B * S
    tm = MOE_TILE
    n_tiles_max = _sorted_rows(T) // tm
    cnt = counts[:N_CLASSES, 0].astype(jnp.int32)
    tiles = (cnt + tm - 1) // tm
    tile_end = jnp.cumsum(tiles)
    row0 = (tile_end - tiles) * tm
    cls = cls.reshape(T)
    pos = rank.reshape(T)
    for ci in range(N_CLASSES):
        pos = pos + jnp.where(cls == ci, row0[ci], 0)
    n_tiles = tile_end[N_CLASSES - 1:].astype(jnp.int32)
    tile_cls = jnp.sum(jnp.arange(n_tiles_max)[:, None] >= tile_end[None, :], axis=1)
    tile_cls = jnp.minimum(tile_cls, N_CLASSES - 1)
    grp0 = (tile_cls // PAIRS_PER_GROUP) * EXPERTS_PER_GROUP
    pair = tile_cls % PAIRS_PER_GROUP
    lo = sum(jnp.where(pair == p, PAIR_LO[p], 0) for p in range(PAIRS_PER_GROUP))
    hi = sum(jnp.where(pair == p, PAIR_HI[p], 0) for p in range(PAIRS_PER_GROUP))
    tile_experts = jnp.stack([grp0 + lo, grp0 + hi], axis=1).reshape(-1).astype(jnp.int32)
    xs = _dispatch(pos, rec.reshape(T, REC_W), sorted_init)
    ys = _experts(tile_experts, n_tiles, xs, w1, w3, w2)
    if last:
        L = S - n_ctx
        out = _combine_latent(pos, ys, xa.reshape(T, D_MODEL), mod, final_w, B, S, L, n_ctx)
        return out.reshape(B, L, D_MODEL), xs
    out = _combine(pos, ys, xa.reshape(T, D_MODEL), mod, final_w, S, n_ctx, False)
    return out.reshape(B, S, D_MODEL), xs


def _rope_tables(L, n_ctx):
    t = jnp.arange(L)
    row = (t // GRID_W).astype(F32)
    col = (t % GRID_W).astype(F32)
    inv = ROPE_BASE ** (-jnp.arange(0, ROPE_AXIS_DIM, 2, dtype=F32) / ROPE_AXIS_DIM)
    ar = row[:, None] * inv
    ac = col[:, None] * inv
    cos_l = jnp.concatenate([jnp.cos(ar), jnp.cos(ar), jnp.cos(ac), jnp.cos(ac)], axis=1)
    sin_l = jnp.concatenate([-jnp.sin(ar), jnp.sin(ar), -jnp.sin(ac), jnp.sin(ac)], axis=1)
    cos_t = jnp.concatenate([jnp.ones((n_ctx, M_HEAD_DIM), F32), cos_l], axis=0)
    sin_t = jnp.concatenate([jnp.zeros((n_ctx, M_HEAD_DIM), F32), sin_l], axis=0)
    return cos_t, sin_t


def kernel(x, c, ctx, c_ctx, ada_w, ada_b, norm1_w, w_in, conv_w, conv_b, gate_b, mnorm_w, rpb, w_out,
           norm2_w, router_w, router_b, exp_w1, exp_w3, exp_w2, final_norm_w):
    B, L, _ = x.shape
    n_ctx = ctx.shape[1]
    depth = ada_w.shape[0]
    assert n_ctx == TOK_TILE and L % TOK_TILE == 0 and L % GRID_W == 0
    S = n_ctx + L
    rows = L // GRID_W
    assert rows >= NA_WROWS + 4 and rows % NA_QROWS == 0 and (S // 4) % HALO == 0

    cvec = jnp.concatenate([c, c_ctx[None], jnp.zeros((16 - B - 1, D_MODEL), F32)], axis=0)
    mods = _adaln(cvec, ada_w, ada_b)
    cos_t, sin_t = _rope_tables(L, n_ctx)
    router_wt = router_w.T
    typ, st, bias = _natten_tables(rpb.reshape((depth * N_HEADS,) + rpb.shape[2:]), rows, n_ctx)

    stream = (x, ctx)
    out = None
    sorted_buf = jnp.zeros((_sorted_rows(B * S), REC_W), F32)
    for layer in range(depth):
        last = layer == depth - 1
        m6 = mods[layer].reshape(16, 6, D_MODEL)
        mod = jnp.stack([jnp.broadcast_to(m6[B], (B, 6, D_MODEL)), m6[:B]], axis=1)
        mq, mo, nq, mk, mv, nk, nv, g = _inproj(stream, mod, norm1_w[layer], w_in[layer], B, S)
        q, k = _prep(mq, mk, conv_w[layer], conv_b[layer], cos_t, sin_t, B, S)
        hf, hb = _mlstm(q, k, mv, g, gate_b[layer], B, S, n_ctx // CHUNK)
        n_out = _natten(nq, nk, nv, typ, st, bias, layer, B, S, n_ctx)
        xa, rec, cls, rank, counts = _outproj(stream, hf, hb, mo, n_out, mod, mnorm_w[layer],
                                              w_out[layer].astype(BF16), norm2_w[layer], router_wt, router_b, B, S)
        out, sorted_buf = _moe(rec, cls, rank, counts, xa, mod, exp_w1[layer], exp_w3[layer], exp_w2[layer],
                               final_norm_w, sorted_buf, B, S, n_ctx, last)
        stream = (out,)
    return out
```

```python
import functools

import numpy as np
import jax
import jax.numpy as jnp
from jax import lax
from jax.experimental import pallas as pl
from jax.experimental.pallas import tpu as pltpu

D_MODEL = 1024
M_WIDTH = 512
M_HEADS = 4
M_HEAD_DIM = 128
N_WIDTH = 512
N_HEADS = 8
N_HEAD_DIM = 64
G_COLS = 16
GRID_W = 64
WIN_H = 8
WIN_W = 16
CHUNK = 128
SCAN_CHUNKS = 2
ROPE_AXIS_DIM = 64
ROPE_BASE = 10000.0
N_EXPERTS = 16
N_GROUPS = 4
EXPERTS_PER_GROUP = 4
MOE_D_FF = 512
NORM_EPS = 1e-6

LANES = 128
TOK_TILE = 256
HALO = 16
NA_QROWS = 2
NA_WROWS = 10
NA_WKEYS = NA_WROWS * GRID_W
NEG = -1e30
G_PAD = LANES
N_PROJ = 7 * M_WIDTH + G_PAD
REC_W = D_MODEL + LANES
MOE_TILE = 1024
PAIRS_PER_GROUP = EXPERTS_PER_GROUP * (EXPERTS_PER_GROUP - 1) // 2
N_CLASSES = N_GROUPS * PAIRS_PER_GROUP
CLS_ROWS = 32
PAIR_LO = (0, 0, 0, 1, 1, 2)
PAIR_HI = (1, 2, 3, 2, 3, 3)
VMEM_LIMIT = 56 * 1024 * 1024

F32 = jnp.float32
BF16 = jnp.bfloat16


def _cparams(sem):
    return pltpu.CompilerParams(dimension_semantics=sem, vmem_limit_bytes=VMEM_LIMIT)


def _split_bf16(a):
    hi = a.astype(BF16)
    lo = (a - hi.astype(F32)).astype(BF16)
    return hi, lo


def _dot3(a, b):
    ah, al = _split_bf16(a)
    bh, bl = _split_bf16(b)
    d = functools.partial(jnp.dot, preferred_element_type=F32)
    return d(ah, bh) + (d(al, bh) + d(ah, bl))


def _dot_nt(a, b):
    return lax.dot_general(a, b, (((1,), (1,)), ((), ())), preferred_element_type=F32)


def _rms(x, w):
    ms = jnp.mean(x * x, axis=-1, keepdims=True)
    return x * lax.rsqrt(ms + NORM_EPS) * w


def _adaln_kernel(c_ref, w_ref, b_ref, o_ref):
    c = c_ref[...]
    s = c * jax.nn.sigmoid(c)
    o_ref[0] = _dot3(s, w_ref[0]) + b_ref[0]


def _adaln(cvec, ada_w, ada_b):
    depth = ada_w.shape[0]
    n = ada_w.shape[2]
    tn = D_MODEL
    return pl.pallas_call(
        _adaln_kernel,
        grid=(depth, n // tn),
        in_specs=[pl.BlockSpec((16, D_MODEL), lambda l, j: (0, 0)),
                  pl.BlockSpec((1, D_MODEL, tn), lambda l, j: (l, 0, j)),
                  pl.BlockSpec((1, 1, tn), lambda l, j: (l, 0, j))],
        out_specs=pl.BlockSpec((1, 16, tn), lambda l, j: (l, 0, j)),
        out_shape=jax.ShapeDtypeStruct((depth, 16, n), F32),
        compiler_params=_cparams(("arbitrary", "arbitrary")),
        name="adaln",
    )(cvec, ada_w, ada_b.reshape(depth, 1, n))


def _inproj_kernel(*refs, split_input):
    if split_input:
        x_ref, ctx_ref, mod_ref, nw_ref, win_ref = refs[:5]
        outs = refs[5:13]
        xt = jnp.where(pl.program_id(1) == 0, ctx_ref[0], x_ref[0])
    else:
        x_ref, mod_ref, nw_ref, win_ref = refs[:4]
        outs = refs[4:12]
        xt = x_ref[0]
    w_ref = refs[-1]

    @pl.when(jnp.logical_and(pl.program_id(0) == 0, pl.program_id(1) == 0))
    def _():
        g0 = 5 * M_WIDTH
        n0 = g0 + G_COLS
        for k in range(5):
            cs = slice(k * M_WIDTH, (k + 1) * M_WIDTH)
            w_ref[:, cs] = win_ref[0, :, cs].astype(BF16)
        for k in range(2):
            w_ref[:, g0 + k * M_WIDTH:g0 + (k + 1) * M_WIDTH] = (
                win_ref[0, :, n0 + k * M_WIDTH:n0 + (k + 1) * M_WIDTH].astype(BF16))
        w_ref[:, 7 * M_WIDTH:] = win_ref[0, :, g0:g0 + G_PAD].astype(BF16)

    h = _rms(xt, nw_ref[...]) * (1.0 + mod_ref[0, 0, 1:2, :]) + mod_ref[0, 0, 0:1, :]
    hb = h.astype(BF16)
    for k in range(7):
        outs[k][0] = jnp.dot(hb, w_ref[:, k * M_WIDTH:(k + 1) * M_WIDTH],
                             preferred_element_type=F32).astype(BF16)
    outs[7][0] = jnp.dot(hb, w_ref[:, 7 * M_WIDTH:], preferred_element_type=F32)


def _stream_specs(split_input):
    tm = TOK_TILE
    if split_input:
        return [pl.BlockSpec((1, tm, D_MODEL), lambda b, j: (b, jnp.maximum(j - 1, 0), 0)),
                pl.BlockSpec((1, tm, D_MODEL), lambda b, j: (b, 0, 0))]
    return [pl.BlockSpec((1, tm, D_MODEL), lambda b, j: (b, j, 0))]


def _mod_spec():
    return pl.BlockSpec((1, 1, 6, D_MODEL), lambda b, j: (b, jnp.minimum(j, 1), 0, 0))


def _inproj(stream, mod, norm_w, w_in, layer, B, S):
    split_input = len(stream) == 2
    tm = TOK_TILE
    tok = lambda w: pl.BlockSpec((1, tm, w), lambda b, j: (b, j, 0))
    out_shape = [jax.ShapeDtypeStruct((B, S, M_WIDTH), BF16)] * 7 + [jax.ShapeDtypeStruct((B, S, G_PAD), F32)]
    return pl.pallas_call(
        functools.partial(_inproj_kernel, split_input=split_input),
        grid=(B, S // tm),
        in_specs=_stream_specs(split_input) + [
            _mod_spec(),
            pl.BlockSpec((1, D_MODEL), lambda b, j: (0, 0)),
            pl.BlockSpec((1,) + w_in.shape[1:], lambda b, j: (layer, 0, 0), pipeline_mode=pl.Buffered(1))],
        out_specs=[tok(M_WIDTH)] * 7 + [tok(G_PAD)],
        out_shape=out_shape,
        scratch_shapes=[pltpu.VMEM((D_MODEL, N_PROJ), BF16)],
        compiler_params=_cparams(("arbitrary", "arbitrary")),
        name="inproj",
    )(*stream, mod, norm_w.reshape(1, D_MODEL), w_in)


def _prep_kernel(q_ref, qp_ref, qn_ref, k_ref, kp_ref, kn_ref, cw_ref, cb_ref, cos_ref, sin_ref,
                 qo_ref, ko_ref, *, n_tiles):
    j = pl.program_id(1)
    tp = TOK_TILE
    has_prev = jnp.logical_and(j != 0, j != 1).astype(F32)
    has_next = jnp.logical_and(j != 0, j != n_tiles - 1).astype(F32)
    rows = lax.broadcasted_iota(jnp.int32, (tp, M_WIDTH), 0)
    lanes = lax.broadcasted_iota(jnp.int32, (tp, M_WIDTH), 1)
    low_half = (lanes % (ROPE_AXIS_DIM)) < (ROPE_AXIS_DIM // 2)
    cosv = jnp.concatenate([cos_ref[...]] * M_HEADS, axis=1)
    sinv = jnp.concatenate([sin_ref[...]] * M_HEADS, axis=1)

    def branch(x_ref, p_ref, n_ref, col0):
        x = x_ref[0].astype(F32)
        prev_row = p_ref[0, HALO - 1:HALO, :].astype(F32) * has_prev
        next_row = n_ref[0, 0:1, :].astype(F32) * has_next
        xm = jnp.where(rows == 0, prev_row, pltpu.roll(x, 1, 0))
        xp = jnp.where(rows == tp - 1, next_row, pltpu.roll(x, tp - 1, 0))
        w = cw_ref[:, col0:col0 + M_WIDTH]
        y = cb_ref[:, col0:col0 + M_WIDTH] + xm * w[0:1] + x * w[1:2] + xp * w[2:3]
        y = y * jax.nn.sigmoid(y)
        half = ROPE_AXIS_DIM // 2
        partner = jnp.where(low_half, pltpu.roll(y, M_WIDTH - half, 1), pltpu.roll(y, half, 1))
        return y * cosv + partner * sinv

    qo_ref[0] = branch(q_ref, qp_ref, qn_ref, 0).astype(BF16)
    ko_ref[0] = (branch(k_ref, kp_ref, kn_ref, M_WIDTH) * (M_HEAD_DIM ** -0.5)).T.astype(BF16)


def _prep(mq, mk, conv_w, conv_b, cos_t, sin_t, B, S):
    tp = TOK_TILE
    n_tiles = S // tp
    per = tp // HALO
    n_halo = S // HALO
    main = pl.BlockSpec((1, tp, M_WIDTH), lambda b, j: (b, j, 0))
    prev = pl.BlockSpec((1, HALO, M_WIDTH), lambda b, j: (b, jnp.maximum(j * per - 1, 0), 0))
    nxt = pl.BlockSpec((1, HALO, M_WIDTH), lambda b, j: (b, jnp.minimum((j + 1) * per, n_halo - 1), 0))
    return pl.pallas_call(
        functools.partial(_prep_kernel, n_tiles=n_tiles),
        grid=(B, n_tiles),
        in_specs=[main, prev, nxt, main, prev, nxt,
                  pl.BlockSpec((3, 2 * M_WIDTH), lambda b, j: (0, 0)),
                  pl.BlockSpec((1, 2 * M_WIDTH), lambda b, j: (0, 0)),
                  pl.BlockSpec((tp, M_HEAD_DIM), lambda b, j: (j, 0)),
                  pl.BlockSpec((tp, M_HEAD_DIM), lambda b, j: (j, 0))],
        out_specs=[main, pl.BlockSpec((1, M_WIDTH, tp), lambda b, j: (b, 0, j))],
        out_shape=[jax.ShapeDtypeStruct((B, S, M_WIDTH), BF16), jax.ShapeDtypeStruct((B, M_WIDTH, S), BF16)],
        compiler_params=_cparams(("arbitrary", "arbitrary")),
        name="mlstm_prep",
    )(mq, mq, mq, mk, mk, mk, conv_w, conv_b.reshape(1, -1), cos_t, sin_t)


def _scan_rows(x, reverse, op, fill):
    n = x.shape[0]
    rows = lax.broadcasted_iota(jnp.int32, x.shape, 0)
    sh = 1
    while sh < n:
        if reverse:
            x = op(x, jnp.where(rows < n - sh, pltpu.roll(x, n - sh, 0), fill))
        else:
            x = op(x, jnp.where(rows >= sh, pltpu.roll(x, sh, 0), fill))
        sh *= 2
    return x


def _log_sigmoid(x):
    return jnp.minimum(x, 0.0) - jnp.log(1.0 + jnp.exp(-jnp.abs(x)))


def _mlstm_kernel(qf_ref, kf_ref, vf_ref, gf_ref, qb_ref, kb_ref, vb_ref, gb_ref, gbias_ref,
                  hf_ref, hb_ref, cn_ref, m_ref):
    c = pl.program_id(1)

    @pl.when(c == 0)
    def _():
        cn_ref[...] = jnp.zeros_like(cn_ref)
        m_ref[...] = jnp.zeros_like(m_ref)

    t = CHUNK
    r_i = lax.broadcasted_iota(jnp.int32, (t, t), 0)
    c_i = lax.broadcasted_iota(jnp.int32, (t, t), 1)
    ones_blk = jnp.ones((t, LANES), BF16)

    dirs = ((qf_ref, kf_ref, vf_ref, gf_ref, hf_ref), (qb_ref, kb_ref, vb_ref, gb_ref, hb_ref))
    for step, d in ((s, d) for s in range(SCAN_CHUNKS) for d in range(2)):
        q_ref, k_ref, v_ref, g_ref, h_ref = dirs[d]
        reverse = d == 1
        ts = pl.ds((SCAN_CHUNKS - 1 - step if reverse else step) * t, t)
        mask = (c_i >= r_i) if reverse else (c_i <= r_i)
        end_row = 0 if reverse else t - 1
        g = g_ref[0, ts, :] + gbias_ref[...]
        bc = _scan_rows(_log_sigmoid(g), reverse, jnp.add, 0.0)
        b_al = pltpu.roll(bc, LANES - M_HEADS, 1)
        a = g - b_al
        m_old = m_ref[d][0:1, :]
        mx = jnp.maximum(m_old, _scan_rows(a, reverse, jnp.maximum, NEG))
        m_end = mx[end_row:end_row + 1, :]
        c_t = -mx
        e_den = jnp.exp(-(b_al + mx))
        w_old = jnp.exp(m_old - m_end)
        a_t = a.T
        w_tok_t = jnp.exp(a - m_end).T
        m_ref[d] = jnp.broadcast_to(b_al[end_row:end_row + 1, :] + m_end, (8, LANES))
        for h in range(M_HEADS):
            icol = 2 * M_HEADS * d + h
            hs = slice(h * M_HEAD_DIM, (h + 1) * M_HEAD_DIM)
            q = q_ref[0, ts, hs]
            k_t = k_ref[0, hs, ts]
            v_ext = jnp.concatenate([v_ref[0, ts, hs], ones_blk], axis=1)
            cn_old = cn_ref[d, h]

            c_b = jnp.broadcast_to(c_t[:, icol:icol + 1], (t, t))
            expo = jnp.concatenate([jnp.where(mask, c_b + a_t[icol:icol + 1, :], NEG),
                                    c_b + m_old[:, icol:icol + 1]], axis=1)
            qk = jnp.dot(q, k_t, preferred_element_type=F32)
            s_ext = (jnp.concatenate([qk, q.astype(F32)], axis=1) * jnp.exp(expo)).astype(BF16)
            rhs = jnp.concatenate([v_ext, cn_old.astype(BF16)], axis=0)
            ext = jnp.dot(s_ext, rhs, preferred_element_type=F32)
            den = jnp.maximum(jnp.abs(ext[:, M_HEAD_DIM:]),
                              jnp.broadcast_to(e_den[:, icol:icol + 1], (t, LANES)))
            h_ref[0, ts, hs] = (ext[:, :M_HEAD_DIM] / den).astype(BF16)

            kw_t = (k_t.astype(F32) * w_tok_t[icol:icol + 1, :]).astype(BF16)
            cn_ref[d, h] = (w_old[:, icol:icol + 1] * cn_old
                            + jnp.dot(kw_t, v_ext, preferred_element_type=F32))


def _mlstm(q, k, v, g, gate_b, B, S, n_ctx_chunks):
    rows = SCAN_CHUNKS * CHUNK
    assert n_ctx_chunks % SCAN_CHUNKS == 0 and S % rows == 0
    nc = S // rows
    ncx = n_ctx_chunks // SCAN_CHUNKS

    def bwd_chunk(c):
        return jnp.where(c < ncx, ncx - 1 - c, nc - 1 + ncx - c)

    fw = lambda w: pl.BlockSpec((1, rows, w), lambda b, c: (b, c, 0))
    bw = lambda w: pl.BlockSpec((1, rows, w), lambda b, c: (b, bwd_chunk(c), 0))
    fw_t = pl.BlockSpec((1, M_WIDTH, rows), lambda b, c: (b, 0, c))
    bw_t = pl.BlockSpec((1, M_WIDTH, rows), lambda b, c: (b, 0, bwd_chunk(c)))
    gbias = jnp.zeros((1, G_PAD), F32).at[0, :G_COLS].set(gate_b)
    return pl.pallas_call(
        _mlstm_kernel,
        grid=(B, nc),
        in_specs=[fw(M_WIDTH), fw_t, fw(M_WIDTH), fw(G_PAD),
                  bw(M_WIDTH), bw_t, bw(M_WIDTH), bw(G_PAD),
                  pl.BlockSpec((1, G_PAD), lambda b, c: (0, 0))],
        out_specs=[fw(M_WIDTH), bw(M_WIDTH)],
        out_shape=[jax.ShapeDtypeStruct((B, S, M_WIDTH), BF16)] * 2,
        scratch_shapes=[pltpu.VMEM((2, M_HEADS, M_HEAD_DIM, 2 * LANES), F32),
                        pltpu.VMEM((2, 8, LANES), F32)],
        compiler_params=_cparams(("arbitrary", "arbitrary")),
        name="mlstm_scan",
    )(q, k, v, g, q, k, v, g, gbias)


def _natten_kernel(typ_ref, st_ref, q_ref, k_ref, v_ref, bias_ref, o_ref, *, n_ctx):
    sub = NA_QROWS * GRID_W
    lane = lax.broadcasted_iota(jnp.int32, (sub, LANES), 1)
    kctx = k_ref[0, 0:n_ctx, :]
    vctx = v_ref[0, 0:n_ctx, :]

    def sub_block(sb, carry):
        typ = typ_ref[sb]
        st = pl.multiple_of(st_ref[sb], GRID_W)
        rows = pl.ds(pl.multiple_of(sb * sub, sub), sub)
        q = q_ref[0, rows, :] * (N_HEAD_DIM ** -0.5)
        kwin = k_ref[0, pl.ds(st, NA_WKEYS), :]
        vwin = v_ref[0, pl.ds(st, NA_WKEYS), :]
        first = lane < N_HEAD_DIM
        zero = jnp.zeros_like(q)
        q2 = jnp.concatenate([jnp.where(first, q, zero), jnp.where(first, zero, q)], axis=0)
        s_win = _dot_nt(q2, kwin) + bias_ref[typ].reshape(2 * sub, NA_WKEYS)
        s_ctx = _dot_nt(q2, kctx)
        m = jnp.maximum(jnp.max(s_win, axis=-1, keepdims=True), jnp.max(s_ctx, axis=-1, keepdims=True))
        p_win = jnp.exp(s_win - m)
        p_ctx = jnp.exp(s_ctx - m)
        l = jnp.sum(p_win, axis=-1, keepdims=True) + jnp.sum(p_ctx, axis=-1, keepdims=True)
        o = (jnp.dot(p_win.astype(BF16), vwin, preferred_element_type=F32)
             + jnp.dot(p_ctx.astype(BF16), vctx, preferred_element_type=F32)) / l
        o_ref[0, rows, :] = jnp.where(first, o[:sub], o[sub:]).astype(BF16)
        return carry

    lax.fori_loop(0, q_ref.shape[1] // sub, sub_block, 0, unroll=2)


def _natten_tables(rpb, rows, n_ctx):
    n_heads = rpb.shape[0]
    n_sb = rows // NA_QROWS
    ws_of = lambda sb: int(np.clip(NA_QROWS * sb - WIN_H // 2, 0, rows - NA_WROWS))
    rs_of = lambda r: int(np.clip(r - WIN_H // 2, 0, rows - WIN_H))
    shape_of = lambda sb: tuple((NA_QROWS * sb + qr - ws_of(sb), rs_of(NA_QROWS * sb + qr) - ws_of(sb))
                                for qr in range(NA_QROWS))
    shapes = sorted(set(shape_of(sb) for sb in range(n_sb)))
    reps = [next(sb for sb in range(n_sb) if shape_of(sb) == sh) for sh in shapes]
    n_dr = 2 * WIN_H - 1
    n_dc = 2 * WIN_W - 1
    lead = GRID_W - WIN_W
    vec = jnp.pad(rpb, ((0, 0), (0, 0), (lead, 2 * GRID_W - lead - n_dc)), constant_values=NEG)
    skew = jnp.broadcast_to(vec[:, :, None, :], (n_heads, n_dr, GRID_W, 2 * GRID_W))
    skew = skew.reshape(n_heads, n_dr, -1)[:, :, :GRID_W * (2 * GRID_W - 1)]
    toep = skew.reshape(n_heads, n_dr, GRID_W, 2 * GRID_W - 1)[..., GRID_W - 1:]
    cq = np.arange(GRID_W)[:, None]
    ck = np.arange(GRID_W)[None, :]
    cs = np.clip(cq - WIN_W // 2, 0, GRID_W - WIN_W)
    toep = jnp.where((ck >= cs) & (ck < cs + WIN_W), toep, NEG)
    masked_tile = jnp.full((n_heads, GRID_W, GRID_W), NEG, F32)
    types = []
    for sb in reps:
        ws = ws_of(sb)
        q_rows = []
        for qr in range(NA_QROWS):
            r = NA_QROWS * sb + qr
            rs = rs_of(r)
            tiles = []
            for kw in range(NA_WROWS):
                kr = ws + kw
                tiles.append(toep[:, kr - r + WIN_H - 1] if rs <= kr < rs + WIN_H else masked_tile)
            q_rows.append(jnp.concatenate(tiles, axis=-1))
        types.append(jnp.concatenate(q_rows, axis=-2))
    types.append(jnp.full_like(types[0], NEG))
    bias = jnp.stack(types, axis=0)
    n_ctx_sb = n_ctx // (NA_QROWS * GRID_W)
    typ, st = [len(shapes)] * n_ctx_sb, [n_ctx] * n_ctx_sb
    for sb in range(n_sb):
        typ.append(shapes.index(shape_of(sb)))
        st.append(n_ctx + ws_of(sb) * GRID_W)
    return jnp.asarray(typ, jnp.int32), jnp.asarray(st, jnp.int32), bias


def _natten(nq, nk, nv, typ, st, bias, layer, B, S, n_ctx):
    n_pairs = N_HEADS // 2
    grid_spec = pltpu.PrefetchScalarGridSpec(
        num_scalar_prefetch=2,
        grid=(n_pairs, B),
        in_specs=[pl.BlockSpec((1, S, LANES), lambda p, b, *_: (b, 0, p)),
                  pl.BlockSpec((1, S, LANES), lambda p, b, *_: (b, 0, p)),
                  pl.BlockSpec((1, S, LANES), lambda p, b, *_: (b, 0, p)),
                  pl.BlockSpec((bias.shape[0], 2, NA_QROWS * GRID_W, NA_WKEYS),
                               lambda p, b, *_: (0, layer * n_pairs + p, 0, 0))],
        out_specs=pl.BlockSpec((1, S, LANES), lambda p, b, *_: (b, 0, p)),
    )
    return pl.pallas_call(
        functools.partial(_natten_kernel, n_ctx=n_ctx),
        grid_spec=grid_spec,
        out_shape=jax.ShapeDtypeStruct((B, S, N_WIDTH), BF16),
        compiler_params=_cparams(("arbitrary", "arbitrary")),
        name="natten",
    )(typ, st, nq, nk, nv, bias)


def _route(logits_t, rb):
    e_i = lax.broadcasted_iota(jnp.int32, logits_t.shape, 0)
    z = logits_t - jnp.max(logits_t, axis=0, keepdims=True)
    ez = jnp.exp(z)
    scores = ez / jnp.sum(ez, axis=0, keepdims=True)
    sel = scores + rb
    best = None
    best_score = None
    for gi in range(N_GROUPS):
        r = [sel[gi * EXPERTS_PER_GROUP + u:gi * EXPERTS_PER_GROUP + u + 1, :] for u in range(EXPERTS_PER_GROUP)]
        gs = None
        for u in range(EXPERTS_PER_GROUP):
            for w in range(u + 1, EXPERTS_PER_GROUP):
                pair = r[u] + r[w]
                gs = pair if gs is None else jnp.maximum(gs, pair)
        if best is None:
            best, best_score = jnp.zeros(gs.shape, jnp.int32), gs
        else:
            better = gs > best_score
            best = jnp.where(better, gi, best)
            best_score = jnp.where(better, gs, best_score)
    masked = jnp.where((e_i // EXPERTS_PER_GROUP) == best, sel, -jnp.inf)
    v1 = jnp.max(masked, axis=0, keepdims=True)
    i1 = jnp.min(jnp.where(masked == v1, e_i, N_EXPERTS), axis=0, keepdims=True)
    masked2 = jnp.where(e_i == i1, -jnp.inf, masked)
    v2 = jnp.max(masked2, axis=0, keepdims=True)
    i2 = jnp.min(jnp.where(masked2 == v2, e_i, N_EXPERTS), axis=0, keepdims=True)
    w1 = jnp.sum(jnp.where(e_i == i1, scores, 0.0), axis=0, keepdims=True)
    w2 = jnp.sum(jnp.where(e_i == i2, scores, 0.0), axis=0, keepdims=True)
    tot = w1 + w2
    gate = jnp.where(e_i == i1, w1 / tot, 0.0) + jnp.where(e_i == i2, w2 / tot, 0.0)
    lo = jnp.minimum(i1, i2) - best * EXPERTS_PER_GROUP
    hi = jnp.maximum(i1, i2) - best * EXPERTS_PER_GROUP
    pair = ((lo * (2 * EXPERTS_PER_GROUP - 1 - lo)) >> 1) + (hi - lo - 1)
    return gate, best * PAIRS_PER_GROUP + pair


def _outproj_kernel(*refs, split_input):
    n_stream = 2 if split_input else 1
    stream = refs[:n_stream]
    (hf_ref, hb_ref, mo_ref, no_ref, mod_ref, mnw_ref, wo_in_ref, n2w_ref, rw_ref, rb_ref,
     xo_ref, rec_ref, grp_ref, rank_ref, cnt_ref, base_ref, wo_ref) = refs[n_stream:]

    @pl.when(jnp.logical_and(pl.program_id(0) == 0, pl.program_id(1) == 0))
    def _():
        base_ref[...] = jnp.zeros_like(base_ref)
        wo_ref[...] = wo_in_ref[0].astype(BF16)

    rh, rl = _split_bf16(rw_ref[...])
    n_rows = xo_ref.shape[1]
    half = n_rows // 2
    logits = []
    for r0 in (0, half):
        rs = slice(r0, r0 + half)
        if split_input:
            xt = jnp.where(pl.program_id(1) == 0, stream[1][0, rs], stream[0][0, rs])
        else:
            xt = stream[0][0, rs]
        hsum = hf_ref[0, rs].astype(F32) + hb_ref[0, rs].astype(F32)
        parts = []
        for h in range(M_HEADS):
            hs = slice(h * M_HEAD_DIM, (h + 1) * M_HEAD_DIM)
            parts.append(_rms(hsum[:, hs], mnw_ref[:, hs]))
        m_out = jnp.concatenate(parts, axis=1) * jax.nn.sigmoid(mo_ref[0, rs].astype(F32))
        y = (jnp.dot(m_out.astype(BF16), wo_ref[0:M_WIDTH, :], preferred_element_type=F32)
             + jnp.dot(no_ref[0, rs], wo_ref[M_WIDTH:, :], preferred_element_type=F32))
        x_new = xt + mod_ref[0, 0, 2:3, :] * y
        xo_ref[0, rs] = x_new
        hx = _rms(x_new, n2w_ref[...]) * (1.0 + mod_ref[0, 0, 4:5, :]) + mod_ref[0, 0, 3:4, :]
        rec_ref[0, rs, :D_MODEL] = hx
        hh, hl = _split_bf16(hx)
        logits.append(_dot_nt(rh, hh) + (_dot_nt(rh, hl) + _dot_nt(rl, hh)))
    logits_t = jnp.concatenate(logits, axis=1)
    gate_t, cls = _route(logits_t, rb_ref[...])
    pad = jnp.zeros((LANES - N_EXPERTS, gate_t.shape[1]), F32)
    rec_ref[0, :, D_MODEL:] = jnp.concatenate([gate_t, pad], axis=0).T
    tm = cls.shape[1]
    onehot = lax.broadcasted_iota(jnp.int32, (CLS_ROWS, tm), 0) == cls
    upper = (lax.broadcasted_iota(jnp.int32, (tm, tm), 0) <= lax.broadcasted_iota(jnp.int32, (tm, tm), 1))
    csum = jnp.dot(onehot.astype(BF16), upper.astype(BF16), preferred_element_type=F32)
    base = base_ref[:, 0:1]
    rank = jnp.sum(jnp.where(onehot, csum - 1.0 + base, 0.0), axis=0, keepdims=True)
    grp_ref[0, 0] = cls
    rank_ref[0, 0] = rank.astype(jnp.int32)
    total = base_ref[...] + csum[:, tm - 1:tm]
    base_ref[...] = total
    cnt_ref[...] = total


def _outproj(stream, hf, hb, mo, n_out, mod, mnorm_w, w_out, layer, norm2_w, router_wt, router_b, B, S):
    split_input = len(stream) == 2
    tm = TOK_TILE
    nt = S // tm
    tok = lambda w: pl.BlockSpec((1, tm, w), lambda b, j: (b, j, 0))
    idx = pl.BlockSpec((1, 1, 1, tm), lambda b, j: (b, j, 0, 0))
    const = lambda shape: pl.BlockSpec(shape, lambda b, j: (0,) * len(shape))
    return pl.pallas_call(
        functools.partial(_outproj_kernel, split_input=split_input),
        grid=(B, nt),
        in_specs=_stream_specs(split_input) + [
            tok(M_WIDTH), tok(M_WIDTH), tok(M_WIDTH), tok(N_WIDTH), _mod_spec(),
            const((1, M_WIDTH)),
            pl.BlockSpec((1, D_MODEL, D_MODEL), lambda b, j: (layer, 0, 0), pipeline_mode=pl.Buffered(1)),
            const((1, D_MODEL)), const((N_EXPERTS, D_MODEL)), const((N_EXPERTS, 1))],
        out_specs=[tok(D_MODEL), tok(REC_W), idx, idx, const((CLS_ROWS, LANES))],
        out_shape=[jax.ShapeDtypeStruct((B, S, D_MODEL), F32),
                   jax.ShapeDtypeStruct((B, S, REC_W), F32),
                   jax.ShapeDtypeStruct((B, nt, 1, tm), jnp.int32),
                   jax.ShapeDtypeStruct((B, nt, 1, tm), jnp.int32),
                   jax.ShapeDtypeStruct((CLS_ROWS, LANES), F32)],
        scratch_shapes=[pltpu.VMEM((CLS_ROWS, LANES), F32), pltpu.VMEM((D_MODEL, D_MODEL), BF16)],
        compiler_params=_cparams(("arbitrary", "arbitrary")),
        name="outproj_router",
    )(*stream, hf, hb, mo, n_out, mod, mnorm_w.reshape(1, -1), w_out, norm2_w.reshape(1, -1),
      router_wt, router_b.reshape(-1, 1))


def _row_copy_wait(src, dst, sem, n_rows):
    pltpu.make_async_copy(src.at[pl.ds(0, n_rows)], dst.at[pl.ds(0, n_rows)], sem).wait()


def _dispatch_kernel(pos_ref, rec_ref, init_ref, xs_ref, sem):
    del init_ref
    n = rec_ref.shape[0]

    def issue(r, carry):
        pltpu.make_async_copy(rec_ref.at[pl.ds(r, 1)], xs_ref.at[pl.ds(pos_ref[r], 1)], sem).start()
        return carry

    lax.fori_loop(0, n, issue, 0, unroll=8)
    _row_copy_wait(rec_ref, xs_ref, sem, n)


def _dispatch(pos, rec, init):
    T, w = rec.shape
    n_sorted = init.shape[0]
    tp = _flat_tile(T)
    return pl.pallas_call(
        _dispatch_kernel,
        grid=(T // tp,),
        in_specs=[pl.BlockSpec((tp,), lambda i: (i,), memory_space=pltpu.SMEM),
                  pl.BlockSpec((tp, w), lambda i: (i, 0)),
                  pl.BlockSpec(memory_space=pl.ANY)],
        out_specs=pl.BlockSpec(memory_space=pl.ANY),
        out_shape=jax.ShapeDtypeStruct((n_sorted, w), F32),
        scratch_shapes=[pltpu.SemaphoreType.DMA(())],
        input_output_aliases={2: 0},
        compiler_params=_cparams(("arbitrary",)),
        name="moe_dispatch",
    )(pos, rec, init)


def _experts_kernel(te_ref, nt_ref, xs_ref, w1_ref, w3_ref, w2_ref, ys_ref, acc_ref):
    i = pl.program_id(0)
    k = pl.program_id(1)

    @pl.when(k == 0)
    def _():
        acc_ref[...] = jnp.zeros_like(acc_ref)

    @pl.when(i < nt_ref[0])
    def _():
        e = te_ref[2 * i + k]
        h = xs_ref[:, :D_MODEL].astype(BF16)
        gate = xs_ref[:, D_MODEL:]
        lane = lax.broadcasted_iota(jnp.int32, gate.shape, 1)
        g_e = jnp.sum(jnp.where(lane == e, gate, 0.0), axis=-1, keepdims=True)
        a = jnp.dot(h, w1_ref[0].astype(BF16), preferred_element_type=F32)
        b3 = jnp.dot(h, w3_ref[0].astype(BF16), preferred_element_type=F32)
        act = (a * jax.nn.sigmoid(a) * b3 * g_e).astype(BF16)
        acc_ref[...] += jnp.dot(act, w2_ref[0].astype(BF16), preferred_element_type=F32)

    @pl.when(k == 1)
    def _():
        ys_ref[...] = acc_ref[...]


def _experts(tile_experts, n_tiles, xs, w1, w3, w2, layer):
    n_sorted, w = xs.shape
    tm = MOE_TILE
    expert = lambda i, k, te, nt: (layer * N_EXPERTS + te[2 * jnp.minimum(i, nt[0] - 1) + k], 0, 0)
    rows = lambda i, k, te, nt: (jnp.minimum(i, nt[0] - 1), 0)
    grid_spec = pltpu.PrefetchScalarGridSpec(
        num_scalar_prefetch=2,
        grid=(n_sorted // tm, 2),
        in_specs=[pl.BlockSpec((tm, w), rows),
                  pl.BlockSpec((1, D_MODEL, MOE_D_FF), expert),
                  pl.BlockSpec((1, D_MODEL, MOE_D_FF), expert),
                  pl.BlockSpec((1, MOE_D_FF, D_MODEL), expert)],
        out_specs=pl.BlockSpec((tm, D_MODEL), lambda i, k, *_: (i, 0)),
        scratch_shapes=[pltpu.VMEM((tm, D_MODEL), F32)],
    )
    return pl.pallas_call(
        _experts_kernel,
        grid_spec=grid_spec,
        out_shape=jax.ShapeDtypeStruct((n_sorted, D_MODEL), F32),
        compiler_params=_cparams(("arbitrary", "arbitrary")),
        name="moe_experts",
    )(tile_experts, n_tiles, xs, w1, w3, w2)


def _combine_kernel(pos_ref, ys_ref, x_ref, mod_ref, fw_ref, o_ref, y_buf, sem, *, S, n_ctx, final_norm):
    n = y_buf.shape[0]

    def issue(r, carry):
        pltpu.make_async_copy(ys_ref.at[pl.ds(pos_ref[r], 1)], y_buf.at[pl.ds(r, 1)], sem).start()
        return carry

    lax.fori_loop(0, n, issue, 0, unroll=8)
    t0 = pl.program_id(0) * n
    b0 = t0 // S
    b1 = jnp.minimum(b0 + 1, mod_ref.shape[0] - 1)
    tok = t0 + lax.broadcasted_iota(jnp.int32, (n, 1), 0)
    second = tok >= (b0 + 1) * S
    is_ctx = (tok - jnp.where(second, b0 + 1, b0) * S) < n_ctx
    gate_mod = jnp.where(second,
                         jnp.where(is_ctx, mod_ref[b1, 0, 5:6, :], mod_ref[b1, 1, 5:6, :]),
                         jnp.where(is_ctx, mod_ref[b0, 0, 5:6, :], mod_ref[b0, 1, 5:6, :]))
    _row_copy_wait(ys_ref, y_buf, sem, n)
    x_new = x_ref[...] + gate_mod * y_buf[...]
    if final_norm:
        x_new = _rms(x_new, fw_ref[...])
    o_ref[...] = x_new


def _combine(pos, ys, xa, mod, final_w, S, n_ctx, final_norm):
    T = xa.shape[0]
    tu = _flat_tile(T)
    assert tu <= S
    return pl.pallas_call(
        functools.partial(_combine_kernel, S=S, n_ctx=n_ctx, final_norm=final_norm),
        grid=(T // tu,),
        in_specs=[pl.BlockSpec((tu,), lambda i: (i,), memory_space=pltpu.SMEM),
                  pl.BlockSpec(memory_space=pl.ANY),
                  pl.BlockSpec((tu, D_MODEL), lambda i: (i, 0)),
                  pl.BlockSpec(mod.shape, lambda i: (0, 0, 0, 0)),
                  pl.BlockSpec((1, D_MODEL), lambda i: (0, 0))],
        out_specs=pl.BlockSpec((tu, D_MODEL), lambda i: (i, 0)),
        out_shape=jax.ShapeDtypeStruct((T, D_MODEL), F32),
        scratch_shapes=[pltpu.VMEM((tu, D_MODEL), F32), pltpu.SemaphoreType.DMA(())],
        compiler_params=_cparams(("arbitrary",)),
        name="moe_combine",
    )(pos, ys, xa, mod, final_w.reshape(1, -1))


def _combine_latent_kernel(pos_ref, ys_ref, xa_ref, mod_ref, fw_ref, o_ref, y_buf, x_buf, sem, x_sem,
                           *, S, L, n_ctx):
    n = y_buf.shape[0]
    per_batch = L // n
    b = pl.program_id(0) // per_batch
    row0 = b * S + n_ctx + (pl.program_id(0) % per_batch) * n
    x_copy = pltpu.make_async_copy(xa_ref.at[pl.ds(row0, n)], x_buf, x_sem)
    x_copy.start()

    def issue(r, carry):
        pltpu.make_async_copy(ys_ref.at[pl.ds(pos_ref[r], 1)], y_buf.at[pl.ds(r, 1)], sem).start()
        return carry

    lax.fori_loop(0, n, issue, 0, unroll=8)
    x_copy.wait()
    _row_copy_wait(ys_ref, y_buf, sem, n)
    o_ref[...] = _rms(x_buf[...] + mod_ref[b, 1, 5:6, :] * y_buf[...], fw_ref[...])


def _combine_latent(pos, ys, xa, mod, final_w, B, S, L, n_ctx):
    tu = _flat_tile(L)
    pos_lat = pos.reshape(B, S)[:, n_ctx:].reshape(B * L)
    return pl.pallas_call(
        functools.partial(_combine_latent_kernel, S=S, L=L, n_ctx=n_ctx),
        grid=(B * L // tu,),
        in_specs=[pl.BlockSpec((tu,), lambda i: (i,), memory_space=pltpu.SMEM),
                  pl.BlockSpec(memory_space=pl.ANY),
                  pl.BlockSpec(memory_space=pl.ANY),
                  pl.BlockSpec(mod.shape, lambda i: (0, 0, 0, 0)),
                  pl.BlockSpec((1, D_MODEL), lambda i: (0, 0))],
        out_specs=pl.BlockSpec((tu, D_MODEL), lambda i: (i, 0)),
        out_shape=jax.ShapeDtypeStruct((B * L, D_MODEL), F32),
        scratch_shapes=[pltpu.VMEM((tu, D_MODEL), F32), pltpu.VMEM((tu, D_MODEL), F32),
                        pltpu.SemaphoreType.DMA(()), pltpu.SemaphoreType.DMA(())],
        compiler_params=_cparams(("arbitrary",)),
        name="moe_combine_latent",
    )(pos_lat, ys, xa, mod, final_w.reshape(1, -1))


def _flat_tile(T):
    return MOE_TILE if T % MOE_TILE == 0 else TOK_TILE


def _sorted_rows(T):
    return (-(-T // MOE_TILE) + N_CLASSES) * MOE_TILE


def _moe(rec, cls, rank, counts, xa, mod, w1, w3, w2, layer, final_w, sorted_init, B, S, n_ctx, last):
    T = B * S
    tm = MOE_TILE
    n_tiles_max = _sorted_rows(T) // tm
    cnt = counts[:N_CLASSES, 0].astype(jnp.int32)
    tiles = (cnt + tm - 1) // tm
    tile_end = jnp.cumsum(tiles)
    row0 = (tile_end - tiles) * tm
    cls = cls.reshape(T)
    pos = rank.reshape(T)
    for ci in range(N_CLASSES):
        pos = pos + jnp.where(cls == ci, row0[ci], 0)
    n_tiles = tile_end[N_CLASSES - 1:].astype(jnp.int32)
    tile_cls = jnp.sum(jnp.arange(n_tiles_max)[:, None] >= tile_end[None, :], axis=1)
    tile_cls = jnp.minimum(tile_cls, N_CLASSES - 1)
    grp0 = (tile_cls // PAIRS_PER_GROUP) * EXPERTS_PER_GROUP
    pair = tile_cls % PAIRS_PER_GROUP
    lo = sum(jnp.where(pair == p, PAIR_LO[p], 0) for p in range(PAIRS_PER_GROUP))
    hi = sum(jnp.where(pair == p, PAIR_HI[p], 0) for p in range(PAIRS_PER_GROUP))
    tile_experts = jnp.stack([grp0 + lo, grp0 + hi], axis=1).reshape(-1).astype(jnp.int32)
    xs = _dispatch(pos, rec.reshape(T, REC_W), sorted_init)
    ys = _experts(tile_experts, n_tiles, xs, w1, w3, w2, layer)
    if last:
        L = S - n_ctx
        out = _combine_latent(pos, ys, xa.reshape(T, D_MODEL), mod, final_w, B, S, L, n_ctx)
        return out.reshape(B, L, D_MODEL), xs
    out = _combine(pos, ys, xa.reshape(T, D_MODEL), mod, final_w, S, n_ctx, False)
    return out.reshape(B, S, D_MODEL), xs


def _rope_tables(L, n_ctx):
    t = jnp.arange(L)
    row = (t // GRID_W).astype(F32)
    col = (t % GRID_W).astype(F32)
    inv = ROPE_BASE ** (-jnp.arange(0, ROPE_AXIS_DIM, 2, dtype=F32) / ROPE_AXIS_DIM)
    ar = row[:, None] * inv
    ac = col[:, None] * inv
    cos_l = jnp.concatenate([jnp.cos(ar), jnp.cos(ar), jnp.cos(ac), jnp.cos(ac)], axis=1)
    sin_l = jnp.concatenate([-jnp.sin(ar), jnp.sin(ar), -jnp.sin(ac), jnp.sin(ac)], axis=1)
    cos_t = jnp.concatenate([jnp.ones((n_ctx, M_HEAD_DIM), F32), cos_l], axis=0)
    sin_t = jnp.concatenate([jnp.zeros((n_ctx, M_HEAD_DIM), F32), sin_l], axis=0)
    return cos_t, sin_t


def kernel(x, c, ctx, c_ctx, ada_w, ada_b, norm1_w, w_in, conv_w, conv_b, gate_b, mnorm_w, rpb, w_out,
           norm2_w, router_w, router_b, exp_w1, exp_w3, exp_w2, final_norm_w):
    B, L, _ = x.shape
    n_ctx = ctx.shape[1]
    depth = ada_w.shape[0]
    assert n_ctx == TOK_TILE and L % TOK_TILE == 0 and L % GRID_W == 0
    S = n_ctx + L
    rows = L // GRID_W
    assert rows >= NA_WROWS + 4 and rows % NA_QROWS == 0 and (S // 4) % HALO == 0

    cvec = jnp.concatenate([c, c_ctx[None], jnp.zeros((16 - B - 1, D_MODEL), F32)], axis=0)
    mods = _adaln(cvec, ada_w, ada_b)
    cos_t, sin_t = _rope_tables(L, n_ctx)
    router_wt = router_w.T
    typ, st, bias = _natten_tables(rpb.reshape((depth * N_HEADS,) + rpb.shape[2:]), rows, n_ctx)

    stream = (x, ctx)
    out = None
    sorted_buf = jnp.zeros((_sorted_rows(B * S), REC_W), F32)
    w1_all = exp_w1.reshape((depth * N_EXPERTS,) + exp_w1.shape[2:])
    w3_all = exp_w3.reshape((depth * N_EXPERTS,) + exp_w3.shape[2:])
    w2_all = exp_w2.reshape((depth * N_EXPERTS,) + exp_w2.shape[2:])
    for layer in range(depth):
        last = layer == depth - 1
        m6 = mods[layer].reshape(16, 6, D_MODEL)
        mod = jnp.stack([jnp.broadcast_to(m6[B], (B, 6, D_MODEL)), m6[:B]], axis=1)
        mq, mo, nq, mk, mv, nk, nv, g = _inproj(stream, mod, norm1_w[layer], w_in, layer, B, S)
        q, k = _prep(mq, mk, conv_w[layer], conv_b[layer], cos_t, sin_t, B, S)
        hf, hb = _mlstm(q, k, mv, g, gate_b[layer], B, S, n_ctx // CHUNK)
        n_out = _natten(nq, nk, nv, typ, st, bias, layer, B, S, n_ctx)
        xa, rec, cls, rank, counts = _outproj(stream, hf, hb, mo, n_out, mod, mnorm_w[layer], w_out, layer,
                                              norm2_w[layer], router_wt, router_b, B, S)
        out, sorted_buf = _moe(rec, cls, rank, counts, xa, mod, w1_all, w3_all, w2_all, layer,
                               final_norm_w, sorted_buf, B, S, n_ctx, last)
        stream = (out,)
    return out
```

```python
import functools

import numpy as np
import jax
import jax.numpy as jnp
from jax import lax
from jax.experimental import pallas as pl
from jax.experimental.pallas import tpu as pltpu

D_MODEL = 1024
M_WIDTH = 512
M_HEADS = 4
M_HEAD_DIM = 128
N_WIDTH = 512
N_HEADS = 8
N_HEAD_DIM = 64
G_COLS = 16
GRID_W = 64
WIN_H = 8
WIN_W = 16
CHUNK = 128
SCAN_CHUNKS = 2
ROPE_AXIS_DIM = 64
ROPE_BASE = 10000.0
N_EXPERTS = 16
N_GROUPS = 4
EXPERTS_PER_GROUP = 4
MOE_D_FF = 512
NORM_EPS = 1e-6

LANES = 128
TOK_TILE = 256
HALO = 16
NA_QROWS = 2
NA_WROWS = 10
NA_WKEYS = NA_WROWS * GRID_W
NEG = -1e30
G_PAD = LANES
N_PROJ = 7 * M_WIDTH + G_PAD
REC_W = D_MODEL + LANES
MOE_TILE = 1024
PAIRS_PER_GROUP = EXPERTS_PER_GROUP * (EXPERTS_PER_GROUP - 1) // 2
N_CLASSES = N_GROUPS * PAIRS_PER_GROUP
CLS_ROWS = 32
PAIR_LO = (0, 0, 0, 1, 1, 2)
PAIR_HI = (1, 2, 3, 2, 3, 3)
VMEM_LIMIT = 56 * 1024 * 1024

F32 = jnp.float32
BF16 = jnp.bfloat16


def _cparams(sem):
    return pltpu.CompilerParams(dimension_semantics=sem, vmem_limit_bytes=VMEM_LIMIT)


def _split_bf16(a):
    hi = a.astype(BF16)
    lo = (a - hi.astype(F32)).astype(BF16)
    return hi, lo


def _dot3(a, b):
    ah, al = _split_bf16(a)
    bh, bl = _split_bf16(b)
    d = functools.partial(jnp.dot, preferred_element_type=F32)
    return d(ah, bh) + (d(al, bh) + d(ah, bl))


def _dot_nt(a, b):
    return lax.dot_general(a, b, (((1,), (1,)), ((), ())), preferred_element_type=F32)


def _rms(x, w):
    ms = jnp.mean(x * x, axis=-1, keepdims=True)
    return x * lax.rsqrt(ms + NORM_EPS) * w


def _adaln_kernel(c_ref, w_ref, b_ref, o_ref):
    c = c_ref[...]
    s = c * jax.nn.sigmoid(c)
    o_ref[0] = _dot3(s, w_ref[0]) + b_ref[0]


def _adaln(cvec, ada_w, ada_b):
    depth = ada_w.shape[0]
    n = ada_w.shape[2]
    tn = D_MODEL
    return pl.pallas_call(
        _adaln_kernel,
        grid=(depth, n // tn),
        in_specs=[pl.BlockSpec((16, D_MODEL), lambda l, j: (0, 0)),
                  pl.BlockSpec((1, D_MODEL, tn), lambda l, j: (l, 0, j)),
                  pl.BlockSpec((1, 1, tn), lambda l, j: (l, 0, j))],
        out_specs=pl.BlockSpec((1, 16, tn), lambda l, j: (l, 0, j)),
        out_shape=jax.ShapeDtypeStruct((depth, 16, n), F32),
        compiler_params=_cparams(("arbitrary", "arbitrary")),
        name="adaln",
    )(cvec, ada_w, ada_b.reshape(depth, 1, n))


def _inproj_kernel(*refs, split_input):
    if split_input:
        x_ref, ctx_ref, mod_ref, nw_ref, win_ref = refs[:5]
        outs = refs[5:13]
        xt = jnp.where(pl.program_id(1) == 0, ctx_ref[0], x_ref[0])
    else:
        x_ref, mod_ref, nw_ref, win_ref = refs[:4]
        outs = refs[4:12]
        xt = x_ref[0]
    w_ref = refs[-1]

    @pl.when(jnp.logical_and(pl.program_id(0) == 0, pl.program_id(1) == 0))
    def _():
        g0 = 5 * M_WIDTH
        n0 = g0 + G_COLS
        for k in range(5):
            cs = slice(k * M_WIDTH, (k + 1) * M_WIDTH)
            w_ref[:, cs] = win_ref[0, :, cs].astype(BF16)
        for k in range(2):
            w_ref[:, g0 + k * M_WIDTH:g0 + (k + 1) * M_WIDTH] = (
                win_ref[0, :, n0 + k * M_WIDTH:n0 + (k + 1) * M_WIDTH].astype(BF16))
        w_ref[:, 7 * M_WIDTH:] = win_ref[0, :, g0:g0 + G_PAD].astype(BF16)

    h = _rms(xt, nw_ref[...]) * (1.0 + mod_ref[0, 0, 1:2, :]) + mod_ref[0, 0, 0:1, :]
    hb = h.astype(BF16)
    for k in range(7):
        outs[k][0] = jnp.dot(hb, w_ref[:, k * M_WIDTH:(k + 1) * M_WIDTH],
                             preferred_element_type=F32).astype(BF16)
    outs[7][0] = jnp.dot(hb, w_ref[:, 7 * M_WIDTH:], preferred_element_type=F32)


def _stream_specs(split_input):
    tm = TOK_TILE
    if split_input:
        return [pl.BlockSpec((1, tm, D_MODEL), lambda b, j: (b, jnp.maximum(j - 1, 0), 0)),
                pl.BlockSpec((1, tm, D_MODEL), lambda b, j: (b, 0, 0))]
    return [pl.BlockSpec((1, tm, D_MODEL), lambda b, j: (b, j, 0))]


def _mod_spec():
    return pl.BlockSpec((1, 1, 6, D_MODEL), lambda b, j: (b, jnp.minimum(j, 1), 0, 0))


def _inproj(stream, mod, norm_w, w_in, layer, B, S):
    split_input = len(stream) == 2
    tm = TOK_TILE
    tok = lambda w: pl.BlockSpec((1, tm, w), lambda b, j: (b, j, 0))
    out_shape = [jax.ShapeDtypeStruct((B, S, M_WIDTH), BF16)] * 7 + [jax.ShapeDtypeStruct((B, S, G_PAD), F32)]
    return pl.pallas_call(
        functools.partial(_inproj_kernel, split_input=split_input),
        grid=(B, S // tm),
        in_specs=_stream_specs(split_input) + [
            _mod_spec(),
            pl.BlockSpec((1, D_MODEL), lambda b, j: (0, 0)),
            pl.BlockSpec((1,) + w_in.shape[1:], lambda b, j: (layer, 0, 0), pipeline_mode=pl.Buffered(1))],
        out_specs=[tok(M_WIDTH)] * 7 + [tok(G_PAD)],
        out_shape=out_shape,
        scratch_shapes=[pltpu.VMEM((D_MODEL, N_PROJ), BF16)],
        compiler_params=_cparams(("arbitrary", "arbitrary")),
        name="inproj",
    )(*stream, mod, norm_w.reshape(1, D_MODEL), w_in)


def _prep_kernel(q_ref, qp_ref, qn_ref, k_ref, kp_ref, kn_ref, cw_ref, cb_ref, cos_ref, sin_ref,
                 qo_ref, ko_ref, *, n_tiles):
    j = pl.program_id(1)
    tp = TOK_TILE
    has_prev = jnp.logical_and(j != 0, j != 1).astype(F32)
    has_next = jnp.logical_and(j != 0, j != n_tiles - 1).astype(F32)
    rows = lax.broadcasted_iota(jnp.int32, (tp, M_WIDTH), 0)
    lanes = lax.broadcasted_iota(jnp.int32, (tp, M_WIDTH), 1)
    low_half = (lanes % (ROPE_AXIS_DIM)) < (ROPE_AXIS_DIM // 2)
    cosv = jnp.concatenate([cos_ref[...]] * M_HEADS, axis=1)
    sinv = jnp.concatenate([sin_ref[...]] * M_HEADS, axis=1)

    def branch(x_ref, p_ref, n_ref, col0):
        x = x_ref[0].astype(F32)
        prev_row = p_ref[0, HALO - 1:HALO, :].astype(F32) * has_prev
        next_row = n_ref[0, 0:1, :].astype(F32) * has_next
        xm = jnp.where(rows == 0, prev_row, pltpu.roll(x, 1, 0))
        xp = jnp.where(rows == tp - 1, next_row, pltpu.roll(x, tp - 1, 0))
        w = cw_ref[:, col0:col0 + M_WIDTH]
        y = cb_ref[:, col0:col0 + M_WIDTH] + xm * w[0:1] + x * w[1:2] + xp * w[2:3]
        y = y * jax.nn.sigmoid(y)
        half = ROPE_AXIS_DIM // 2
        partner = jnp.where(low_half, pltpu.roll(y, M_WIDTH - half, 1), pltpu.roll(y, half, 1))
        return y * cosv + partner * sinv

    qo_ref[0] = branch(q_ref, qp_ref, qn_ref, 0).astype(BF16)
    ko_ref[0] = (branch(k_ref, kp_ref, kn_ref, M_WIDTH) * (M_HEAD_DIM ** -0.5)).T.astype(BF16)


def _prep(mq, mk, conv_w, conv_b, cos_t, sin_t, B, S):
    tp = TOK_TILE
    n_tiles = S // tp
    per = tp // HALO
    n_halo = S // HALO
    main = pl.BlockSpec((1, tp, M_WIDTH), lambda b, j: (b, j, 0))
    prev = pl.BlockSpec((1, HALO, M_WIDTH), lambda b, j: (b, jnp.maximum(j * per - 1, 0), 0))
    nxt = pl.BlockSpec((1, HALO, M_WIDTH), lambda b, j: (b, jnp.minimum((j + 1) * per, n_halo - 1), 0))
    return pl.pallas_call(
        functools.partial(_prep_kernel, n_tiles=n_tiles),
        grid=(B, n_tiles),
        in_specs=[main, prev, nxt, main, prev, nxt,
                  pl.BlockSpec((3, 2 * M_WIDTH), lambda b, j: (0, 0)),
                  pl.BlockSpec((1, 2 * M_WIDTH), lambda b, j: (0, 0)),
                  pl.BlockSpec((tp, M_HEAD_DIM), lambda b, j: (j, 0)),
                  pl.BlockSpec((tp, M_HEAD_DIM), lambda b, j: (j, 0))],
        out_specs=[main, pl.BlockSpec((1, M_WIDTH, tp), lambda b, j: (b, 0, j))],
        out_shape=[jax.ShapeDtypeStruct((B, S, M_WIDTH), BF16), jax.ShapeDtypeStruct((B, M_WIDTH, S), BF16)],
        compiler_params=_cparams(("arbitrary", "arbitrary")),
        name="mlstm_prep",
    )(mq, mq, mq, mk, mk, mk, conv_w, conv_b.reshape(1, -1), cos_t, sin_t)


def _scan_rows(x, reverse, op, fill):
    n = x.shape[0]
    rows = lax.broadcasted_iota(jnp.int32, x.shape, 0)
    sh = 1
    while sh < n:
        if reverse:
            x = op(x, jnp.where(rows < n - sh, pltpu.roll(x, n - sh, 0), fill))
        else:
            x = op(x, jnp.where(rows >= sh, pltpu.roll(x, sh, 0), fill))
        sh *= 2
    return x


def _log_sigmoid(x):
    return jnp.minimum(x, 0.0) - jnp.log(1.0 + jnp.exp(-jnp.abs(x)))


def _mlstm_kernel(qf_ref, kf_ref, vf_ref, gf_ref, qb_ref, kb_ref, vb_ref, gb_ref, gbias_ref,
                  hf_ref, hb_ref, cn_ref, m_ref):
    c = pl.program_id(1)

    @pl.when(c == 0)
    def _():
        cn_ref[...] = jnp.zeros_like(cn_ref)
        m_ref[...] = jnp.zeros_like(m_ref)

    t = CHUNK
    r_i = lax.broadcasted_iota(jnp.int32, (t, t), 0)
    c_i = lax.broadcasted_iota(jnp.int32, (t, t), 1)
    ones_blk = jnp.ones((t, LANES), BF16)

    dirs = ((qf_ref, kf_ref, vf_ref, gf_ref, hf_ref), (qb_ref, kb_ref, vb_ref, gb_ref, hb_ref))
    for step, d in ((s, d) for s in range(SCAN_CHUNKS) for d in range(2)):
        q_ref, k_ref, v_ref, g_ref, h_ref = dirs[d]
        reverse = d == 1
        ts = pl.ds((SCAN_CHUNKS - 1 - step if reverse else step) * t, t)
        mask = (c_i >= r_i) if reverse else (c_i <= r_i)
        end_row = 0 if reverse else t - 1
        g = g_ref[0, ts, :] + gbias_ref[...]
        bc = _scan_rows(_log_sigmoid(g), reverse, jnp.add, 0.0)
        b_al = pltpu.roll(bc, LANES - M_HEADS, 1)
        a = g - b_al
        m_old = m_ref[d][0:1, :]
        mx = jnp.maximum(m_old, _scan_rows(a, reverse, jnp.maximum, NEG))
        m_end = mx[end_row:end_row + 1, :]
        c_t = -mx
        e_den = jnp.exp(-(b_al + mx))
        w_old = jnp.exp(m_old - m_end)
        a_t = a.T
        w_tok_t = jnp.exp(a - m_end).T
        m_ref[d] = jnp.broadcast_to(b_al[end_row:end_row + 1, :] + m_end, (8, LANES))
        for h in range(M_HEADS):
            icol = 2 * M_HEADS * d + h
            hs = slice(h * M_HEAD_DIM, (h + 1) * M_HEAD_DIM)
            q = q_ref[0, ts, hs]
            k_t = k_ref[0, hs, ts]
            v_ext = jnp.concatenate([v_ref[0, ts, hs], ones_blk], axis=1)
            cn_old = cn_ref[d, h]

            c_b = jnp.broadcast_to(c_t[:, icol:icol + 1], (t, t))
            expo = jnp.concatenate([jnp.where(mask, c_b + a_t[icol:icol + 1, :], NEG),
                                    c_b + m_old[:, icol:icol + 1]], axis=1)
            qk = jnp.dot(q, k_t, preferred_element_type=F32)
            s_ext = (jnp.concatenate([qk, q.astype(F32)], axis=1) * jnp.exp(expo)).astype(BF16)
            rhs = jnp.concatenate([v_ext, cn_old.astype(BF16)], axis=0)
            ext = jnp.dot(s_ext, rhs, preferred_element_type=F32)
            den = jnp.maximum(jnp.abs(ext[:, M_HEAD_DIM:]),
                              jnp.broadcast_to(e_den[:, icol:icol + 1], (t, LANES)))
            h_ref[0, ts, hs] = (ext[:, :M_HEAD_DIM] / den).astype(BF16)

            kw_t = (k_t.astype(F32) * w_tok_t[icol:icol + 1, :]).astype(BF16)
            cn_ref[d, h] = (w_old[:, icol:icol + 1] * cn_old
                            + jnp.dot(kw_t, v_ext, preferred_element_type=F32))


def _mlstm(q, k, v, g, gate_b, B, S, n_ctx_chunks):
    rows = SCAN_CHUNKS * CHUNK
    assert n_ctx_chunks % SCAN_CHUNKS == 0 and S % rows == 0
    nc = S // rows
    ncx = n_ctx_chunks // SCAN_CHUNKS

    def bwd_chunk(c):
        return jnp.where(c < ncx, ncx - 1 - c, nc - 1 + ncx - c)

    fw = lambda w: pl.BlockSpec((1, rows, w), lambda b, c: (b, c, 0))
    bw = lambda w: pl.BlockSpec((1, rows, w), lambda b, c: (b, bwd_chunk(c), 0))
    fw_t = pl.BlockSpec((1, M_WIDTH, rows), lambda b, c: (b, 0, c))
    bw_t = pl.BlockSpec((1, M_WIDTH, rows), lambda b, c: (b, 0, bwd_chunk(c)))
    gbias = jnp.zeros((1, G_PAD), F32).at[0, :G_COLS].set(gate_b)
    return pl.pallas_call(
        _mlstm_kernel,
        grid=(B, nc),
        in_specs=[fw(M_WIDTH), fw_t, fw(M_WIDTH), fw(G_PAD),
                  bw(M_WIDTH), bw_t, bw(M_WIDTH), bw(G_PAD),
                  pl.BlockSpec((1, G_PAD), lambda b, c: (0, 0))],
        out_specs=[fw(M_WIDTH), bw(M_WIDTH)],
        out_shape=[jax.ShapeDtypeStruct((B, S, M_WIDTH), BF16)] * 2,
        scratch_shapes=[pltpu.VMEM((2, M_HEADS, M_HEAD_DIM, 2 * LANES), F32),
                        pltpu.VMEM((2, 8, LANES), F32)],
        compiler_params=_cparams(("arbitrary", "arbitrary")),
        name="mlstm_scan",
    )(q, k, v, g, q, k, v, g, gbias)


def _natten_kernel(typ_ref, st_ref, q_ref, k_ref, v_ref, bias_ref, o_ref, *, n_ctx):
    sub = NA_QROWS * GRID_W
    lane = lax.broadcasted_iota(jnp.int32, (sub, LANES), 1)
    kctx = k_ref[0, 0:n_ctx, :]
    vctx = v_ref[0, 0:n_ctx, :]

    def sub_block(sb, carry):
        typ = typ_ref[sb]
        st = pl.multiple_of(st_ref[sb], GRID_W)
        rows = pl.ds(pl.multiple_of(sb * sub, sub), sub)
        q = q_ref[0, rows, :] * (N_HEAD_DIM ** -0.5)
        kwin = k_ref[0, pl.ds(st, NA_WKEYS), :]
        vwin = v_ref[0, pl.ds(st, NA_WKEYS), :]
        first = lane < N_HEAD_DIM
        zero = jnp.zeros_like(q)
        q2 = jnp.concatenate([jnp.where(first, q, zero), jnp.where(first, zero, q)], axis=0)
        s_win = _dot_nt(q2, kwin) + bias_ref[typ].reshape(2 * sub, NA_WKEYS)
        s_ctx = _dot_nt(q2, kctx)
        m = jnp.maximum(jnp.max(s_win, axis=-1, keepdims=True), jnp.max(s_ctx, axis=-1, keepdims=True))
        p_win = jnp.exp(s_win - m)
        p_ctx = jnp.exp(s_ctx - m)
        l = jnp.sum(p_win, axis=-1, keepdims=True) + jnp.sum(p_ctx, axis=-1, keepdims=True)
        o = (jnp.dot(p_win.astype(BF16), vwin, preferred_element_type=F32)
             + jnp.dot(p_ctx.astype(BF16), vctx, preferred_element_type=F32)) / l
        o_ref[0, rows, :] = jnp.where(first, o[:sub], o[sub:]).astype(BF16)
        return carry

    lax.fori_loop(0, q_ref.shape[1] // sub, sub_block, 0, unroll=2)


def _natten_tables(rpb, rows, n_ctx):
    n_heads = rpb.shape[0]
    n_sb = rows // NA_QROWS
    ws_of = lambda sb: int(np.clip(NA_QROWS * sb - WIN_H // 2, 0, rows - NA_WROWS))
    rs_of = lambda r: int(np.clip(r - WIN_H // 2, 0, rows - WIN_H))
    shape_of = lambda sb: tuple((NA_QROWS * sb + qr - ws_of(sb), rs_of(NA_QROWS * sb + qr) - ws_of(sb))
                                for qr in range(NA_QROWS))
    shapes = sorted(set(shape_of(sb) for sb in range(n_sb)))
    reps = [next(sb for sb in range(n_sb) if shape_of(sb) == sh) for sh in shapes]
    n_dr = 2 * WIN_H - 1
    n_dc = 2 * WIN_W - 1
    lead = GRID_W - WIN_W
    vec = jnp.pad(rpb, ((0, 0), (0, 0), (lead, 2 * GRID_W - lead - n_dc)), constant_values=NEG)
    skew = jnp.broadcast_to(vec[:, :, None, :], (n_heads, n_dr, GRID_W, 2 * GRID_W))
    skew = skew.reshape(n_heads, n_dr, -1)[:, :, :GRID_W * (2 * GRID_W - 1)]
    toep = skew.reshape(n_heads, n_dr, GRID_W, 2 * GRID_W - 1)[..., GRID_W - 1:]
    cq = np.arange(GRID_W)[:, None]
    ck = np.arange(GRID_W)[None, :]
    cs = np.clip(cq - WIN_W // 2, 0, GRID_W - WIN_W)
    toep = jnp.where((ck >= cs) & (ck < cs + WIN_W), toep, NEG)
    masked_tile = jnp.full((n_heads, GRID_W, GRID_W), NEG, F32)
    types = []
    for sb in reps:
        ws = ws_of(sb)
        q_rows = []
        for qr in range(NA_QROWS):
            r = NA_QROWS * sb + qr
            rs = rs_of(r)
            tiles = []
            for kw in range(NA_WROWS):
                kr = ws + kw
                tiles.append(toep[:, kr - r + WIN_H - 1] if rs <= kr < rs + WIN_H else masked_tile)
            q_rows.append(jnp.concatenate(tiles, axis=-1))
        types.append(jnp.concatenate(q_rows, axis=-2))
    types.append(jnp.full_like(types[0], NEG))
    bias = jnp.stack(types, axis=0)
    n_ctx_sb = n_ctx // (NA_QROWS * GRID_W)
    typ, st = [len(shapes)] * n_ctx_sb, [n_ctx] * n_ctx_sb
    for sb in range(n_sb):
        typ.append(shapes.index(shape_of(sb)))
        st.append(n_ctx + ws_of(sb) * GRID_W)
    return jnp.asarray(typ, jnp.int32), jnp.asarray(st, jnp.int32), bias


def _natten(nq, nk, nv, typ, st, bias, layer, B, S, n_ctx):
    n_pairs = N_HEADS // 2
    grid_spec = pltpu.PrefetchScalarGridSpec(
        num_scalar_prefetch=2,
        grid=(n_pairs, B),
        in_specs=[pl.BlockSpec((1, S, LANES), lambda p, b, *_: (b, 0, p)),
                  pl.BlockSpec((1, S, LANES), lambda p, b, *_: (b, 0, p)),
                  pl.BlockSpec((1, S, LANES), lambda p, b, *_: (b, 0, p)),
                  pl.BlockSpec((bias.shape[0], 2, NA_QROWS * GRID_W, NA_WKEYS),
                               lambda p, b, *_: (0, layer * n_pairs + p, 0, 0))],
        out_specs=pl.BlockSpec((1, S, LANES), lambda p, b, *_: (b, 0, p)),
    )
    return pl.pallas_call(
        functools.partial(_natten_kernel, n_ctx=n_ctx),
        grid_spec=grid_spec,
        out_shape=jax.ShapeDtypeStruct((B, S, N_WIDTH), BF16),
        compiler_params=_cparams(("arbitrary", "arbitrary")),
        name="natten",
    )(typ, st, nq, nk, nv, bias)


def _route(logits_t, rb):
    e_i = lax.broadcasted_iota(jnp.int32, logits_t.shape, 0)
    z = logits_t - jnp.max(logits_t, axis=0, keepdims=True)
    ez = jnp.exp(z)
    scores = ez / jnp.sum(ez, axis=0, keepdims=True)
    sel = scores + rb
    best = None
    best_score = None
    for gi in range(N_GROUPS):
        r = [sel[gi * EXPERTS_PER_GROUP + u:gi * EXPERTS_PER_GROUP + u + 1, :] for u in range(EXPERTS_PER_GROUP)]
        gs = None
        for u in range(EXPERTS_PER_GROUP):
            for w in range(u + 1, EXPERTS_PER_GROUP):
                pair = r[u] + r[w]
                gs = pair if gs is None else jnp.maximum(gs, pair)
        if best is None:
            best, best_score = jnp.zeros(gs.shape, jnp.int32), gs
        else:
            better = gs > best_score
            best = jnp.where(better, gi, best)
            best_score = jnp.where(better, gs, best_score)
    masked = jnp.where((e_i // EXPERTS_PER_GROUP) == best, sel, -jnp.inf)
    v1 = jnp.max(masked, axis=0, keepdims=True)
    i1 = jnp.min(jnp.where(masked == v1, e_i, N_EXPERTS), axis=0, keepdims=True)
    masked2 = jnp.where(e_i == i1, -jnp.inf, masked)
    v2 = jnp.max(masked2, axis=0, keepdims=True)
    i2 = jnp.min(jnp.where(masked2 == v2, e_i, N_EXPERTS), axis=0, keepdims=True)
    w1 = jnp.sum(jnp.where(e_i == i1, scores, 0.0), axis=0, keepdims=True)
    w2 = jnp.sum(jnp.where(e_i == i2, scores, 0.0), axis=0, keepdims=True)
    tot = w1 + w2
    gate = jnp.where(e_i == i1, w1 / tot, 0.0) + jnp.where(e_i == i2, w2 / tot, 0.0)
    lo = jnp.minimum(i1, i2) - best * EXPERTS_PER_GROUP
    hi = jnp.maximum(i1, i2) - best * EXPERTS_PER_GROUP
    pair = ((lo * (2 * EXPERTS_PER_GROUP - 1 - lo)) >> 1) + (hi - lo - 1)
    return gate, best * PAIRS_PER_GROUP + pair


def _outproj_kernel(*refs, split_input):
    n_stream = 2 if split_input else 1
    stream = refs[:n_stream]
    (hf_ref, hb_ref, mo_ref, no_ref, mod_ref, mnw_ref, wo_in_ref, n2w_ref, rw_ref, rb_ref,
     xo_ref, rec_ref, grp_ref, rank_ref, cnt_ref, base_ref, wo_ref) = refs[n_stream:]

    @pl.when(jnp.logical_and(pl.program_id(0) == 0, pl.program_id(1) == 0))
    def _():
        base_ref[...] = jnp.zeros_like(base_ref)
        wo_ref[...] = wo_in_ref[0].astype(BF16)

    rh, rl = _split_bf16(rw_ref[...])
    n_rows = xo_ref.shape[1]
    half = n_rows // 2
    logits = []
    for r0 in (0, half):
        rs = slice(r0, r0 + half)
        if split_input:
            xt = jnp.where(pl.program_id(1) == 0, stream[1][0, rs], stream[0][0, rs])
        else:
            xt = stream[0][0, rs]
        hsum = hf_ref[0, rs].astype(F32) + hb_ref[0, rs].astype(F32)
        parts = []
        for h in range(M_HEADS):
            hs = slice(h * M_HEAD_DIM, (h + 1) * M_HEAD_DIM)
            parts.append(_rms(hsum[:, hs], mnw_ref[:, hs]))
        m_out = jnp.concatenate(parts, axis=1) * jax.nn.sigmoid(mo_ref[0, rs].astype(F32))
        y = (jnp.dot(m_out.astype(BF16), wo_ref[0:M_WIDTH, :], preferred_element_type=F32)
             + jnp.dot(no_ref[0, rs], wo_ref[M_WIDTH:, :], preferred_element_type=F32))
        x_new = xt + mod_ref[0, 0, 2:3, :] * y
        xo_ref[0, rs] = x_new
        hx = _rms(x_new, n2w_ref[...]) * (1.0 + mod_ref[0, 0, 4:5, :]) + mod_ref[0, 0, 3:4, :]
        rec_ref[0, rs, :D_MODEL] = hx
        hh, hl = _split_bf16(hx)
        logits.append(_dot_nt(rh, hh) + (_dot_nt(rh, hl) + _dot_nt(rl, hh)))
    logits_t = jnp.concatenate(logits, axis=1)
    gate_t, cls = _route(logits_t, rb_ref[...])
    pad = jnp.zeros((LANES - N_EXPERTS, gate_t.shape[1]), F32)
    rec_ref[0, :, D_MODEL:] = jnp.concatenate([gate_t, pad], axis=0).T
    tm = cls.shape[1]
    onehot = lax.broadcasted_iota(jnp.int32, (CLS_ROWS, tm), 0) == cls
    upper = (lax.broadcasted_iota(jnp.int32, (tm, tm), 0) <= lax.broadcasted_iota(jnp.int32, (tm, tm), 1))
    csum = jnp.dot(onehot.astype(BF16), upper.astype(BF16), preferred_element_type=F32)
    base = base_ref[:, 0:1]
    rank = jnp.sum(jnp.where(onehot, csum - 1.0 + base, 0.0), axis=0, keepdims=True)
    grp_ref[0, 0] = cls
    rank_ref[0, 0] = rank.astype(jnp.int32)
    total = base_ref[...] + csum[:, tm - 1:tm]
    base_ref[...] = total
    cnt_ref[...] = total


def _outproj(stream, hf, hb, mo, n_out, mod, mnorm_w, w_out, layer, norm2_w, router_wt, router_b, B, S):
    split_input = len(stream) == 2
    tm = TOK_TILE
    nt = S // tm
    tok = lambda w: pl.BlockSpec((1, tm, w), lambda b, j: (b, j, 0))
    idx = pl.BlockSpec((1, 1, 1, tm), lambda b, j: (b, j, 0, 0))
    const = lambda shape: pl.BlockSpec(shape, lambda b, j: (0,) * len(shape))
    return pl.pallas_call(
        functools.partial(_outproj_kernel, split_input=split_input),
        grid=(B, nt),
        in_specs=_stream_specs(split_input) + [
            tok(M_WIDTH), tok(M_WIDTH), tok(M_WIDTH), tok(N_WIDTH), _mod_spec(),
            const((1, M_WIDTH)),
            pl.BlockSpec((1, D_MODEL, D_MODEL), lambda b, j: (layer, 0, 0), pipeline_mode=pl.Buffered(1)),
            const((1, D_MODEL)), const((N_EXPERTS, D_MODEL)), const((N_EXPERTS, 1))],
        out_specs=[tok(D_MODEL), tok(REC_W), idx, idx, const((CLS_ROWS, LANES))],
        out_shape=[jax.ShapeDtypeStruct((B, S, D_MODEL), F32),
                   jax.ShapeDtypeStruct((B, S, REC_W), F32),
                   jax.ShapeDtypeStruct((B, nt, 1, tm), jnp.int32),
                   jax.ShapeDtypeStruct((B, nt, 1, tm), jnp.int32),
                   jax.ShapeDtypeStruct((CLS_ROWS, LANES), F32)],
        scratch_shapes=[pltpu.VMEM((CLS_ROWS, LANES), F32), pltpu.VMEM((D_MODEL, D_MODEL), BF16)],
        compiler_params=_cparams(("arbitrary", "arbitrary")),
        name="outproj_router",
    )(*stream, hf, hb, mo, n_out, mod, mnorm_w.reshape(1, -1), w_out, norm2_w.reshape(1, -1),
      router_wt, router_b.reshape(-1, 1))


def _row_copy_wait(src, dst, sem, n_rows):
    pltpu.make_async_copy(src.at[pl.ds(0, n_rows)], dst.at[pl.ds(0, n_rows)], sem).wait()


def _dispatch_kernel(pos_ref, rec_ref, init_ref, xs_ref, sem):
    del init_ref
    n = rec_ref.shape[0]

    def issue(r, carry):
        pltpu.make_async_copy(rec_ref.at[pl.ds(r, 1)], xs_ref.at[pl.ds(pos_ref[r], 1)], sem).start()
        return carry

    lax.fori_loop(0, n, issue, 0, unroll=8)
    _row_copy_wait(rec_ref, xs_ref, sem, n)


def _dispatch(pos, rec, init):
    T, w = rec.shape
    n_sorted = init.shape[0]
    tp = _flat_tile(T)
    return pl.pallas_call(
        _dispatch_kernel,
        grid=(T // tp,),
        in_specs=[pl.BlockSpec((tp,), lambda i: (i,), memory_space=pltpu.SMEM),
                  pl.BlockSpec((tp, w), lambda i: (i, 0)),
                  pl.BlockSpec(memory_space=pl.ANY)],
        out_specs=pl.BlockSpec(memory_space=pl.ANY),
        out_shape=jax.ShapeDtypeStruct((n_sorted, w), F32),
        scratch_shapes=[pltpu.SemaphoreType.DMA(())],
        input_output_aliases={2: 0},
        compiler_params=_cparams(("arbitrary",)),
        name="moe_dispatch",
    )(pos, rec, init)


def _experts_kernel(te_ref, nt_ref, xs_ref, w1_ref, w3_ref, w2_ref, ys_ref):
    i = pl.program_id(0)
    k = pl.program_id(1)
    valid = i < nt_ref[0]

    def contribution():
        e = te_ref[2 * i + k]
        h = xs_ref[:, :D_MODEL].astype(BF16)
        gate = xs_ref[:, D_MODEL:]
        lane = lax.broadcasted_iota(jnp.int32, gate.shape, 1)
        g_e = jnp.sum(jnp.where(lane == e, gate, 0.0), axis=-1, keepdims=True)
        a = jnp.dot(h, w1_ref[0].astype(BF16), preferred_element_type=F32)
        b3 = jnp.dot(h, w3_ref[0].astype(BF16), preferred_element_type=F32)
        act = (a * jax.nn.sigmoid(a) * b3 * g_e).astype(BF16)
        return jnp.dot(act, w2_ref[0].astype(BF16), preferred_element_type=F32)

    @pl.when(jnp.logical_and(valid, k == 0))
    def _():
        ys_ref[...] = contribution()

    @pl.when(jnp.logical_and(valid, k == 1))
    def _():
        ys_ref[...] += contribution()

    @pl.when(jnp.logical_and(jnp.logical_not(valid), k == 0))
    def _():
        ys_ref[...] = jnp.zeros_like(ys_ref)


def _experts(tile_experts, n_tiles, xs, w1, w3, w2, layer):
    n_sorted, w = xs.shape
    tm = MOE_TILE
    expert = lambda i, k, te, nt: (layer * N_EXPERTS + te[2 * jnp.minimum(i, nt[0] - 1) + k], 0, 0)
    rows = lambda i, k, te, nt: (jnp.minimum(i, nt[0] - 1), 0)
    grid_spec = pltpu.PrefetchScalarGridSpec(
        num_scalar_prefetch=2,
        grid=(n_sorted // tm, 2),
        in_specs=[pl.BlockSpec((tm, w), rows),
                  pl.BlockSpec((1, D_MODEL, MOE_D_FF), expert),
                  pl.BlockSpec((1, D_MODEL, MOE_D_FF), expert),
                  pl.BlockSpec((1, MOE_D_FF, D_MODEL), expert)],
        out_specs=pl.BlockSpec((tm, D_MODEL), lambda i, k, *_: (i, 0)),
    )
    return pl.pallas_call(
        _experts_kernel,
        grid_spec=grid_spec,
        out_shape=jax.ShapeDtypeStruct((n_sorted, D_MODEL), F32),
        compiler_params=_cparams(("arbitrary", "arbitrary")),
        name="moe_experts",
    )(tile_experts, n_tiles, xs, w1, w3, w2)


def _combine_kernel(pos_ref, ys_ref, x_ref, mod_ref, fw_ref, o_ref, y_buf, sem, *, S, n_ctx, final_norm):
    n = y_buf.shape[0]

    def issue(r, carry):
        pltpu.make_async_copy(ys_ref.at[pl.ds(pos_ref[r], 1)], y_buf.at[pl.ds(r, 1)], sem).start()
        return carry

    lax.fori_loop(0, n, issue, 0, unroll=8)
    _row_copy_wait(ys_ref, y_buf, sem, n)
    for r0 in range(0, n, TOK_TILE):
        t0 = pl.program_id(0) * n + r0
        b = t0 // S
        kind = jnp.where(t0 - b * S < n_ctx, 0, 1)
        rs = slice(r0, r0 + TOK_TILE)
        x_new = x_ref[rs, :] + mod_ref[b, kind, 5:6, :] * y_buf[rs, :]
        if final_norm:
            x_new = _rms(x_new, fw_ref[...])
        o_ref[rs, :] = x_new


def _combine(pos, ys, xa, mod, final_w, S, n_ctx, final_norm):
    T = xa.shape[0]
    tu = _flat_tile(T)
    assert tu <= S
    return pl.pallas_call(
        functools.partial(_combine_kernel, S=S, n_ctx=n_ctx, final_norm=final_norm),
        grid=(T // tu,),
        in_specs=[pl.BlockSpec((tu,), lambda i: (i,), memory_space=pltpu.SMEM),
                  pl.BlockSpec(memory_space=pl.ANY),
                  pl.BlockSpec((tu, D_MODEL), lambda i: (i, 0)),
                  pl.BlockSpec(mod.shape, lambda i: (0, 0, 0, 0)),
                  pl.BlockSpec((1, D_MODEL), lambda i: (0, 0))],
        out_specs=pl.BlockSpec((tu, D_MODEL), lambda i: (i, 0)),
        out_shape=jax.ShapeDtypeStruct((T, D_MODEL), F32),
        scratch_shapes=[pltpu.VMEM((tu, D_MODEL), F32), pltpu.SemaphoreType.DMA(())],
        compiler_params=_cparams(("arbitrary",)),
        name="moe_combine",
    )(pos, ys, xa, mod, final_w.reshape(1, -1))


def _combine_latent_kernel(pos_ref, ys_ref, xa_ref, mod_ref, fw_ref, o_ref, y_buf, x_buf, sem, x_sem,
                           *, S, L, n_ctx):
    n = y_buf.shape[0]
    per_batch = L // n
    b = pl.program_id(0) // per_batch
    row0 = b * S + n_ctx + (pl.program_id(0) % per_batch) * n
    x_copy = pltpu.make_async_copy(xa_ref.at[pl.ds(row0, n)], x_buf, x_sem)
    x_copy.start()

    def issue(r, carry):
        pltpu.make_async_copy(ys_ref.at[pl.ds(pos_ref[r], 1)], y_buf.at[pl.ds(r, 1)], sem).start()
        return carry

    lax.fori_loop(0, n, issue, 0, unroll=8)
    x_copy.wait()
    _row_copy_wait(ys_ref, y_buf, sem, n)
    o_ref[...] = _rms(x_buf[...] + mod_ref[b, 1, 5:6, :] * y_buf[...], fw_ref[...])


def _combine_latent(pos, ys, xa, mod, final_w, B, S, L, n_ctx):
    tu = _flat_tile(L)
    pos_lat = pos.reshape(B, S)[:, n_ctx:].reshape(B * L)
    return pl.pallas_call(
        functools.partial(_combine_latent_kernel, S=S, L=L, n_ctx=n_ctx),
        grid=(B * L // tu,),
        in_specs=[pl.BlockSpec((tu,), lambda i: (i,), memory_space=pltpu.SMEM),
                  pl.BlockSpec(memory_space=pl.ANY),
                  pl.BlockSpec(memory_space=pl.ANY),
                  pl.BlockSpec(mod.shape, lambda i: (0, 0, 0, 0)),
                  pl.BlockSpec((1, D_MODEL), lambda i: (0, 0))],
        out_specs=pl.BlockSpec((tu, D_MODEL), lambda i: (i, 0)),
        out_shape=jax.ShapeDtypeStruct((B * L, D_MODEL), F32),
        scratch_shapes=[pltpu.VMEM((tu, D_MODEL), F32), pltpu.VMEM((tu, D_MODEL), F32),
                        pltpu.SemaphoreType.DMA(()), pltpu.SemaphoreType.DMA(())],
        compiler_params=_cparams(("arbitrary",)),
        name="moe_combine_latent",
    )(pos_lat, ys, xa, mod, final_w.reshape(1, -1))


def _flat_tile(T):
    return MOE_TILE if T % MOE_TILE == 0 else TOK_TILE


def _sorted_rows(T):
    return (-(-T // MOE_TILE) + N_CLASSES) * MOE_TILE


def _moe(rec, cls, rank, counts, xa, mod, w1, w3, w2, layer, final_w, sorted_init, B, S, n_ctx, last):
    T = B * S
    tm = MOE_TILE
    n_tiles_max = _sorted_rows(T) // tm
    cnt = counts[:N_CLASSES, 0].astype(jnp.int32)
    tiles = (cnt + tm - 1) // tm
    tile_end = jnp.cumsum(tiles)
    row0 = (tile_end - tiles) * tm
    cls = cls.reshape(T)
    pos = rank.reshape(T)
    for ci in range(N_CLASSES):
        pos = pos + jnp.where(cls == ci, row0[ci], 0)
    n_tiles = tile_end[N_CLASSES - 1:].astype(jnp.int32)
    tile_cls = jnp.sum(jnp.arange(n_tiles_max)[:, None] >= tile_end[None, :], axis=1)
    tile_cls = jnp.minimum(tile_cls, N_CLASSES - 1)
    grp0 = (tile_cls // PAIRS_PER_GROUP) * EXPERTS_PER_GROUP
    pair = tile_cls % PAIRS_PER_GROUP
    lo = sum(jnp.where(pair == p, PAIR_LO[p], 0) for p in range(PAIRS_PER_GROUP))
    hi = sum(jnp.where(pair == p, PAIR_HI[p], 0) for p in range(PAIRS_PER_GROUP))
    tile_experts = jnp.stack([grp0 + lo, grp0 + hi], axis=1).reshape(-1).astype(jnp.int32)
    xs = _dispatch(pos, rec.reshape(T, REC_W), sorted_init)
    ys = _experts(tile_experts, n_tiles, xs, w1, w3, w2, layer)
    if last:
        L = S - n_ctx
        out = _combine_latent(pos, ys, xa.reshape(T, D_MODEL), mod, final_w, B, S, L, n_ctx)
        return out.reshape(B, L, D_MODEL), xs
    out = _combine(pos, ys, xa.reshape(T, D_MODEL), mod, final_w, S, n_ctx, False)
    return out.reshape(B, S, D_MODEL), xs


def _rope_tables(L, n_ctx):
    t = jnp.arange(L)
    row = (t // GRID_W).astype(F32)
    col = (t % GRID_W).astype(F32)
    inv = ROPE_BASE ** (-jnp.arange(0, ROPE_AXIS_DIM, 2, dtype=F32) / ROPE_AXIS_DIM)
    ar = row[:, None] * inv
    ac = col[:, None] * inv
    cos_l = jnp.concatenate([jnp.cos(ar), jnp.cos(ar), jnp.cos(ac), jnp.cos(ac)], axis=1)
    sin_l = jnp.concatenate([-jnp.sin(ar), jnp.sin(ar), -jnp.sin(ac), jnp.sin(ac)], axis=1)
    cos_t = jnp.concatenate([jnp.ones((n_ctx, M_HEAD_DIM), F32), cos_l], axis=0)
    sin_t = jnp.concatenate([jnp.zeros((n_ctx, M_HEAD_DIM), F32), sin_l], axis=0)
    return cos_t, sin_t


def kernel(x, c, ctx, c_ctx, ada_w, ada_b, norm1_w, w_in, conv_w, conv_b, gate_b, mnorm_w, rpb, w_out,
           norm2_w, router_w, router_b, exp_w1, exp_w3, exp_w2, final_norm_w):
    B, L, _ = x.shape
    n_ctx = ctx.shape[1]
    depth = ada_w.shape[0]
    assert n_ctx == TOK_TILE and L % TOK_TILE == 0 and L % GRID_W == 0
    S = n_ctx + L
    rows = L // GRID_W
    assert rows >= NA_WROWS + 4 and rows % NA_QROWS == 0 and (S // 4) % HALO == 0

    cvec = jnp.concatenate([c, c_ctx[None], jnp.zeros((16 - B - 1, D_MODEL), F32)], axis=0)
    mods = _adaln(cvec, ada_w, ada_b)
    cos_t, sin_t = _rope_tables(L, n_ctx)
    router_wt = router_w.T
    typ, st, bias = _natten_tables(rpb.reshape((depth * N_HEADS,) + rpb.shape[2:]), rows, n_ctx)

    stream = (x, ctx)
    out = None
    sorted_buf = jnp.zeros((_sorted_rows(B * S), REC_W), F32)
    w1_all = exp_w1.reshape((depth * N_EXPERTS,) + exp_w1.shape[2:])
    w3_all = exp_w3.reshape((depth * N_EXPERTS,) + exp_w3.shape[2:])
    w2_all = exp_w2.reshape((depth * N_EXPERTS,) + exp_w2.shape[2:])
    for layer in range(depth):
        last = layer == depth - 1
        m6 = mods[layer].reshape(16, 6, D_MODEL)
        mod = jnp.stack([jnp.broadcast_to(m6[B], (B, 6, D_MODEL)), m6[:B]], axis=1)
        mq, mo, nq, mk, mv, nk, nv, g = _inproj(stream, mod, norm1_w[layer], w_in, layer, B, S)
        q, k = _prep(mq, mk, conv_w[layer], conv_b[layer], cos_t, sin_t, B, S)
        hf, hb = _mlstm(q, k, mv, g, gate_b[layer], B, S, n_ctx // CHUNK)
        n_out = _natten(nq, nk, nv, typ, st, bias, layer, B, S, n_ctx)
        xa, rec, cls, rank, counts = _outproj(stream, hf, hb, mo, n_out, mod, mnorm_w[layer], w_out, layer,
                                              norm2_w[layer], router_wt, router_b, B, S)
        out, sorted_buf = _moe(rec, cls, rank, counts, xa, mod, w1_all, w3_all, w2_all, layer,
                               final_norm_w, sorted_buf, B, S, n_ctx, last)
        stream = (out,)
    return out
```

```python
import functools

import numpy as np
import jax
import jax.numpy as jnp
from jax import lax
from jax.experimental import pallas as pl
from jax.experimental.pallas import tpu as pltpu

D_MODEL = 1024
M_WIDTH = 512
M_HEADS = 4
M_HEAD_DIM = 128
N_WIDTH = 512
N_HEADS = 8
N_HEAD_DIM = 64
G_COLS = 16
GRID_W = 64
WIN_H = 8
WIN_W = 16
CHUNK = 128
SCAN_CHUNKS = 2
ROPE_AXIS_DIM = 64
ROPE_BASE = 10000.0
N_EXPERTS = 16
N_GROUPS = 4
EXPERTS_PER_GROUP = 4
MOE_D_FF = 512
NORM_EPS = 1e-6

LANES = 128
TOK_TILE = 256
HALO = 16
NA_QROWS = 2
NA_WROWS = 10
NA_WKEYS = NA_WROWS * GRID_W
NEG = -1e30
G_PAD = LANES
N_PROJ = 7 * M_WIDTH + G_PAD
REC_W = D_MODEL + LANES
MOE_TILE = 1024
PAIRS_PER_GROUP = EXPERTS_PER_GROUP * (EXPERTS_PER_GROUP - 1) // 2
N_CLASSES = N_GROUPS * PAIRS_PER_GROUP
CLS_ROWS = 32
PAIR_LO = (0, 0, 0, 1, 1, 2)
PAIR_HI = (1, 2, 3, 2, 3, 3)
VMEM_LIMIT = 56 * 1024 * 1024

F32 = jnp.float32
BF16 = jnp.bfloat16


def _cparams(sem):
    return pltpu.CompilerParams(dimension_semantics=sem, vmem_limit_bytes=VMEM_LIMIT)


def _split_bf16(a):
    hi = a.astype(BF16)
    lo = (a - hi.astype(F32)).astype(BF16)
    return hi, lo


def _dot3(a, b):
    ah, al = _split_bf16(a)
    bh, bl = _split_bf16(b)
    d = functools.partial(jnp.dot, preferred_element_type=F32)
    return d(ah, bh) + (d(al, bh) + d(ah, bl))


def _dot_nt(a, b):
    return lax.dot_general(a, b, (((1,), (1,)), ((), ())), preferred_element_type=F32)


def _rms(x, w):
    ms = jnp.mean(x * x, axis=-1, keepdims=True)
    return x * lax.rsqrt(ms + NORM_EPS) * w


def _adaln_kernel(c_ref, w_ref, b_ref, o_ref):
    c = c_ref[...]
    s = c * jax.nn.sigmoid(c)
    o_ref[0] = _dot3(s, w_ref[0]) + b_ref[0]


def _adaln(cvec, ada_w, ada_b):
    depth = ada_w.shape[0]
    n = ada_w.shape[2]
    tn = D_MODEL
    return pl.pallas_call(
        _adaln_kernel,
        grid=(depth, n // tn),
        in_specs=[pl.BlockSpec((16, D_MODEL), lambda l, j: (0, 0)),
                  pl.BlockSpec((1, D_MODEL, tn), lambda l, j: (l, 0, j)),
                  pl.BlockSpec((1, 1, tn), lambda l, j: (l, 0, j))],
        out_specs=pl.BlockSpec((1, 16, tn), lambda l, j: (l, 0, j)),
        out_shape=jax.ShapeDtypeStruct((depth, 16, n), F32),
        compiler_params=_cparams(("arbitrary", "arbitrary")),
        name="adaln",
    )(cvec, ada_w, ada_b.reshape(depth, 1, n))


def _inproj_kernel(*refs, split_input):
    if split_input:
        x_ref, ctx_ref, mod_ref, nw_ref, win_ref = refs[:5]
        outs = refs[5:13]
        xt = jnp.where(pl.program_id(1) == 0, ctx_ref[0], x_ref[0])
    else:
        x_ref, mod_ref, nw_ref, win_ref = refs[:4]
        outs = refs[4:12]
        xt = x_ref[0]
    w_ref = refs[-1]

    @pl.when(jnp.logical_and(pl.program_id(0) == 0, pl.program_id(1) == 0))
    def _():
        g0 = 5 * M_WIDTH
        n0 = g0 + G_COLS
        for k in range(5):
            cs = slice(k * M_WIDTH, (k + 1) * M_WIDTH)
            w_ref[:, cs] = win_ref[0, :, cs].astype(BF16)
        for k in range(2):
            w_ref[:, g0 + k * M_WIDTH:g0 + (k + 1) * M_WIDTH] = (
                win_ref[0, :, n0 + k * M_WIDTH:n0 + (k + 1) * M_WIDTH].astype(BF16))
        w_ref[:, 7 * M_WIDTH:] = win_ref[0, :, g0:g0 + G_PAD].astype(BF16)

    h = _rms(xt, nw_ref[...]) * (1.0 + mod_ref[0, 0, 1:2, :]) + mod_ref[0, 0, 0:1, :]
    hb = h.astype(BF16)
    for k in range(7):
        outs[k][0] = jnp.dot(hb, w_ref[:, k * M_WIDTH:(k + 1) * M_WIDTH],
                             preferred_element_type=F32).astype(BF16)
    outs[7][0] = jnp.dot(hb, w_ref[:, 7 * M_WIDTH:], preferred_element_type=F32)


def _stream_specs(split_input):
    tm = TOK_TILE
    if split_input:
        return [pl.BlockSpec((1, tm, D_MODEL), lambda b, j: (b, jnp.maximum(j - 1, 0), 0)),
                pl.BlockSpec((1, tm, D_MODEL), lambda b, j: (b, 0, 0))]
    return [pl.BlockSpec((1, tm, D_MODEL), lambda b, j: (b, j, 0))]


def _mod_spec():
    return pl.BlockSpec((1, 1, 6, D_MODEL), lambda b, j: (b, jnp.minimum(j, 1), 0, 0))


def _inproj(stream, mod, norm_w, w_in, layer, B, S):
    split_input = len(stream) == 2
    tm = TOK_TILE
    tok = lambda w: pl.BlockSpec((1, tm, w), lambda b, j: (b, j, 0))
    out_shape = [jax.ShapeDtypeStruct((B, S, M_WIDTH), BF16)] * 7 + [jax.ShapeDtypeStruct((B, S, G_PAD), F32)]
    return pl.pallas_call(
        functools.partial(_inproj_kernel, split_input=split_input),
        grid=(B, S // tm),
        in_specs=_stream_specs(split_input) + [
            _mod_spec(),
            pl.BlockSpec((1, D_MODEL), lambda b, j: (0, 0)),
            pl.BlockSpec((1,) + w_in.shape[1:], lambda b, j: (layer, 0, 0), pipeline_mode=pl.Buffered(1))],
        out_specs=[tok(M_WIDTH)] * 7 + [tok(G_PAD)],
        out_shape=out_shape,
        scratch_shapes=[pltpu.VMEM((D_MODEL, N_PROJ), BF16)],
        compiler_params=_cparams(("arbitrary", "arbitrary")),
        name="inproj",
    )(*stream, mod, norm_w.reshape(1, D_MODEL), w_in)


def _prep_kernel(q_ref, qp_ref, qn_ref, k_ref, kp_ref, kn_ref, cw_ref, cb_ref, cos_ref, sin_ref,
                 qo_ref, ko_ref, *, n_tiles):
    j = pl.program_id(1)
    tp = TOK_TILE
    has_prev = jnp.logical_and(j != 0, j != 1).astype(F32)
    has_next = jnp.logical_and(j != 0, j != n_tiles - 1).astype(F32)
    rows = lax.broadcasted_iota(jnp.int32, (tp, M_WIDTH), 0)
    lanes = lax.broadcasted_iota(jnp.int32, (tp, M_WIDTH), 1)
    low_half = (lanes % (ROPE_AXIS_DIM)) < (ROPE_AXIS_DIM // 2)
    cosv = jnp.concatenate([cos_ref[...]] * M_HEADS, axis=1)
    sinv = jnp.concatenate([sin_ref[...]] * M_HEADS, axis=1)

    def branch(x_ref, p_ref, n_ref, col0):
        x = x_ref[0].astype(F32)
        prev_row = p_ref[0, HALO - 1:HALO, :].astype(F32) * has_prev
        next_row = n_ref[0, 0:1, :].astype(F32) * has_next
        xm = jnp.where(rows == 0, prev_row, pltpu.roll(x, 1, 0))
        xp = jnp.where(rows == tp - 1, next_row, pltpu.roll(x, tp - 1, 0))
        w = cw_ref[:, col0:col0 + M_WIDTH]
        y = cb_ref[:, col0:col0 + M_WIDTH] + xm * w[0:1] + x * w[1:2] + xp * w[2:3]
        y = y * jax.nn.sigmoid(y)
        half = ROPE_AXIS_DIM // 2
        partner = jnp.where(low_half, pltpu.roll(y, M_WIDTH - half, 1), pltpu.roll(y, half, 1))
        return y * cosv + partner * sinv

    qo_ref[0] = branch(q_ref, qp_ref, qn_ref, 0).astype(BF16)
    ko_ref[0] = (branch(k_ref, kp_ref, kn_ref, M_WIDTH) * (M_HEAD_DIM ** -0.5)).T.astype(BF16)


def _prep(mq, mk, conv_w, conv_b, cos_t, sin_t, B, S):
    tp = TOK_TILE
    n_tiles = S // tp
    per = tp // HALO
    n_halo = S // HALO
    main = pl.BlockSpec((1, tp, M_WIDTH), lambda b, j: (b, j, 0))
    prev = pl.BlockSpec((1, HALO, M_WIDTH), lambda b, j: (b, jnp.maximum(j * per - 1, 0), 0))
    nxt = pl.BlockSpec((1, HALO, M_WIDTH), lambda b, j: (b, jnp.minimum((j + 1) * per, n_halo - 1), 0))
    return pl.pallas_call(
        functools.partial(_prep_kernel, n_tiles=n_tiles),
        grid=(B, n_tiles),
        in_specs=[main, prev, nxt, main, prev, nxt,
                  pl.BlockSpec((3, 2 * M_WIDTH), lambda b, j: (0, 0)),
                  pl.BlockSpec((1, 2 * M_WIDTH), lambda b, j: (0, 0)),
                  pl.BlockSpec((tp, M_HEAD_DIM), lambda b, j: (j, 0)),
                  pl.BlockSpec((tp, M_HEAD_DIM), lambda b, j: (j, 0))],
        out_specs=[main, pl.BlockSpec((1, M_WIDTH, tp), lambda b, j: (b, 0, j))],
        out_shape=[jax.ShapeDtypeStruct((B, S, M_WIDTH), BF16), jax.ShapeDtypeStruct((B, M_WIDTH, S), BF16)],
        compiler_params=_cparams(("arbitrary", "arbitrary")),
        name="mlstm_prep",
    )(mq, mq, mq, mk, mk, mk, conv_w, conv_b.reshape(1, -1), cos_t, sin_t)


def _scan_rows(x, reverse, op, fill):
    n = x.shape[0]
    rows = lax.broadcasted_iota(jnp.int32, x.shape, 0)
    sh = 1
    while sh < n:
        if reverse:
            x = op(x, jnp.where(rows < n - sh, pltpu.roll(x, n - sh, 0), fill))
        else:
            x = op(x, jnp.where(rows >= sh, pltpu.roll(x, sh, 0), fill))
        sh *= 2
    return x


def _log_sigmoid(x):
    return jnp.minimum(x, 0.0) - jnp.log(1.0 + jnp.exp(-jnp.abs(x)))


def _mlstm_kernel(qf_ref, kf_ref, vf_ref, gf_ref, qb_ref, kb_ref, vb_ref, gb_ref, gbias_ref,
                  hf_ref, hb_ref, cn_ref, m_ref):
    c = pl.program_id(1)

    @pl.when(c == 0)
    def _():
        cn_ref[...] = jnp.zeros_like(cn_ref)
        m_ref[...] = jnp.zeros_like(m_ref)

    t = CHUNK
    r_i = lax.broadcasted_iota(jnp.int32, (t, t), 0)
    c_i = lax.broadcasted_iota(jnp.int32, (t, t), 1)
    ones_blk = jnp.ones((t, LANES), BF16)

    dirs = ((qf_ref, kf_ref, vf_ref, gf_ref, hf_ref), (qb_ref, kb_ref, vb_ref, gb_ref, hb_ref))
    for step, d in ((s, d) for s in range(SCAN_CHUNKS) for d in range(2)):
        q_ref, k_ref, v_ref, g_ref, h_ref = dirs[d]
        reverse = d == 1
        ts = pl.ds((SCAN_CHUNKS - 1 - step if reverse else step) * t, t)
        mask = (c_i >= r_i) if reverse else (c_i <= r_i)
        end_row = 0 if reverse else t - 1
        g = g_ref[0, ts, :] + gbias_ref[...]
        bc = _scan_rows(_log_sigmoid(g), reverse, jnp.add, 0.0)
        b_al = pltpu.roll(bc, LANES - M_HEADS, 1)
        a = g - b_al
        m_old = m_ref[d][0:1, :]
        mx = jnp.maximum(m_old, _scan_rows(a, reverse, jnp.maximum, NEG))
        m_end = mx[end_row:end_row + 1, :]
        c_t = -mx
        e_den = jnp.exp(-(b_al + mx))
        w_old = jnp.exp(m_old - m_end)
        a_t = a.T
        w_tok_t = jnp.exp(a - m_end).T
        m_ref[d] = jnp.broadcast_to(b_al[end_row:end_row + 1, :] + m_end, (8, LANES))
        for h in range(M_HEADS):
            icol = 2 * M_HEADS * d + h
            hs = slice(h * M_HEAD_DIM, (h + 1) * M_HEAD_DIM)
            q = q_ref[0, ts, hs]
            k_t = k_ref[0, hs, ts]
            v_ext = jnp.concatenate([v_ref[0, ts, hs], ones_blk], axis=1)
            cn_old = cn_ref[d, h]

            c_b = jnp.broadcast_to(c_t[:, icol:icol + 1], (t, t))
            expo = jnp.concatenate([jnp.where(mask, c_b + a_t[icol:icol + 1, :], NEG),
                                    c_b + m_old[:, icol:icol + 1]], axis=1)
            qk = jnp.dot(q, k_t, preferred_element_type=F32)
            s_ext = (jnp.concatenate([qk, q.astype(F32)], axis=1) * jnp.exp(expo)).astype(BF16)
            rhs = jnp.concatenate([v_ext, cn_old.astype(BF16)], axis=0)
            ext = jnp.dot(s_ext, rhs, preferred_element_type=F32)
            den = jnp.maximum(jnp.abs(ext[:, M_HEAD_DIM:]),
                              jnp.broadcast_to(e_den[:, icol:icol + 1], (t, LANES)))
            h_ref[0, ts, hs] = (ext[:, :M_HEAD_DIM] / den).astype(BF16)

            kw_t = (k_t.astype(F32) * w_tok_t[icol:icol + 1, :]).astype(BF16)
            cn_ref[d, h] = (w_old[:, icol:icol + 1] * cn_old
                            + jnp.dot(kw_t, v_ext, preferred_element_type=F32))


def _mlstm(q, k, v, g, gate_b, B, S, n_ctx_chunks):
    rows = SCAN_CHUNKS * CHUNK
    assert n_ctx_chunks % SCAN_CHUNKS == 0 and S % rows == 0
    nc = S // rows
    ncx = n_ctx_chunks // SCAN_CHUNKS

    def bwd_chunk(c):
        return jnp.where(c < ncx, ncx - 1 - c, nc - 1 + ncx - c)

    fw = lambda w: pl.BlockSpec((1, rows, w), lambda b, c: (b, c, 0))
    bw = lambda w: pl.BlockSpec((1, rows, w), lambda b, c: (b, bwd_chunk(c), 0))
    fw_t = pl.BlockSpec((1, M_WIDTH, rows), lambda b, c: (b, 0, c))
    bw_t = pl.BlockSpec((1, M_WIDTH, rows), lambda b, c: (b, 0, bwd_chunk(c)))
    gbias = jnp.zeros((1, G_PAD), F32).at[0, :G_COLS].set(gate_b)
    return pl.pallas_call(
        _mlstm_kernel,
        grid=(B, nc),
        in_specs=[fw(M_WIDTH), fw_t, fw(M_WIDTH), fw(G_PAD),
                  bw(M_WIDTH), bw_t, bw(M_WIDTH), bw(G_PAD),
                  pl.BlockSpec((1, G_PAD), lambda b, c: (0, 0))],
        out_specs=[fw(M_WIDTH), bw(M_WIDTH)],
        out_shape=[jax.ShapeDtypeStruct((B, S, M_WIDTH), BF16)] * 2,
        scratch_shapes=[pltpu.VMEM((2, M_HEADS, M_HEAD_DIM, 2 * LANES), F32),
                        pltpu.VMEM((2, 8, LANES), F32)],
        compiler_params=_cparams(("arbitrary", "arbitrary")),
        name="mlstm_scan",
    )(q, k, v, g, q, k, v, g, gbias)


def _natten_kernel(typ_ref, st_ref, q_ref, k_ref, v_ref, bias_ref, o_ref, *, n_ctx):
    sub = NA_QROWS * GRID_W
    lane = lax.broadcasted_iota(jnp.int32, (sub, LANES), 1)
    kctx = k_ref[0, 0:n_ctx, :]
    vctx = v_ref[0, 0:n_ctx, :]

    def sub_block(sb, carry):
        typ = typ_ref[sb]
        st = pl.multiple_of(st_ref[sb], GRID_W)
        rows = pl.ds(pl.multiple_of(sb * sub, sub), sub)
        q = q_ref[0, rows, :] * (N_HEAD_DIM ** -0.5)
        kwin = k_ref[0, pl.ds(st, NA_WKEYS), :]
        vwin = v_ref[0, pl.ds(st, NA_WKEYS), :]
        first = lane < N_HEAD_DIM
        zero = jnp.zeros_like(q)
        q2 = jnp.concatenate([jnp.where(first, q, zero), jnp.where(first, zero, q)], axis=0)
        s_win = _dot_nt(q2, kwin) + bias_ref[typ].reshape(2 * sub, NA_WKEYS)
        s_ctx = _dot_nt(q2, kctx)
        m = jnp.maximum(jnp.max(s_win, axis=-1, keepdims=True), jnp.max(s_ctx, axis=-1, keepdims=True))
        p_win = jnp.exp(s_win - m)
        p_ctx = jnp.exp(s_ctx - m)
        l = jnp.sum(p_win, axis=-1, keepdims=True) + jnp.sum(p_ctx, axis=-1, keepdims=True)
        o = (jnp.dot(p_win.astype(BF16), vwin, preferred_element_type=F32)
             + jnp.dot(p_ctx.astype(BF16), vctx, preferred_element_type=F32)) / l
        o_ref[0, rows, :] = jnp.where(first, o[:sub], o[sub:]).astype(BF16)
        return carry

    lax.fori_loop(0, q_ref.shape[1] // sub, sub_block, 0, unroll=2)


def _natten_tables(rpb, rows, n_ctx):
    n_heads = rpb.shape[0]
    n_sb = rows // NA_QROWS
    ws_of = lambda sb: int(np.clip(NA_QROWS * sb - WIN_H // 2, 0, rows - NA_WROWS))
    rs_of = lambda r: int(np.clip(r - WIN_H // 2, 0, rows - WIN_H))
    shape_of = lambda sb: tuple((NA_QROWS * sb + qr - ws_of(sb), rs_of(NA_QROWS * sb + qr) - ws_of(sb))
                                for qr in range(NA_QROWS))
    shapes = sorted(set(shape_of(sb) for sb in range(n_sb)))
    reps = [next(sb for sb in range(n_sb) if shape_of(sb) == sh) for sh in shapes]
    n_dr = 2 * WIN_H - 1
    n_dc = 2 * WIN_W - 1
    lead = GRID_W - WIN_W
    vec = jnp.pad(rpb, ((0, 0), (0, 0), (lead, 2 * GRID_W - lead - n_dc)), constant_values=NEG)
    skew = jnp.broadcast_to(vec[:, :, None, :], (n_heads, n_dr, GRID_W, 2 * GRID_W))
    skew = skew.reshape(n_heads, n_dr, -1)[:, :, :GRID_W * (2 * GRID_W - 1)]
    toep = skew.reshape(n_heads, n_dr, GRID_W, 2 * GRID_W - 1)[..., GRID_W - 1:]
    cq = np.arange(GRID_W)[:, None]
    ck = np.arange(GRID_W)[None, :]
    cs = np.clip(cq - WIN_W // 2, 0, GRID_W - WIN_W)
    toep = jnp.where((ck >= cs) & (ck < cs + WIN_W), toep, NEG)
    masked_tile = jnp.full((n_heads, GRID_W, GRID_W), NEG, F32)
    types = []
    for sb in reps:
        ws = ws_of(sb)
        q_rows = []
        for qr in range(NA_QROWS):
            r = NA_QROWS * sb + qr
            rs = rs_of(r)
            tiles = []
            for kw in range(NA_WROWS):
                kr = ws + kw
                tiles.append(toep[:, kr - r + WIN_H - 1] if rs <= kr < rs + WIN_H else masked_tile)
            q_rows.append(jnp.concatenate(tiles, axis=-1))
        types.append(jnp.concatenate(q_rows, axis=-2))
    types.append(jnp.full_like(types[0], NEG))
    bias = jnp.stack(types, axis=0)
    n_ctx_sb = n_ctx // (NA_QROWS * GRID_W)
    typ, st = [len(shapes)] * n_ctx_sb, [n_ctx] * n_ctx_sb
    for sb in range(n_sb):
        typ.append(shapes.index(shape_of(sb)))
        st.append(n_ctx + ws_of(sb) * GRID_W)
    return jnp.asarray(typ, jnp.int32), jnp.asarray(st, jnp.int32), bias


def _natten(nq, nk, nv, typ, st, bias, layer, B, S, n_ctx):
    n_pairs = N_HEADS // 2
    grid_spec = pltpu.PrefetchScalarGridSpec(
        num_scalar_prefetch=2,
        grid=(n_pairs, B),
        in_specs=[pl.BlockSpec((1, S, LANES), lambda p, b, *_: (b, 0, p)),
                  pl.BlockSpec((1, S, LANES), lambda p, b, *_: (b, 0, p)),
                  pl.BlockSpec((1, S, LANES), lambda p, b, *_: (b, 0, p)),
                  pl.BlockSpec((bias.shape[0], 2, NA_QROWS * GRID_W, NA_WKEYS),
                               lambda p, b, *_: (0, layer * n_pairs + p, 0, 0))],
        out_specs=pl.BlockSpec((1, S, LANES), lambda p, b, *_: (b, 0, p)),
    )
    return pl.pallas_call(
        functools.partial(_natten_kernel, n_ctx=n_ctx),
        grid_spec=grid_spec,
        out_shape=jax.ShapeDtypeStruct((B, S, N_WIDTH), BF16),
        compiler_params=_cparams(("arbitrary", "arbitrary")),
        name="natten",
    )(typ, st, nq, nk, nv, bias)


def _route(logits_t, rb):
    e_i = lax.broadcasted_iota(jnp.int32, logits_t.shape, 0)
    z = logits_t - jnp.max(logits_t, axis=0, keepdims=True)
    ez = jnp.exp(z)
    scores = ez / jnp.sum(ez, axis=0, keepdims=True)
    sel = scores + rb
    best = None
    best_score = None
    for gi in range(N_GROUPS):
        r = [sel[gi * EXPERTS_PER_GROUP + u:gi * EXPERTS_PER_GROUP + u + 1, :] for u in range(EXPERTS_PER_GROUP)]
        gs = None
        for u in range(EXPERTS_PER_GROUP):
            for w in range(u + 1, EXPERTS_PER_GROUP):
                pair = r[u] + r[w]
                gs = pair if gs is None else jnp.maximum(gs, pair)
        if best is None:
            best, best_score = jnp.zeros(gs.shape, jnp.int32), gs
        else:
            better = gs > best_score
            best = jnp.where(better, gi, best)
            best_score = jnp.where(better, gs, best_score)
    masked = jnp.where((e_i // EXPERTS_PER_GROUP) == best, sel, -jnp.inf)
    v1 = jnp.max(masked, axis=0, keepdims=True)
    i1 = jnp.min(jnp.where(masked == v1, e_i, N_EXPERTS), axis=0, keepdims=True)
    masked2 = jnp.where(e_i == i1, -jnp.inf, masked)
    v2 = jnp.max(masked2, axis=0, keepdims=True)
    i2 = jnp.min(jnp.where(masked2 == v2, e_i, N_EXPERTS), axis=0, keepdims=True)
    w1 = jnp.sum(jnp.where(e_i == i1, scores, 0.0), axis=0, keepdims=True)
    w2 = jnp.sum(jnp.where(e_i == i2, scores, 0.0), axis=0, keepdims=True)
    tot = w1 + w2
    gate = jnp.where(e_i == i1, w1 / tot, 0.0) + jnp.where(e_i == i2, w2 / tot, 0.0)
    lo = jnp.minimum(i1, i2) - best * EXPERTS_PER_GROUP
    hi = jnp.maximum(i1, i2) - best * EXPERTS_PER_GROUP
    pair = ((lo * (2 * EXPERTS_PER_GROUP - 1 - lo)) >> 1) + (hi - lo - 1)
    return gate, best * PAIRS_PER_GROUP + pair


def _outproj_kernel(*refs, split_input):
    n_stream = 2 if split_input else 1
    stream = refs[:n_stream]
    (hf_ref, hb_ref, mo_ref, no_ref, mod_ref, mnw_ref, wo_in_ref, n2w_ref, rw_ref, rb_ref,
     xo_ref, rec_ref, grp_ref, rank_ref, cnt_ref, base_ref, wo_ref) = refs[n_stream:]

    @pl.when(jnp.logical_and(pl.program_id(0) == 0, pl.program_id(1) == 0))
    def _():
        base_ref[...] = jnp.zeros_like(base_ref)
        wo_ref[...] = wo_in_ref[0].astype(BF16)

    rh, rl = _split_bf16(rw_ref[...])
    n_rows = xo_ref.shape[1]
    half = n_rows // 2
    logits = []
    for r0 in (0, half):
        rs = slice(r0, r0 + half)
        if split_input:
            xt = jnp.where(pl.program_id(1) == 0, stream[1][0, rs], stream[0][0, rs])
        else:
            xt = stream[0][0, rs]
        hsum = hf_ref[0, rs].astype(F32) + hb_ref[0, rs].astype(F32)
        parts = []
        for h in range(M_HEADS):
            hs = slice(h * M_HEAD_DIM, (h + 1) * M_HEAD_DIM)
            parts.append(_rms(hsum[:, hs], mnw_ref[:, hs]))
        m_out = jnp.concatenate(parts, axis=1) * jax.nn.sigmoid(mo_ref[0, rs].astype(F32))
        y = (jnp.dot(m_out.astype(BF16), wo_ref[0:M_WIDTH, :], preferred_element_type=F32)
             + jnp.dot(no_ref[0, rs], wo_ref[M_WIDTH:, :], preferred_element_type=F32))
        x_new = xt + mod_ref[0, 0, 2:3, :] * y
        xo_ref[0, rs] = x_new
        hx = _rms(x_new, n2w_ref[...]) * (1.0 + mod_ref[0, 0, 4:5, :]) + mod_ref[0, 0, 3:4, :]
        rec_ref[0, rs, :D_MODEL] = hx
        hh, hl = _split_bf16(hx)
        logits.append(_dot_nt(rh, hh) + (_dot_nt(rh, hl) + _dot_nt(rl, hh)))
    logits_t = jnp.concatenate(logits, axis=1)
    gate_t, cls = _route(logits_t, rb_ref[...])
    pad = jnp.zeros((LANES - N_EXPERTS, gate_t.shape[1]), F32)
    rec_ref[0, :, D_MODEL:] = jnp.concatenate([gate_t, pad], axis=0).T
    tm = cls.shape[1]
    onehot = lax.broadcasted_iota(jnp.int32, (CLS_ROWS, tm), 0) == cls
    upper = (lax.broadcasted_iota(jnp.int32, (tm, tm), 0) <= lax.broadcasted_iota(jnp.int32, (tm, tm), 1))
    csum = jnp.dot(onehot.astype(BF16), upper.astype(BF16), preferred_element_type=F32)
    base = base_ref[:, 0:1]
    rank = jnp.sum(jnp.where(onehot, csum - 1.0 + base, 0.0), axis=0, keepdims=True)
    grp_ref[0, 0] = cls
    rank_ref[0, 0] = rank.astype(jnp.int32)
    total = base_ref[...] + csum[:, tm - 1:tm]
    base_ref[...] = total
    cnt_ref[...] = total


def _outproj(stream, hf, hb, mo, n_out, mod, mnorm_w, w_out, layer, norm2_w, router_wt, router_b, B, S):
    split_input = len(stream) == 2
    tm = TOK_TILE
    nt = S // tm
    tok = lambda w: pl.BlockSpec((1, tm, w), lambda b, j: (b, j, 0))
    idx = pl.BlockSpec((1, 1, 1, tm), lambda b, j: (b, j, 0, 0))
    const = lambda shape: pl.BlockSpec(shape, lambda b, j: (0,) * len(shape))
    return pl.pallas_call(
        functools.partial(_outproj_kernel, split_input=split_input),
        grid=(B, nt),
        in_specs=_stream_specs(split_input) + [
            tok(M_WIDTH), tok(M_WIDTH), tok(M_WIDTH), tok(N_WIDTH), _mod_spec(),
            const((1, M_WIDTH)),
            pl.BlockSpec((1, D_MODEL, D_MODEL), lambda b, j: (layer, 0, 0), pipeline_mode=pl.Buffered(1)),
            const((1, D_MODEL)), const((N_EXPERTS, D_MODEL)), const((N_EXPERTS, 1))],
        out_specs=[tok(D_MODEL), tok(REC_W), idx, idx, const((CLS_ROWS, LANES))],
        out_shape=[jax.ShapeDtypeStruct((B, S, D_MODEL), F32),
                   jax.ShapeDtypeStruct((B, S, REC_W), F32),
                   jax.ShapeDtypeStruct((B, nt, 1, tm), jnp.int32),
                   jax.ShapeDtypeStruct((B, nt, 1, tm), jnp.int32),
                   jax.ShapeDtypeStruct((CLS_ROWS, LANES), F32)],
        scratch_shapes=[pltpu.VMEM((CLS_ROWS, LANES), F32), pltpu.VMEM((D_MODEL, D_MODEL), BF16)],
        compiler_params=_cparams(("arbitrary", "arbitrary")),
        name="outproj_router",
    )(*stream, hf, hb, mo, n_out, mod, mnorm_w.reshape(1, -1), w_out, norm2_w.reshape(1, -1),
      router_wt, router_b.reshape(-1, 1))


def _row_copy_wait(src, dst, sem, n_rows):
    pltpu.make_async_copy(src.at[pl.ds(0, n_rows)], dst.at[pl.ds(0, n_rows)], sem).wait()


def _dispatch_kernel(pos_ref, rec_ref, init_ref, xs_ref, sem):
    del init_ref
    n = rec_ref.shape[0]

    def issue(r, carry):
        pltpu.make_async_copy(rec_ref.at[pl.ds(r, 1)], xs_ref.at[pl.ds(pos_ref[r], 1)], sem).start()
        return carry

    lax.fori_loop(0, n, issue, 0, unroll=8)
    _row_copy_wait(rec_ref, xs_ref, sem, n)


def _dispatch(pos, rec, init):
    T, w = rec.shape
    n_sorted = init.shape[0]
    tp = _flat_tile(T)
    return pl.pallas_call(
        _dispatch_kernel,
        grid=(T // tp,),
        in_specs=[pl.BlockSpec((tp,), lambda i: (i,), memory_space=pltpu.SMEM),
                  pl.BlockSpec((tp, w), lambda i: (i, 0)),
                  pl.BlockSpec(memory_space=pl.ANY)],
        out_specs=pl.BlockSpec(memory_space=pl.ANY),
        out_shape=jax.ShapeDtypeStruct((n_sorted, w), F32),
        scratch_shapes=[pltpu.SemaphoreType.DMA(())],
        input_output_aliases={2: 0},
        compiler_params=_cparams(("arbitrary",)),
        name="moe_dispatch",
    )(pos, rec, init)


def _experts_kernel(te_ref, nt_ref, xs_ref, w1_ref, w3_ref, w2_ref, ys_ref):
    i = pl.program_id(0)
    k = pl.program_id(1)
    valid = i < nt_ref[0]

    def contribution():
        e = te_ref[2 * i + k]
        h = xs_ref[:, :D_MODEL].astype(BF16)
        gate = xs_ref[:, D_MODEL:]
        lane = lax.broadcasted_iota(jnp.int32, gate.shape, 1)
        g_e = jnp.sum(jnp.where(lane == e, gate, 0.0), axis=-1, keepdims=True)
        a = jnp.dot(h, w1_ref[0].astype(BF16), preferred_element_type=F32)
        b3 = jnp.dot(h, w3_ref[0].astype(BF16), preferred_element_type=F32)
        act = (a * jax.nn.sigmoid(a) * b3 * g_e).astype(BF16)
        return jnp.dot(act, w2_ref[0].astype(BF16), preferred_element_type=F32)

    @pl.when(jnp.logical_and(valid, k == 0))
    def _():
        ys_ref[...] = contribution()

    @pl.when(jnp.logical_and(valid, k == 1))
    def _():
        ys_ref[...] += contribution()

    @pl.when(jnp.logical_and(jnp.logical_not(valid), k == 0))
    def _():
        ys_ref[...] = jnp.zeros_like(ys_ref)


def _experts(tile_experts, n_tiles, xs, w1, w3, w2, layer):
    n_sorted, w = xs.shape
    tm = MOE_TILE
    expert = lambda i, k, te, nt: (
        layer * N_EXPERTS + te[2 * jnp.minimum(i, nt[0] - 1) + jnp.where(i < nt[0], k, 1)], 0, 0)
    rows = lambda i, k, te, nt: (jnp.minimum(i, nt[0] - 1), 0)
    grid_spec = pltpu.PrefetchScalarGridSpec(
        num_scalar_prefetch=2,
        grid=(n_sorted // tm, 2),
        in_specs=[pl.BlockSpec((tm, w), rows),
                  pl.BlockSpec((1, D_MODEL, MOE_D_FF), expert),
                  pl.BlockSpec((1, D_MODEL, MOE_D_FF), expert),
                  pl.BlockSpec((1, MOE_D_FF, D_MODEL), expert)],
        out_specs=pl.BlockSpec((tm, D_MODEL), lambda i, k, *_: (i, 0)),
    )
    return pl.pallas_call(
        _experts_kernel,
        grid_spec=grid_spec,
        out_shape=jax.ShapeDtypeStruct((n_sorted, D_MODEL), F32),
        compiler_params=_cparams(("arbitrary", "arbitrary")),
        name="moe_experts",
    )(tile_experts, n_tiles, xs, w1, w3, w2)


def _combine_kernel(pos_ref, ys_ref, x_ref, mod_ref, fw_ref, o_ref, y_buf, sem, *, S, n_ctx, final_norm):
    n = y_buf.shape[0]

    def issue(r, carry):
        pltpu.make_async_copy(ys_ref.at[pl.ds(pos_ref[r], 1)], y_buf.at[pl.ds(r, 1)], sem).start()
        return carry

    lax.fori_loop(0, n, issue, 0, unroll=8)
    _row_copy_wait(ys_ref, y_buf, sem, n)
    for r0 in range(0, n, TOK_TILE):
        t0 = pl.program_id(0) * n + r0
        b = t0 // S
        kind = jnp.where(t0 - b * S < n_ctx, 0, 1)
        rs = slice(r0, r0 + TOK_TILE)
        x_new = x_ref[rs, :] + mod_ref[b, kind, 5:6, :] * y_buf[rs, :]
        if final_norm:
            x_new = _rms(x_new, fw_ref[...])
        o_ref[rs, :] = x_new


def _combine(pos, ys, xa, mod, final_w, S, n_ctx, final_norm):
    T = xa.shape[0]
    tu = _flat_tile(T)
    assert tu <= S
    return pl.pallas_call(
        functools.partial(_combine_kernel, S=S, n_ctx=n_ctx, final_norm=final_norm),
        grid=(T // tu,),
        in_specs=[pl.BlockSpec((tu,), lambda i: (i,), memory_space=pltpu.SMEM),
                  pl.BlockSpec(memory_space=pl.ANY),
                  pl.BlockSpec((tu, D_MODEL), lambda i: (i, 0)),
                  pl.BlockSpec(mod.shape, lambda i: (0, 0, 0, 0)),
                  pl.BlockSpec((1, D_MODEL), lambda i: (0, 0))],
        out_specs=pl.BlockSpec((tu, D_MODEL), lambda i: (i, 0)),
        out_shape=jax.ShapeDtypeStruct((T, D_MODEL), F32),
        scratch_shapes=[pltpu.VMEM((tu, D_MODEL), F32), pltpu.SemaphoreType.DMA(())],
        compiler_params=_cparams(("arbitrary",)),
        name="moe_combine",
    )(pos, ys, xa, mod, final_w.reshape(1, -1))


def _combine_latent_kernel(pos_ref, ys_ref, xa_ref, mod_ref, fw_ref, o_ref, y_buf, x_buf, sem, x_sem,
                           *, S, L, n_ctx):
    n = y_buf.shape[0]
    per_batch = L // n
    b = pl.program_id(0) // per_batch
    row0 = b * S + n_ctx + (pl.program_id(0) % per_batch) * n
    x_copy = pltpu.make_async_copy(xa_ref.at[pl.ds(row0, n)], x_buf, x_sem)
    x_copy.start()

    def issue(r, carry):
        pltpu.make_async_copy(ys_ref.at[pl.ds(pos_ref[r], 1)], y_buf.at[pl.ds(r, 1)], sem).start()
        return carry

    lax.fori_loop(0, n, issue, 0, unroll=8)
    x_copy.wait()
    _row_copy_wait(ys_ref, y_buf, sem, n)
    o_ref[...] = _rms(x_buf[...] + mod_ref[b, 1, 5:6, :] * y_buf[...], fw_ref[...])


def _combine_latent(pos, ys, xa, mod, final_w, B, S, L, n_ctx):
    tu = _flat_tile(L)
    pos_lat = pos.reshape(B, S)[:, n_ctx:].reshape(B * L)
    return pl.pallas_call(
        functools.partial(_combine_latent_kernel, S=S, L=L, n_ctx=n_ctx),
        grid=(B * L // tu,),
        in_specs=[pl.BlockSpec((tu,), lambda i: (i,), memory_space=pltpu.SMEM),
                  pl.BlockSpec(memory_space=pl.ANY),
                  pl.BlockSpec(memory_space=pl.ANY),
                  pl.BlockSpec(mod.shape, lambda i: (0, 0, 0, 0)),
                  pl.BlockSpec((1, D_MODEL), lambda i: (0, 0))],
        out_specs=pl.BlockSpec((tu, D_MODEL), lambda i: (i, 0)),
        out_shape=jax.ShapeDtypeStruct((B * L, D_MODEL), F32),
        scratch_shapes=[pltpu.VMEM((tu, D_MODEL), F32), pltpu.VMEM((tu, D_MODEL), F32),
                        pltpu.SemaphoreType.DMA(()), pltpu.SemaphoreType.DMA(())],
        compiler_params=_cparams(("arbitrary",)),
        name="moe_combine_latent",
    )(pos_lat, ys, xa, mod, final_w.reshape(1, -1))


def _flat_tile(T):
    return MOE_TILE if T % MOE_TILE == 0 else TOK_TILE


def _sorted_rows(T):
    return (-(-T // MOE_TILE) + N_CLASSES) * MOE_TILE


def _moe(rec, cls, rank, counts, xa, mod, w1, w3, w2, layer, final_w, sorted_init, B, S, n_ctx, last):
    T = B * S
    tm = MOE_TILE
    n_tiles_max = _sorted_rows(T) // tm
    cnt = counts[:N_CLASSES, 0].astype(jnp.int32)
    tiles = (cnt + tm - 1) // tm
    tile_end = jnp.cumsum(tiles)
    row0 = (tile_end - tiles) * tm
    cls = cls.reshape(T)
    pos = rank.reshape(T)
    for ci in range(N_CLASSES):
        pos = pos + jnp.where(cls == ci, row0[ci], 0)
    n_tiles = tile_end[N_CLASSES - 1:].astype(jnp.int32)
    tile_cls = jnp.sum(jnp.arange(n_tiles_max)[:, None] >= tile_end[None, :], axis=1)
    tile_cls = jnp.minimum(tile_cls, N_CLASSES - 1)
    grp0 = (tile_cls // PAIRS_PER_GROUP) * EXPERTS_PER_GROUP
    pair = tile_cls % PAIRS_PER_GROUP
    lo = sum(jnp.where(pair == p, PAIR_LO[p], 0) for p in range(PAIRS_PER_GROUP))
    hi = sum(jnp.where(pair == p, PAIR_HI[p], 0) for p in range(PAIRS_PER_GROUP))
    first_tile = jnp.sum(jnp.where(tile_cls[:, None] == jnp.arange(N_CLASSES)[None, :],
                                   (tile_end - tiles)[None, :], 0), axis=1)
    swap = (jnp.arange(n_tiles_max) - first_tile) % 2 == 1
    tile_experts = jnp.stack([grp0 + jnp.where(swap, hi, lo), grp0 + jnp.where(swap, lo, hi)],
                             axis=1).reshape(-1).astype(jnp.int32)
    xs = _dispatch(pos, rec.reshape(T, REC_W), sorted_init)
    ys = _experts(tile_experts, n_tiles, xs, w1, w3, w2, layer)
    if last:
        L = S - n_ctx
        out = _combine_latent(pos, ys, xa.reshape(T, D_MODEL), mod, final_w, B, S, L, n_ctx)
        return out.reshape(B, L, D_MODEL), xs
    out = _combine(pos, ys, xa.reshape(T, D_MODEL), mod, final_w, S, n_ctx, False)
    return out.reshape(B, S, D_MODEL), xs


def _rope_tables(L, n_ctx):
    t = jnp.arange(L)
    row = (t // GRID_W).astype(F32)
    col = (t % GRID_W).astype(F32)
    inv = ROPE_BASE ** (-jnp.arange(0, ROPE_AXIS_DIM, 2, dtype=F32) / ROPE_AXIS_DIM)
    ar = row[:, None] * inv
    ac = col[:, None] * inv
    cos_l = jnp.concatenate([jnp.cos(ar), jnp.cos(ar), jnp.cos(ac), jnp.cos(ac)], axis=1)
    sin_l = jnp.concatenate([-jnp.sin(ar), jnp.sin(ar), -jnp.sin(ac), jnp.sin(ac)], axis=1)
    cos_t = jnp.concatenate([jnp.ones((n_ctx, M_HEAD_DIM), F32), cos_l], axis=0)
    sin_t = jnp.concatenate([jnp.zeros((n_ctx, M_HEAD_DIM), F32), sin_l], axis=0)
    return cos_t, sin_t


def kernel(x, c, ctx, c_ctx, ada_w, ada_b, norm1_w, w_in, conv_w, conv_b, gate_b, mnorm_w, rpb, w_out,
           norm2_w, router_w, router_b, exp_w1, exp_w3, exp_w2, final_norm_w):
    B, L, _ = x.shape
    n_ctx = ctx.shape[1]
    depth = ada_w.shape[0]
    assert n_ctx == TOK_TILE and L % TOK_TILE == 0 and L % GRID_W == 0
    S = n_ctx + L
    rows = L // GRID_W
    assert rows >= NA_WROWS + 4 and rows % NA_QROWS == 0 and (S // 4) % HALO == 0

    cvec = jnp.concatenate([c, c_ctx[None], jnp.zeros((16 - B - 1, D_MODEL), F32)], axis=0)
    mods = _adaln(cvec, ada_w, ada_b)
    cos_t, sin_t = _rope_tables(L, n_ctx)
    router_wt = router_w.T
    typ, st, bias = _natten_tables(rpb.reshape((depth * N_HEADS,) + rpb.shape[2:]), rows, n_ctx)

    stream = (x, ctx)
    out = None
    sorted_buf = jnp.zeros((_sorted_rows(B * S), REC_W), F32)
    w1_all = exp_w1.reshape((depth * N_EXPERTS,) + exp_w1.shape[2:])
    w3_all = exp_w3.reshape((depth * N_EXPERTS,) + exp_w3.shape[2:])
    w2_all = exp_w2.reshape((depth * N_EXPERTS,) + exp_w2.shape[2:])
    for layer in range(depth):
        last = layer == depth - 1
        m6 = mods[layer].reshape(16, 6, D_MODEL)
        mod = jnp.stack([jnp.broadcast_to(m6[B], (B, 6, D_MODEL)), m6[:B]], axis=1)
        mq, mo, nq, mk, mv, nk, nv, g = _inproj(stream, mod, norm1_w[layer], w_in, layer, B, S)
        q, k = _prep(mq, mk, conv_w[layer], conv_b[layer], cos_t, sin_t, B, S)
        hf, hb = _mlstm(q, k, mv, g, gate_b[layer], B, S, n_ctx // CHUNK)
        n_out = _natten(nq, nk, nv, typ, st, bias, layer, B, S, n_ctx)
        xa, rec, cls, rank, counts = _outproj(stream, hf, hb, mo, n_out, mod, mnorm_w[layer], w_out, layer,
                                              norm2_w[layer], router_wt, router_b, B, S)
        out, sorted_buf = _moe(rec, cls, rank, counts, xa, mod, w1_all, w3_all, w2_all, layer,
                               final_norm_w, sorted_buf, B, S, n_ctx, last)
        stream = (out,)
    return out
```

```python
import functools

import numpy as np
import jax
import jax.numpy as jnp
from jax import lax
from jax.experimental import pallas as pl
from jax.experimental.pallas import tpu as pltpu

D_MODEL = 1024
M_WIDTH = 512
M_HEADS = 4
M_HEAD_DIM = 128
N_WIDTH = 512
N_HEADS = 8
N_HEAD_DIM = 64
G_COLS = 16
GRID_W = 64
WIN_H = 8
WIN_W = 16
CHUNK = 128
SCAN_CHUNKS = 2
ROPE_AXIS_DIM = 64
ROPE_BASE = 10000.0
N_EXPERTS = 16
N_GROUPS = 4
EXPERTS_PER_GROUP = 4
MOE_D_FF = 512
NORM_EPS = 1e-6

LANES = 128
TOK_TILE = 256
HALO = 16
NA_QROWS = 2
NA_WROWS = 10
NA_WKEYS = NA_WROWS * GRID_W
NEG = -1e30
G_PAD = LANES
N_PROJ = 7 * M_WIDTH + G_PAD
REC_W = D_MODEL + LANES
MOE_TILE = 1024
PAIRS_PER_GROUP = EXPERTS_PER_GROUP * (EXPERTS_PER_GROUP - 1) // 2
N_CLASSES = N_GROUPS * PAIRS_PER_GROUP
CLS_ROWS = 32
PAIR_LO = (0, 0, 0, 1, 1, 2)
PAIR_HI = (1, 2, 3, 2, 3, 3)
VMEM_LIMIT = 56 * 1024 * 1024

F32 = jnp.float32
BF16 = jnp.bfloat16


def _cparams(sem):
    return pltpu.CompilerParams(dimension_semantics=sem, vmem_limit_bytes=VMEM_LIMIT)


def _split_bf16(a):
    hi = a.astype(BF16)
    lo = (a - hi.astype(F32)).astype(BF16)
    return hi, lo


def _dot3(a, b):
    ah, al = _split_bf16(a)
    bh, bl = _split_bf16(b)
    d = functools.partial(jnp.dot, preferred_element_type=F32)
    return d(ah, bh) + (d(al, bh) + d(ah, bl))


def _dot_nt(a, b):
    return lax.dot_general(a, b, (((1,), (1,)), ((), ())), preferred_element_type=F32)


def _rms(x, w):
    ms = jnp.mean(x * x, axis=-1, keepdims=True)
    return x * lax.rsqrt(ms + NORM_EPS) * w


def _adaln_kernel(c_ref, w_ref, b_ref, o_ref):
    c = c_ref[...]
    s = c * jax.nn.sigmoid(c)
    o_ref[0] = _dot3(s, w_ref[0]) + b_ref[0]


def _adaln(cvec, ada_w, ada_b):
    depth = ada_w.shape[0]
    n = ada_w.shape[2]
    tn = D_MODEL
    return pl.pallas_call(
        _adaln_kernel,
        grid=(depth, n // tn),
        in_specs=[pl.BlockSpec((16, D_MODEL), lambda l, j: (0, 0)),
                  pl.BlockSpec((1, D_MODEL, tn), lambda l, j: (l, 0, j)),
                  pl.BlockSpec((1, 1, tn), lambda l, j: (l, 0, j))],
        out_specs=pl.BlockSpec((1, 16, tn), lambda l, j: (l, 0, j)),
        out_shape=jax.ShapeDtypeStruct((depth, 16, n), F32),
        compiler_params=_cparams(("arbitrary", "arbitrary")),
        name="adaln",
    )(cvec, ada_w, ada_b.reshape(depth, 1, n))


def _inproj_kernel(*refs, split_input):
    if split_input:
        x_ref, ctx_ref, mod_ref, nw_ref, win_ref = refs[:5]
        outs = refs[5:13]
        xt = jnp.where(pl.program_id(1) == 0, ctx_ref[0], x_ref[0])
    else:
        x_ref, mod_ref, nw_ref, win_ref = refs[:4]
        outs = refs[4:12]
        xt = x_ref[0]
    w_ref = refs[-1]

    @pl.when(jnp.logical_and(pl.program_id(0) == 0, pl.program_id(1) == 0))
    def _():
        g0 = 5 * M_WIDTH
        n0 = g0 + G_COLS
        for k in range(5):
            cs = slice(k * M_WIDTH, (k + 1) * M_WIDTH)
            w_ref[:, cs] = win_ref[0, :, cs].astype(BF16)
        for k in range(2):
            w_ref[:, g0 + k * M_WIDTH:g0 + (k + 1) * M_WIDTH] = (
                win_ref[0, :, n0 + k * M_WIDTH:n0 + (k + 1) * M_WIDTH].astype(BF16))
        w_ref[:, 7 * M_WIDTH:] = win_ref[0, :, g0:g0 + G_PAD].astype(BF16)

    h = _rms(xt, nw_ref[...]) * (1.0 + mod_ref[0, 0, 1:2, :]) + mod_ref[0, 0, 0:1, :]
    hb = h.astype(BF16)
    for k in range(7):
        outs[k][0] = jnp.dot(hb, w_ref[:, k * M_WIDTH:(k + 1) * M_WIDTH],
                             preferred_element_type=F32).astype(BF16)
    outs[7][0] = jnp.dot(hb, w_ref[:, 7 * M_WIDTH:], preferred_element_type=F32)


def _stream_specs(split_input):
    tm = TOK_TILE
    if split_input:
        return [pl.BlockSpec((1, tm, D_MODEL), lambda b, j: (b, jnp.maximum(j - 1, 0), 0)),
                pl.BlockSpec((1, tm, D_MODEL), lambda b, j: (b, 0, 0))]
    return [pl.BlockSpec((1, tm, D_MODEL), lambda b, j: (b, j, 0))]


def _mod_spec():
    return pl.BlockSpec((1, 1, 6, D_MODEL), lambda b, j: (b, jnp.minimum(j, 1), 0, 0))


def _inproj(stream, mod, norm_w, w_in, layer, B, S):
    split_input = len(stream) == 2
    tm = TOK_TILE
    tok = lambda w: pl.BlockSpec((1, tm, w), lambda b, j: (b, j, 0))
    out_shape = [jax.ShapeDtypeStruct((B, S, M_WIDTH), BF16)] * 7 + [jax.ShapeDtypeStruct((B, S, G_PAD), F32)]
    return pl.pallas_call(
        functools.partial(_inproj_kernel, split_input=split_input),
        grid=(B, S // tm),
        in_specs=_stream_specs(split_input) + [
            _mod_spec(),
            pl.BlockSpec((1, D_MODEL), lambda b, j: (0, 0)),
            pl.BlockSpec((1,) + w_in.shape[1:], lambda b, j: (layer, 0, 0), pipeline_mode=pl.Buffered(1))],
        out_specs=[tok(M_WIDTH)] * 7 + [tok(G_PAD)],
        out_shape=out_shape,
        scratch_shapes=[pltpu.VMEM((D_MODEL, N_PROJ), BF16)],
        compiler_params=_cparams(("arbitrary", "arbitrary")),
        name="inproj",
    )(*stream, mod, norm_w.reshape(1, D_MODEL), w_in)


def _prep_kernel(q_ref, qp_ref, qn_ref, k_ref, kp_ref, kn_ref, cw_ref, cb_ref, cos_ref, sin_ref,
                 qo_ref, ko_ref, *, n_tiles):
    j = pl.program_id(1)
    tp = TOK_TILE
    has_prev = jnp.logical_and(j != 0, j != 1).astype(F32)
    has_next = jnp.logical_and(j != 0, j != n_tiles - 1).astype(F32)
    rows = lax.broadcasted_iota(jnp.int32, (tp, M_WIDTH), 0)
    lanes = lax.broadcasted_iota(jnp.int32, (tp, M_WIDTH), 1)
    low_half = (lanes % (ROPE_AXIS_DIM)) < (ROPE_AXIS_DIM // 2)
    cosv = jnp.concatenate([cos_ref[...]] * M_HEADS, axis=1)
    sinv = jnp.concatenate([sin_ref[...]] * M_HEADS, axis=1)

    def branch(x_ref, p_ref, n_ref, col0):
        x = x_ref[0].astype(F32)
        prev_row = p_ref[0, HALO - 1:HALO, :].astype(F32) * has_prev
        next_row = n_ref[0, 0:1, :].astype(F32) * has_next
        xm = jnp.where(rows == 0, prev_row, pltpu.roll(x, 1, 0))
        xp = jnp.where(rows == tp - 1, next_row, pltpu.roll(x, tp - 1, 0))
        w = cw_ref[:, col0:col0 + M_WIDTH]
        y = cb_ref[:, col0:col0 + M_WIDTH] + xm * w[0:1] + x * w[1:2] + xp * w[2:3]
        y = y * jax.nn.sigmoid(y)
        half = ROPE_AXIS_DIM // 2
        partner = jnp.where(low_half, pltpu.roll(y, M_WIDTH - half, 1), pltpu.roll(y, half, 1))
        return y * cosv + partner * sinv

    qo_ref[0] = branch(q_ref, qp_ref, qn_ref, 0).astype(BF16)
    ko_ref[0] = (branch(k_ref, kp_ref, kn_ref, M_WIDTH) * (M_HEAD_DIM ** -0.5)).T.astype(BF16)


def _prep(mq, mk, conv_w, conv_b, cos_t, sin_t, B, S):
    tp = TOK_TILE
    n_tiles = S // tp
    per = tp // HALO
    n_halo = S // HALO
    main = pl.BlockSpec((1, tp, M_WIDTH), lambda b, j: (b, j, 0))
    prev = pl.BlockSpec((1, HALO, M_WIDTH), lambda b, j: (b, jnp.maximum(j * per - 1, 0), 0))
    nxt = pl.BlockSpec((1, HALO, M_WIDTH), lambda b, j: (b, jnp.minimum((j + 1) * per, n_halo - 1), 0))
    return pl.pallas_call(
        functools.partial(_prep_kernel, n_tiles=n_tiles),
        grid=(B, n_tiles),
        in_specs=[main, prev, nxt, main, prev, nxt,
                  pl.BlockSpec((3, 2 * M_WIDTH), lambda b, j: (0, 0)),
                  pl.BlockSpec((1, 2 * M_WIDTH), lambda b, j: (0, 0)),
                  pl.BlockSpec((tp, M_HEAD_DIM), lambda b, j: (j, 0)),
                  pl.BlockSpec((tp, M_HEAD_DIM), lambda b, j: (j, 0))],
        out_specs=[main, pl.BlockSpec((1, M_WIDTH, tp), lambda b, j: (b, 0, j))],
        out_shape=[jax.ShapeDtypeStruct((B, S, M_WIDTH), BF16), jax.ShapeDtypeStruct((B, M_WIDTH, S), BF16)],
        compiler_params=_cparams(("arbitrary", "arbitrary")),
        name="mlstm_prep",
    )(mq, mq, mq, mk, mk, mk, conv_w, conv_b.reshape(1, -1), cos_t, sin_t)


def _scan_rows(x, reverse, op, fill):
    n = x.shape[0]
    rows = lax.broadcasted_iota(jnp.int32, x.shape, 0)
    sh = 1
    while sh < n:
        if reverse:
            x = op(x, jnp.where(rows < n - sh, pltpu.roll(x, n - sh, 0), fill))
        else:
            x = op(x, jnp.where(rows >= sh, pltpu.roll(x, sh, 0), fill))
        sh *= 2
    return x


def _log_sigmoid(x):
    return jnp.minimum(x, 0.0) - jnp.log(1.0 + jnp.exp(-jnp.abs(x)))


def _mlstm_kernel(qf_ref, kf_ref, vf_ref, gf_ref, qb_ref, kb_ref, vb_ref, gb_ref, gbias_ref,
                  hf_ref, hb_ref, cn_ref, m_ref):
    c = pl.program_id(1)

    @pl.when(c == 0)
    def _():
        cn_ref[...] = jnp.zeros_like(cn_ref)
        m_ref[...] = jnp.zeros_like(m_ref)

    t = CHUNK
    r_i = lax.broadcasted_iota(jnp.int32, (t, t), 0)
    c_i = lax.broadcasted_iota(jnp.int32, (t, t), 1)
    ones_blk = jnp.ones((t, LANES), BF16)

    dirs = ((qf_ref, kf_ref, vf_ref, gf_ref, hf_ref), (qb_ref, kb_ref, vb_ref, gb_ref, hb_ref))
    for step, d in ((s, d) for s in range(SCAN_CHUNKS) for d in range(2)):
        q_ref, k_ref, v_ref, g_ref, h_ref = dirs[d]
        reverse = d == 1
        ts = pl.ds((SCAN_CHUNKS - 1 - step if reverse else step) * t, t)
        mask = (c_i >= r_i) if reverse else (c_i <= r_i)
        end_row = 0 if reverse else t - 1
        g = g_ref[0, ts, :] + gbias_ref[...]
        bc = _scan_rows(_log_sigmoid(g), reverse, jnp.add, 0.0)
        b_al = pltpu.roll(bc, LANES - M_HEADS, 1)
        a = g - b_al
        m_old = m_ref[d][0:1, :]
        mx = jnp.maximum(m_old, _scan_rows(a, reverse, jnp.maximum, NEG))
        m_end = mx[end_row:end_row + 1, :]
        c_t = -mx
        e_den = jnp.exp(-(b_al + mx))
        w_old = jnp.exp(m_old - m_end)
        a_t = a.T
        w_tok_t = jnp.exp(a - m_end).T
        m_ref[d] = jnp.broadcast_to(b_al[end_row:end_row + 1, :] + m_end, (8, LANES))
        for h in range(M_HEADS):
            icol = 2 * M_HEADS * d + h
            hs = slice(h * M_HEAD_DIM, (h + 1) * M_HEAD_DIM)
            q = q_ref[0, ts, hs]
            k_t = k_ref[0, hs, ts]
            v_ext = jnp.concatenate([v_ref[0, ts, hs], ones_blk], axis=1)
            cn_old = cn_ref[d, h]

            c_b = jnp.broadcast_to(c_t[:, icol:icol + 1], (t, t))
            expo = jnp.concatenate([jnp.where(mask, c_b + a_t[icol:icol + 1, :], NEG),
                                    c_b + m_old[:, icol:icol + 1]], axis=1)
            qk = jnp.dot(q, k_t, preferred_element_type=F32)
            s_ext = (jnp.concatenate([qk, q.astype(F32)], axis=1) * jnp.exp(expo)).astype(BF16)
            rhs = jnp.concatenate([v_ext, cn_old.astype(BF16)], axis=0)
            ext = jnp.dot(s_ext, rhs, preferred_element_type=F32)
            den = jnp.maximum(jnp.abs(ext[:, M_HEAD_DIM:]),
                              jnp.broadcast_to(e_den[:, icol:icol + 1], (t, LANES)))
            h_ref[0, ts, hs] = (ext[:, :M_HEAD_DIM] / den).astype(BF16)

            kw_t = (k_t.astype(F32) * w_tok_t[icol:icol + 1, :]).astype(BF16)
            cn_ref[d, h] = (w_old[:, icol:icol + 1] * cn_old
                            + jnp.dot(kw_t, v_ext, preferred_element_type=F32))


def _mlstm(q, k, v, g, gate_b, B, S, n_ctx_chunks):
    rows = SCAN_CHUNKS * CHUNK
    assert n_ctx_chunks % SCAN_CHUNKS == 0 and S % rows == 0
    nc = S // rows
    ncx = n_ctx_chunks // SCAN_CHUNKS

    def bwd_chunk(c):
        return jnp.where(c < ncx, ncx - 1 - c, nc - 1 + ncx - c)

    fw = lambda w: pl.BlockSpec((1, rows, w), lambda b, c: (b, c, 0))
    bw = lambda w: pl.BlockSpec((1, rows, w), lambda b, c: (b, bwd_chunk(c), 0))
    fw_t = pl.BlockSpec((1, M_WIDTH, rows), lambda b, c: (b, 0, c))
    bw_t = pl.BlockSpec((1, M_WIDTH, rows), lambda b, c: (b, 0, bwd_chunk(c)))
    gbias = jnp.zeros((1, G_PAD), F32).at[0, :G_COLS].set(gate_b)
    return pl.pallas_call(
        _mlstm_kernel,
        grid=(B, nc),
        in_specs=[fw(M_WIDTH), fw_t, fw(M_WIDTH), fw(G_PAD),
                  bw(M_WIDTH), bw_t, bw(M_WIDTH), bw(G_PAD),
                  pl.BlockSpec((1, G_PAD), lambda b, c: (0, 0))],
        out_specs=[fw(M_WIDTH), bw(M_WIDTH)],
        out_shape=[jax.ShapeDtypeStruct((B, S, M_WIDTH), BF16)] * 2,
        scratch_shapes=[pltpu.VMEM((2, M_HEADS, M_HEAD_DIM, 2 * LANES), F32),
                        pltpu.VMEM((2, 8, LANES), F32)],
        compiler_params=_cparams(("arbitrary", "arbitrary")),
        name="mlstm_scan",
    )(q, k, v, g, q, k, v, g, gbias)


def _natten_kernel(typ_ref, st_ref, q_ref, k_ref, v_ref, bias_ref, o_ref, *, n_ctx):
    sub = NA_QROWS * GRID_W
    lane = lax.broadcasted_iota(jnp.int32, (sub, LANES), 1)
    kctx = k_ref[0, 0:n_ctx, :]
    vctx = v_ref[0, 0:n_ctx, :]

    def sub_block(sb, carry):
        typ = typ_ref[sb]
        st = pl.multiple_of(st_ref[sb], GRID_W)
        rows = pl.ds(pl.multiple_of(sb * sub, sub), sub)
        q = q_ref[0, rows, :] * (N_HEAD_DIM ** -0.5)
        kwin = k_ref[0, pl.ds(st, NA_WKEYS), :]
        vwin = v_ref[0, pl.ds(st, NA_WKEYS), :]
        first = lane < N_HEAD_DIM
        zero = jnp.zeros_like(q)
        q2 = jnp.concatenate([jnp.where(first, q, zero), jnp.where(first, zero, q)], axis=0)
        s_win = _dot_nt(q2, kwin) + bias_ref[typ].reshape(2 * sub, NA_WKEYS)
        s_ctx = _dot_nt(q2, kctx)
        m = jnp.maximum(jnp.max(s_win, axis=-1, keepdims=True), jnp.max(s_ctx, axis=-1, keepdims=True))
        p_win = jnp.exp(s_win - m)
        p_ctx = jnp.exp(s_ctx - m)
        l = jnp.sum(p_win, axis=-1, keepdims=True) + jnp.sum(p_ctx, axis=-1, keepdims=True)
        o = (jnp.dot(p_win.astype(BF16), vwin, preferred_element_type=F32)
             + jnp.dot(p_ctx.astype(BF16), vctx, preferred_element_type=F32)) / l
        o_ref[0, rows, :] = jnp.where(first, o[:sub], o[sub:]).astype(BF16)
        return carry

    lax.fori_loop(0, q_ref.shape[1] // sub, sub_block, 0, unroll=4)


def _natten_tables(rpb, rows, n_ctx):
    n_heads = rpb.shape[0]
    n_sb = rows // NA_QROWS
    ws_of = lambda sb: int(np.clip(NA_QROWS * sb - WIN_H // 2, 0, rows - NA_WROWS))
    rs_of = lambda r: int(np.clip(r - WIN_H // 2, 0, rows - WIN_H))
    shape_of = lambda sb: tuple((NA_QROWS * sb + qr - ws_of(sb), rs_of(NA_QROWS * sb + qr) - ws_of(sb))
                                for qr in range(NA_QROWS))
    shapes = sorted(set(shape_of(sb) for sb in range(n_sb)))
    reps = [next(sb for sb in range(n_sb) if shape_of(sb) == sh) for sh in shapes]
    n_dr = 2 * WIN_H - 1
    n_dc = 2 * WIN_W - 1
    lead = GRID_W - WIN_W
    vec = jnp.pad(rpb, ((0, 0), (0, 0), (lead, 2 * GRID_W - lead - n_dc)), constant_values=NEG)
    skew = jnp.broadcast_to(vec[:, :, None, :], (n_heads, n_dr, GRID_W, 2 * GRID_W))
    skew = skew.reshape(n_heads, n_dr, -1)[:, :, :GRID_W * (2 * GRID_W - 1)]
    toep = skew.reshape(n_heads, n_dr, GRID_W, 2 * GRID_W - 1)[..., GRID_W - 1:]
    cq = np.arange(GRID_W)[:, None]
    ck = np.arange(GRID_W)[None, :]
    cs = np.clip(cq - WIN_W // 2, 0, GRID_W - WIN_W)
    toep = jnp.where((ck >= cs) & (ck < cs + WIN_W), toep, NEG)
    masked_tile = jnp.full((n_heads, GRID_W, GRID_W), NEG, F32)
    types = []
    for sb in reps:
        ws = ws_of(sb)
        q_rows = []
        for qr in range(NA_QROWS):
            r = NA_QROWS * sb + qr
            rs = rs_of(r)
            tiles = []
            for kw in range(NA_WROWS):
                kr = ws + kw
                tiles.append(toep[:, kr - r + WIN_H - 1] if rs <= kr < rs + WIN_H else masked_tile)
            q_rows.append(jnp.concatenate(tiles, axis=-1))
        types.append(jnp.concatenate(q_rows, axis=-2))
    types.append(jnp.full_like(types[0], NEG))
    bias = jnp.stack(types, axis=0)
    n_ctx_sb = n_ctx // (NA_QROWS * GRID_W)
    typ, st = [len(shapes)] * n_ctx_sb, [n_ctx] * n_ctx_sb
    for sb in range(n_sb):
        typ.append(shapes.index(shape_of(sb)))
        st.append(n_ctx + ws_of(sb) * GRID_W)
    return jnp.asarray(typ, jnp.int32), jnp.asarray(st, jnp.int32), bias


def _natten(nq, nk, nv, typ, st, bias, layer, B, S, n_ctx):
    n_pairs = N_HEADS // 2
    grid_spec = pltpu.PrefetchScalarGridSpec(
        num_scalar_prefetch=2,
        grid=(n_pairs, B),
        in_specs=[pl.BlockSpec((1, S, LANES), lambda p, b, *_: (b, 0, p)),
                  pl.BlockSpec((1, S, LANES), lambda p, b, *_: (b, 0, p)),
                  pl.BlockSpec((1, S, LANES), lambda p, b, *_: (b, 0, p)),
                  pl.BlockSpec((bias.shape[0], 2, NA_QROWS * GRID_W, NA_WKEYS),
                               lambda p, b, *_: (0, layer * n_pairs + p, 0, 0))],
        out_specs=pl.BlockSpec((1, S, LANES), lambda p, b, *_: (b, 0, p)),
    )
    return pl.pallas_call(
        functools.partial(_natten_kernel, n_ctx=n_ctx),
        grid_spec=grid_spec,
        out_shape=jax.ShapeDtypeStruct((B, S, N_WIDTH), BF16),
        compiler_params=_cparams(("arbitrary", "arbitrary")),
        name="natten",
    )(typ, st, nq, nk, nv, bias)


def _route(logits_t, rb):
    e_i = lax.broadcasted_iota(jnp.int32, logits_t.shape, 0)
    z = logits_t - jnp.max(logits_t, axis=0, keepdims=True)
    ez = jnp.exp(z)
    scores = ez / jnp.sum(ez, axis=0, keepdims=True)
    sel = scores + rb
    best = None
    best_score = None
    for gi in range(N_GROUPS):
        r = [sel[gi * EXPERTS_PER_GROUP + u:gi * EXPERTS_PER_GROUP + u + 1, :] for u in range(EXPERTS_PER_GROUP)]
        gs = None
        for u in range(EXPERTS_PER_GROUP):
            for w in range(u + 1, EXPERTS_PER_GROUP):
                pair = r[u] + r[w]
                gs = pair if gs is None else jnp.maximum(gs, pair)
        if best is None:
            best, best_score = jnp.zeros(gs.shape, jnp.int32), gs
        else:
            better = gs > best_score
            best = jnp.where(better, gi, best)
            best_score = jnp.where(better, gs, best_score)
    masked = jnp.where((e_i // EXPERTS_PER_GROUP) == best, sel, -jnp.inf)
    v1 = jnp.max(masked, axis=0, keepdims=True)
    i1 = jnp.min(jnp.where(masked == v1, e_i, N_EXPERTS), axis=0, keepdims=True)
    masked2 = jnp.where(e_i == i1, -jnp.inf, masked)
    v2 = jnp.max(masked2, axis=0, keepdims=True)
    i2 = jnp.min(jnp.where(masked2 == v2, e_i, N_EXPERTS), axis=0, keepdims=True)
    w1 = jnp.sum(jnp.where(e_i == i1, scores, 0.0), axis=0, keepdims=True)
    w2 = jnp.sum(jnp.where(e_i == i2, scores, 0.0), axis=0, keepdims=True)
    tot = w1 + w2
    gate = jnp.where(e_i == i1, w1 / tot, 0.0) + jnp.where(e_i == i2, w2 / tot, 0.0)
    lo = jnp.minimum(i1, i2) - best * EXPERTS_PER_GROUP
    hi = jnp.maximum(i1, i2) - best * EXPERTS_PER_GROUP
    pair = ((lo * (2 * EXPERTS_PER_GROUP - 1 - lo)) >> 1) + (hi - lo - 1)
    return gate, best * PAIRS_PER_GROUP + pair


def _outproj_kernel(*refs, split_input):
    n_stream = 2 if split_input else 1
    stream = refs[:n_stream]
    (hf_ref, hb_ref, mo_ref, no_ref, mod_ref, mnw_ref, wo_in_ref, n2w_ref, rw_ref, rb_ref,
     xo_ref, rec_ref, grp_ref, rank_ref, cnt_ref, base_ref, wo_ref) = refs[n_stream:]

    @pl.when(jnp.logical_and(pl.program_id(0) == 0, pl.program_id(1) == 0))
    def _():
        base_ref[...] = jnp.zeros_like(base_ref)
        wo_ref[...] = wo_in_ref[0].astype(BF16)

    rh, rl = _split_bf16(rw_ref[...])
    n_rows = xo_ref.shape[1]
    half = n_rows // 2
    logits = []
    for r0 in (0, half):
        rs = slice(r0, r0 + half)
        if split_input:
            xt = jnp.where(pl.program_id(1) == 0, stream[1][0, rs], stream[0][0, rs])
        else:
            xt = stream[0][0, rs]
        hsum = hf_ref[0, rs].astype(F32) + hb_ref[0, rs].astype(F32)
        parts = []
        for h in range(M_HEADS):
            hs = slice(h * M_HEAD_DIM, (h + 1) * M_HEAD_DIM)
            parts.append(_rms(hsum[:, hs], mnw_ref[:, hs]))
        m_out = jnp.concatenate(parts, axis=1) * jax.nn.sigmoid(mo_ref[0, rs].astype(F32))
        y = (jnp.dot(m_out.astype(BF16), wo_ref[0:M_WIDTH, :], preferred_element_type=F32)
             + jnp.dot(no_ref[0, rs], wo_ref[M_WIDTH:, :], preferred_element_type=F32))
        x_new = xt + mod_ref[0, 0, 2:3, :] * y
        xo_ref[0, rs] = x_new
        hx = _rms(x_new, n2w_ref[...]) * (1.0 + mod_ref[0, 0, 4:5, :]) + mod_ref[0, 0, 3:4, :]
        rec_ref[0, rs, :D_MODEL] = hx
        hh, hl = _split_bf16(hx)
        logits.append(_dot_nt(rh, hh) + (_dot_nt(rh, hl) + _dot_nt(rl, hh)))
    logits_t = jnp.concatenate(logits, axis=1)
    gate_t, cls = _route(logits_t, rb_ref[...])
    pad = jnp.zeros((LANES - N_EXPERTS, gate_t.shape[1]), F32)
    rec_ref[0, :, D_MODEL:] = jnp.concatenate([gate_t, pad], axis=0).T
    tm = cls.shape[1]
    onehot = lax.broadcasted_iota(jnp.int32, (CLS_ROWS, tm), 0) == cls
    upper = (lax.broadcasted_iota(jnp.int32, (tm, tm), 0) <= lax.broadcasted_iota(jnp.int32, (tm, tm), 1))
    csum = jnp.dot(onehot.astype(BF16), upper.astype(BF16), preferred_element_type=F32)
    base = base_ref[:, 0:1]
    rank = jnp.sum(jnp.where(onehot, csum - 1.0 + base, 0.0), axis=0, keepdims=True)
    grp_ref[0, 0] = cls
    rank_ref[0, 0] = rank.astype(jnp.int32)
    total = base_ref[...] + csum[:, tm - 1:tm]
    base_ref[...] = total
    cnt_ref[...] = total


def _outproj(stream, hf, hb, mo, n_out, mod, mnorm_w, w_out, layer, norm2_w, router_wt, router_b, B, S):
    split_input = len(stream) == 2
    tm = TOK_TILE
    nt = S // tm
    tok = lambda w: pl.BlockSpec((1, tm, w), lambda b, j: (b, j, 0))
    idx = pl.BlockSpec((1, 1, 1, tm), lambda b, j: (b, j, 0, 0))
    const = lambda shape: pl.BlockSpec(shape, lambda b, j: (0,) * len(shape))
    return pl.pallas_call(
        functools.partial(_outproj_kernel, split_input=split_input),
        grid=(B, nt),
        in_specs=_stream_specs(split_input) + [
            tok(M_WIDTH), tok(M_WIDTH), tok(M_WIDTH), tok(N_WIDTH), _mod_spec(),
            const((1, M_WIDTH)),
            pl.BlockSpec((1, D_MODEL, D_MODEL), lambda b, j: (layer, 0, 0), pipeline_mode=pl.Buffered(1)),
            const((1, D_MODEL)), const((N_EXPERTS, D_MODEL)), const((N_EXPERTS, 1))],
        out_specs=[tok(D_MODEL), tok(REC_W), idx, idx, const((CLS_ROWS, LANES))],
        out_shape=[jax.ShapeDtypeStruct((B, S, D_MODEL), F32),
                   jax.ShapeDtypeStruct((B, S, REC_W), F32),
                   jax.ShapeDtypeStruct((B, nt, 1, tm), jnp.int32),
                   jax.ShapeDtypeStruct((B, nt, 1, tm), jnp.int32),
                   jax.ShapeDtypeStruct((CLS_ROWS, LANES), F32)],
        scratch_shapes=[pltpu.VMEM((CLS_ROWS, LANES), F32), pltpu.VMEM((D_MODEL, D_MODEL), BF16)],
        compiler_params=_cparams(("arbitrary", "arbitrary")),
        name="outproj_router",
    )(*stream, hf, hb, mo, n_out, mod, mnorm_w.reshape(1, -1), w_out, norm2_w.reshape(1, -1),
      router_wt, router_b.reshape(-1, 1))


def _row_copy_wait(src, dst, sem, n_rows):
    pltpu.make_async_copy(src.at[pl.ds(0, n_rows)], dst.at[pl.ds(0, n_rows)], sem).wait()


def _dispatch_kernel(pos_ref, rec_ref, init_ref, xs_ref, sem):
    del init_ref
    n = rec_ref.shape[0]

    def issue(r, carry):
        pltpu.make_async_copy(rec_ref.at[pl.ds(r, 1)], xs_ref.at[pl.ds(pos_ref[r], 1)], sem).start()
        return carry

    lax.fori_loop(0, n, issue, 0, unroll=8)
    _row_copy_wait(rec_ref, xs_ref, sem, n)


def _dispatch(pos, rec, init):
    T, w = rec.shape
    n_sorted = init.shape[0]
    tp = _flat_tile(T)
    return pl.pallas_call(
        _dispatch_kernel,
        grid=(T // tp,),
        in_specs=[pl.BlockSpec((tp,), lambda i: (i,), memory_space=pltpu.SMEM),
                  pl.BlockSpec((tp, w), lambda i: (i, 0)),
                  pl.BlockSpec(memory_space=pl.ANY)],
        out_specs=pl.BlockSpec(memory_space=pl.ANY),
        out_shape=jax.ShapeDtypeStruct((n_sorted, w), F32),
        scratch_shapes=[pltpu.SemaphoreType.DMA(())],
        input_output_aliases={2: 0},
        compiler_params=_cparams(("arbitrary",)),
        name="moe_dispatch",
    )(pos, rec, init)


def _experts_kernel(te_ref, nt_ref, xs_ref, w1_ref, w3_ref, w2_ref, ys_ref):
    i = pl.program_id(0)
    k = pl.program_id(1)
    valid = i < nt_ref[0]

    def contribution():
        e = te_ref[2 * i + k]
        h = xs_ref[:, :D_MODEL].astype(BF16)
        gate = xs_ref[:, D_MODEL:]
        lane = lax.broadcasted_iota(jnp.int32, gate.shape, 1)
        g_e = jnp.sum(jnp.where(lane == e, gate, 0.0), axis=-1, keepdims=True)
        a = jnp.dot(h, w1_ref[0].astype(BF16), preferred_element_type=F32)
        b3 = jnp.dot(h, w3_ref[0].astype(BF16), preferred_element_type=F32)
        act = (a * jax.nn.sigmoid(a) * b3 * g_e).astype(BF16)
        return jnp.dot(act, w2_ref[0].astype(BF16), preferred_element_type=F32)

    @pl.when(jnp.logical_and(valid, k == 0))
    def _():
        ys_ref[...] = contribution()

    @pl.when(jnp.logical_and(valid, k == 1))
    def _():
        ys_ref[...] += contribution()

    @pl.when(jnp.logical_and(jnp.logical_not(valid), k == 0))
    def _():
        ys_ref[...] = jnp.zeros_like(ys_ref)


def _experts(tile_experts, n_tiles, xs, w1, w3, w2, layer):
    n_sorted, w = xs.shape
    tm = MOE_TILE
    expert = lambda i, k, te, nt: (
        layer * N_EXPERTS + te[2 * jnp.minimum(i, nt[0] - 1) + jnp.where(i < nt[0], k, 1)], 0, 0)
    rows = lambda i, k, te, nt: (jnp.minimum(i, nt[0] - 1), 0)
    grid_spec = pltpu.PrefetchScalarGridSpec(
        num_scalar_prefetch=2,
        grid=(n_sorted // tm, 2),
        in_specs=[pl.BlockSpec((tm, w), rows),
                  pl.BlockSpec((1, D_MODEL, MOE_D_FF), expert),
                  pl.BlockSpec((1, D_MODEL, MOE_D_FF), expert),
                  pl.BlockSpec((1, MOE_D_FF, D_MODEL), expert)],
        out_specs=pl.BlockSpec((tm, D_MODEL), lambda i, k, *_: (i, 0)),
    )
    return pl.pallas_call(
        _experts_kernel,
        grid_spec=grid_spec,
        out_shape=jax.ShapeDtypeStruct((n_sorted, D_MODEL), F32),
        compiler_params=_cparams(("arbitrary", "arbitrary")),
        name="moe_experts",
    )(tile_experts, n_tiles, xs, w1, w3, w2)


def _combine_kernel(pos_ref, ys_ref, x_ref, mod_ref, o_ref, y_buf, sem, *, S, n_ctx):
    n = y_buf.shape[0]

    def issue(r, carry):
        pltpu.make_async_copy(ys_ref.at[pl.ds(pos_ref[r], 1)], y_buf.at[pl.ds(r, 1)], sem).start()
        return carry

    lax.fori_loop(0, n, issue, 0, unroll=8)
    _row_copy_wait(ys_ref, y_buf, sem, n)
    for r0 in range(0, n, TOK_TILE):
        t0 = pl.program_id(0) * n + r0
        b = t0 // S
        kind = jnp.where(t0 - b * S < n_ctx, 0, 1)
        rs = slice(r0, r0 + TOK_TILE)
        o_ref[rs, :] = x_ref[rs, :] + mod_ref[b, kind, 5:6, :] * y_buf[rs, :]


def _combine(pos, ys, xa, mod, S, n_ctx):
    T = xa.shape[0]
    tu = _flat_tile(T)
    assert tu <= S
    return pl.pallas_call(
        functools.partial(_combine_kernel, S=S, n_ctx=n_ctx),
        grid=(T // tu,),
        in_specs=[pl.BlockSpec((tu,), lambda i: (i,), memory_space=pltpu.SMEM),
                  pl.BlockSpec(memory_space=pl.ANY),
                  pl.BlockSpec((tu, D_MODEL), lambda i: (i, 0)),
                  pl.BlockSpec(mod.shape, lambda i: (0, 0, 0, 0))],
        out_specs=pl.BlockSpec((tu, D_MODEL), lambda i: (i, 0)),
        out_shape=jax.ShapeDtypeStruct((T, D_MODEL), F32),
        scratch_shapes=[pltpu.VMEM((tu, D_MODEL), F32), pltpu.SemaphoreType.DMA(())],
        compiler_params=_cparams(("arbitrary",)),
        name="moe_combine",
    )(pos, ys, xa, mod)


def _combine_latent_kernel(pos_ref, ys_ref, xa_ref, mod_ref, fw_ref, o_ref, y_buf, x_buf, sem, x_sem,
                           *, S, L, n_ctx):
    n = y_buf.shape[0]
    per_batch = L // n
    b = pl.program_id(0) // per_batch
    row0 = b * S + n_ctx + (pl.program_id(0) % per_batch) * n
    x_copy = pltpu.make_async_copy(xa_ref.at[pl.ds(row0, n)], x_buf, x_sem)
    x_copy.start()

    def issue(r, carry):
        pltpu.make_async_copy(ys_ref.at[pl.ds(pos_ref[r], 1)], y_buf.at[pl.ds(r, 1)], sem).start()
        return carry

    lax.fori_loop(0, n, issue, 0, unroll=8)
    x_copy.wait()
    _row_copy_wait(ys_ref, y_buf, sem, n)
    o_ref[...] = _rms(x_buf[...] + mod_ref[b, 1, 5:6, :] * y_buf[...], fw_ref[...])


def _combine_latent(pos, ys, xa, mod, final_w, B, S, L, n_ctx):
    tu = _flat_tile(L)
    pos_lat = pos.reshape(B, S)[:, n_ctx:].reshape(B * L)
    return pl.pallas_call(
        functools.partial(_combine_latent_kernel, S=S, L=L, n_ctx=n_ctx),
        grid=(B * L // tu,),
        in_specs=[pl.BlockSpec((tu,), lambda i: (i,), memory_space=pltpu.SMEM),
                  pl.BlockSpec(memory_space=pl.ANY),
                  pl.BlockSpec(memory_space=pl.ANY),
                  pl.BlockSpec(mod.shape, lambda i: (0, 0, 0, 0)),
                  pl.BlockSpec((1, D_MODEL), lambda i: (0, 0))],
        out_specs=pl.BlockSpec((tu, D_MODEL), lambda i: (i, 0)),
        out_shape=jax.ShapeDtypeStruct((B * L, D_MODEL), F32),
        scratch_shapes=[pltpu.VMEM((tu, D_MODEL), F32), pltpu.VMEM((tu, D_MODEL), F32),
                        pltpu.SemaphoreType.DMA(()), pltpu.SemaphoreType.DMA(())],
        compiler_params=_cparams(("arbitrary",)),
        name="moe_combine_latent",
    )(pos_lat, ys, xa, mod, final_w.reshape(1, -1))


def _flat_tile(T):
    return MOE_TILE if T % MOE_TILE == 0 else TOK_TILE


def _sorted_rows(T):
    return (-(-T // MOE_TILE) + N_CLASSES) * MOE_TILE


def _moe(rec, cls, rank, counts, xa, mod, w1, w3, w2, layer, final_w, sorted_init, B, S, n_ctx, last):
    T = B * S
    tm = MOE_TILE
    n_tiles_max = _sorted_rows(T) // tm
    cnt = counts[:N_CLASSES, 0].astype(jnp.int32)
    tiles = (cnt + tm - 1) // tm
    tile_end = jnp.cumsum(tiles)
    row0 = (tile_end - tiles) * tm
    cls = cls.reshape(T)
    pos = rank.reshape(T)
    for ci in range(N_CLASSES):
        pos = pos + jnp.where(cls == ci, row0[ci], 0)
    n_tiles = tile_end[N_CLASSES - 1:].astype(jnp.int32)
    tile_cls = jnp.sum(jnp.arange(n_tiles_max)[:, None] >= tile_end[None, :], axis=1)
    tile_cls = jnp.minimum(tile_cls, N_CLASSES - 1)
    grp0 = (tile_cls // PAIRS_PER_GROUP) * EXPERTS_PER_GROUP
    pair = tile_cls % PAIRS_PER_GROUP
    lo = sum(jnp.where(pair == p, PAIR_LO[p], 0) for p in range(PAIRS_PER_GROUP))
    hi = sum(jnp.where(pair == p, PAIR_HI[p], 0) for p in range(PAIRS_PER_GROUP))
    first_tile = jnp.sum(jnp.where(tile_cls[:, None] == jnp.arange(N_CLASSES)[None, :],
                                   (tile_end - tiles)[None, :], 0), axis=1)
    swap = (jnp.arange(n_tiles_max) - first_tile) % 2 == 1
    tile_experts = jnp.stack([grp0 + jnp.where(swap, hi, lo), grp0 + jnp.where(swap, lo, hi)],
                             axis=1).reshape(-1).astype(jnp.int32)
    xs = _dispatch(pos, rec.reshape(T, REC_W), sorted_init)
    ys = _experts(tile_experts, n_tiles, xs, w1, w3, w2, layer)
    if last:
        L = S - n_ctx
        out = _combine_latent(pos, ys, xa.reshape(T, D_MODEL), mod, final_w, B, S, L, n_ctx)
        return out.reshape(B, L, D_MODEL), xs
    out = _combine(pos, ys, xa.reshape(T, D_MODEL), mod, S, n_ctx)
    return out.reshape(B, S, D_MODEL), xs


def _rope_tables(L, n_ctx):
    t = jnp.arange(L)
    row = (t // GRID_W).astype(F32)
    col = (t % GRID_W).astype(F32)
    inv = ROPE_BASE ** (-jnp.arange(0, ROPE_AXIS_DIM, 2, dtype=F32) / ROPE_AXIS_DIM)
    ar = row[:, None] * inv
    ac = col[:, None] * inv
    cos_l = jnp.concatenate([jnp.cos(ar), jnp.cos(ar), jnp.cos(ac), jnp.cos(ac)], axis=1)
    sin_l = jnp.concatenate([-jnp.sin(ar), jnp.sin(ar), -jnp.sin(ac), jnp.sin(ac)], axis=1)
    cos_t = jnp.concatenate([jnp.ones((n_ctx, M_HEAD_DIM), F32), cos_l], axis=0)
    sin_t = jnp.concatenate([jnp.zeros((n_ctx, M_HEAD_DIM), F32), sin_l], axis=0)
    return cos_t, sin_t


def kernel(x, c, ctx, c_ctx, ada_w, ada_b, norm1_w, w_in, conv_w, conv_b, gate_b, mnorm_w, rpb, w_out,
           norm2_w, router_w, router_b, exp_w1, exp_w3, exp_w2, final_norm_w):
    B, L, _ = x.shape
    n_ctx = ctx.shape[1]
    depth = ada_w.shape[0]
    assert n_ctx == TOK_TILE and L % TOK_TILE == 0 and L % GRID_W == 0
    S = n_ctx + L
    rows = L // GRID_W
    assert rows >= NA_WROWS + 4 and rows % NA_QROWS == 0 and (S // 4) % HALO == 0

    cvec = jnp.concatenate([c, c_ctx[None], jnp.zeros((16 - B - 1, D_MODEL), F32)], axis=0)
    mods = _adaln(cvec, ada_w, ada_b)
    cos_t, sin_t = _rope_tables(L, n_ctx)
    router_wt = router_w.T
    typ, st, bias = _natten_tables(rpb.reshape((depth * N_HEADS,) + rpb.shape[2:]), rows, n_ctx)

    stream = (x, ctx)
    out = None
    sorted_buf = jnp.zeros((_sorted_rows(B * S), REC_W), F32)
    w1_all = exp_w1.reshape((depth * N_EXPERTS,) + exp_w1.shape[2:])
    w3_all = exp_w3.reshape((depth * N_EXPERTS,) + exp_w3.shape[2:])
    w2_all = exp_w2.reshape((depth * N_EXPERTS,) + exp_w2.shape[2:])
    for layer in range(depth):
        last = layer == depth - 1
        m6 = mods[layer].reshape(16, 6, D_MODEL)
        mod = jnp.stack([jnp.broadcast_to(m6[B], (B, 6, D_MODEL)), m6[:B]], axis=1)
        mq, mo, nq, mk, mv, nk, nv, g = _inproj(stream, mod, norm1_w[layer], w_in, layer, B, S)
        q, k = _prep(mq, mk, conv_w[layer], conv_b[layer], cos_t, sin_t, B, S)
        hf, hb = _mlstm(q, k, mv, g, gate_b[layer], B, S, n_ctx // CHUNK)
        n_out = _natten(nq, nk, nv, typ, st, bias, layer, B, S, n_ctx)
        xa, rec, cls, rank, counts = _outproj(stream, hf, hb, mo, n_out, mod, mnorm_w[layer], w_out, layer,
                                              norm2_w[layer], router_wt, router_b, B, S)
        out, sorted_buf = _moe(rec, cls, rank, counts, xa, mod, w1_all, w3_all, w2_all, layer,
                               final_norm_w, sorted_buf, B, S, n_ctx, last)
        stream = (out,)
    return out
```

```python
import functools

import numpy as np
import jax
import jax.numpy as jnp
from jax import lax
from jax.experimental import pallas as pl
from jax.experimental.pallas import tpu as pltpu

D_MODEL = 1024
M_WIDTH = 512
M_HEADS = 4
M_HEAD_DIM = 128
N_WIDTH = 512
N_HEADS = 8
N_HEAD_DIM = 64
G_COLS = 16
GRID_W = 64
WIN_H = 8
WIN_W = 16
CHUNK = 128
SCAN_CHUNKS = 2
ROPE_AXIS_DIM = 64
ROPE_BASE = 10000.0
N_EXPERTS = 16
N_GROUPS = 4
EXPERTS_PER_GROUP = 4
MOE_D_FF = 512
NORM_EPS = 1e-6

LANES = 128
TOK_TILE = 256
HALO = 16
NA_QROWS = 2
NA_WROWS = 10
NA_WKEYS = NA_WROWS * GRID_W
NEG = -1e30
G_PAD = LANES
N_PROJ = 7 * M_WIDTH + G_PAD
REC_W = D_MODEL + LANES
MOE_TILE = 1024
PAIRS_PER_GROUP = EXPERTS_PER_GROUP * (EXPERTS_PER_GROUP - 1) // 2
N_CLASSES = N_GROUPS * PAIRS_PER_GROUP
CLS_ROWS = 32
PAIR_LO = (0, 0, 0, 1, 1, 2)
PAIR_HI = (1, 2, 3, 2, 3, 3)
VMEM_LIMIT = 56 * 1024 * 1024

F32 = jnp.float32
BF16 = jnp.bfloat16


def _cparams(sem):
    return pltpu.CompilerParams(dimension_semantics=sem, vmem_limit_bytes=VMEM_LIMIT)


def _split_bf16(a):
    hi = a.astype(BF16)
    lo = (a - hi.astype(F32)).astype(BF16)
    return hi, lo


def _dot3(a, b):
    ah, al = _split_bf16(a)
    bh, bl = _split_bf16(b)
    d = functools.partial(jnp.dot, preferred_element_type=F32)
    return d(ah, bh) + (d(al, bh) + d(ah, bl))


def _dot_nt(a, b):
    return lax.dot_general(a, b, (((1,), (1,)), ((), ())), preferred_element_type=F32)


def _rms(x, w):
    ms = jnp.mean(x * x, axis=-1, keepdims=True)
    return x * lax.rsqrt(ms + NORM_EPS) * w


def _adaln_kernel(c_ref, w_ref, b_ref, o_ref):
    c = c_ref[...]
    s = c * jax.nn.sigmoid(c)
    o_ref[0] = _dot3(s, w_ref[0]) + b_ref[0]


def _adaln(cvec, ada_w, ada_b):
    depth = ada_w.shape[0]
    n = ada_w.shape[2]
    tn = D_MODEL
    return pl.pallas_call(
        _adaln_kernel,
        grid=(depth, n // tn),
        in_specs=[pl.BlockSpec((16, D_MODEL), lambda l, j: (0, 0)),
                  pl.BlockSpec((1, D_MODEL, tn), lambda l, j: (l, 0, j)),
                  pl.BlockSpec((1, 1, tn), lambda l, j: (l, 0, j))],
        out_specs=pl.BlockSpec((1, 16, tn), lambda l, j: (l, 0, j)),
        out_shape=jax.ShapeDtypeStruct((depth, 16, n), F32),
        compiler_params=_cparams(("arbitrary", "arbitrary")),
        name="adaln",
    )(cvec, ada_w, ada_b.reshape(depth, 1, n))


def _inproj_kernel(*refs, split_input):
    if split_input:
        x_ref, ctx_ref, mod_ref, nw_ref, win_ref = refs[:5]
        outs = refs[5:13]
        xt = jnp.where(pl.program_id(1) == 0, ctx_ref[0], x_ref[0])
    else:
        x_ref, mod_ref, nw_ref, win_ref = refs[:4]
        outs = refs[4:12]
        xt = x_ref[0]
    w_ref = refs[-1]

    @pl.when(jnp.logical_and(pl.program_id(0) == 0, pl.program_id(1) == 0))
    def _():
        g0 = 5 * M_WIDTH
        n0 = g0 + G_COLS
        for k in range(5):
            cs = slice(k * M_WIDTH, (k + 1) * M_WIDTH)
            w_ref[:, cs] = win_ref[0, :, cs].astype(BF16)
        for k in range(2):
            w_ref[:, g0 + k * M_WIDTH:g0 + (k + 1) * M_WIDTH] = (
                win_ref[0, :, n0 + k * M_WIDTH:n0 + (k + 1) * M_WIDTH].astype(BF16))
        w_ref[:, 7 * M_WIDTH:] = win_ref[0, :, g0:g0 + G_PAD].astype(BF16)

    h = _rms(xt, nw_ref[...]) * (1.0 + mod_ref[0, 0, 1:2, :]) + mod_ref[0, 0, 0:1, :]
    hb = h.astype(BF16)
    for k in range(7):
        outs[k][0] = jnp.dot(hb, w_ref[:, k * M_WIDTH:(k + 1) * M_WIDTH],
                             preferred_element_type=F32).astype(BF16)
    outs[7][0] = jnp.dot(hb, w_ref[:, 7 * M_WIDTH:], preferred_element_type=F32)


def _stream_specs(split_input):
    tm = TOK_TILE
    if split_input:
        return [pl.BlockSpec((1, tm, D_MODEL), lambda b, j: (b, jnp.maximum(j - 1, 0), 0)),
                pl.BlockSpec((1, tm, D_MODEL), lambda b, j: (b, 0, 0))]
    return [pl.BlockSpec((1, tm, D_MODEL), lambda b, j: (b, j, 0))]


def _mod_spec():
    return pl.BlockSpec((1, 1, 6, D_MODEL), lambda b, j: (b, jnp.minimum(j, 1), 0, 0))


def _inproj(stream, mod, norm_w, w_in, layer, B, S):
    split_input = len(stream) == 2
    tm = TOK_TILE
    tok = lambda w: pl.BlockSpec((1, tm, w), lambda b, j: (b, j, 0))
    out_shape = [jax.ShapeDtypeStruct((B, S, M_WIDTH), BF16)] * 7 + [jax.ShapeDtypeStruct((B, S, G_PAD), F32)]
    return pl.pallas_call(
        functools.partial(_inproj_kernel, split_input=split_input),
        grid=(B, S // tm),
        in_specs=_stream_specs(split_input) + [
            _mod_spec(),
            pl.BlockSpec((1, D_MODEL), lambda b, j: (0, 0)),
            pl.BlockSpec((1,) + w_in.shape[1:], lambda b, j: (layer, 0, 0), pipeline_mode=pl.Buffered(1))],
        out_specs=[tok(M_WIDTH)] * 7 + [tok(G_PAD)],
        out_shape=out_shape,
        scratch_shapes=[pltpu.VMEM((D_MODEL, N_PROJ), BF16)],
        compiler_params=_cparams(("arbitrary", "arbitrary")),
        name="inproj",
    )(*stream, mod, norm_w.reshape(1, D_MODEL), w_in)


def _prep_kernel(q_ref, qp_ref, qn_ref, k_ref, kp_ref, kn_ref, cw_ref, cb_ref, cos_ref, sin_ref,
                 qo_ref, ko_ref, *, n_tiles):
    j = pl.program_id(1)
    tp = TOK_TILE
    has_prev = jnp.logical_and(j != 0, j != 1).astype(F32)
    has_next = jnp.logical_and(j != 0, j != n_tiles - 1).astype(F32)
    rows = lax.broadcasted_iota(jnp.int32, (tp, M_WIDTH), 0)
    lanes = lax.broadcasted_iota(jnp.int32, (tp, M_WIDTH), 1)
    low_half = (lanes % (ROPE_AXIS_DIM)) < (ROPE_AXIS_DIM // 2)
    cosv = jnp.concatenate([cos_ref[...]] * M_HEADS, axis=1)
    sinv = jnp.concatenate([sin_ref[...]] * M_HEADS, axis=1)

    def branch(x_ref, p_ref, n_ref, col0):
        x = x_ref[0].astype(F32)
        prev_row = p_ref[0, HALO - 1:HALO, :].astype(F32) * has_prev
        next_row = n_ref[0, 0:1, :].astype(F32) * has_next
        xm = jnp.where(rows == 0, prev_row, pltpu.roll(x, 1, 0))
        xp = jnp.where(rows == tp - 1, next_row, pltpu.roll(x, tp - 1, 0))
        w = cw_ref[:, col0:col0 + M_WIDTH]
        y = cb_ref[:, col0:col0 + M_WIDTH] + xm * w[0:1] + x * w[1:2] + xp * w[2:3]
        y = y * jax.nn.sigmoid(y)
        half = ROPE_AXIS_DIM // 2
        partner = jnp.where(low_half, pltpu.roll(y, M_WIDTH - half, 1), pltpu.roll(y, half, 1))
        return y * cosv + partner * sinv

    qo_ref[0] = branch(q_ref, qp_ref, qn_ref, 0).astype(BF16)
    ko_ref[0] = (branch(k_ref, kp_ref, kn_ref, M_WIDTH) * (M_HEAD_DIM ** -0.5)).T.astype(BF16)


def _prep(mq, mk, conv_w, conv_b, cos_t, sin_t, B, S):
    tp = TOK_TILE
    n_tiles = S // tp
    per = tp // HALO
    n_halo = S // HALO
    main = pl.BlockSpec((1, tp, M_WIDTH), lambda b, j: (b, j, 0))
    prev = pl.BlockSpec((1, HALO, M_WIDTH), lambda b, j: (b, jnp.maximum(j * per - 1, 0), 0))
    nxt = pl.BlockSpec((1, HALO, M_WIDTH), lambda b, j: (b, jnp.minimum((j + 1) * per, n_halo - 1), 0))
    return pl.pallas_call(
        functools.partial(_prep_kernel, n_tiles=n_tiles),
        grid=(B, n_tiles),
        in_specs=[main, prev, nxt, main, prev, nxt,
                  pl.BlockSpec((3, 2 * M_WIDTH), lambda b, j: (0, 0)),
                  pl.BlockSpec((1, 2 * M_WIDTH), lambda b, j: (0, 0)),
                  pl.BlockSpec((tp, M_HEAD_DIM), lambda b, j: (j, 0)),
                  pl.BlockSpec((tp, M_HEAD_DIM), lambda b, j: (j, 0))],
        out_specs=[main, pl.BlockSpec((1, M_WIDTH, tp), lambda b, j: (b, 0, j))],
        out_shape=[jax.ShapeDtypeStruct((B, S, M_WIDTH), BF16), jax.ShapeDtypeStruct((B, M_WIDTH, S), BF16)],
        compiler_params=_cparams(("arbitrary", "arbitrary")),
        name="mlstm_prep",
    )(mq, mq, mq, mk, mk, mk, conv_w, conv_b.reshape(1, -1), cos_t, sin_t)


def _scan_rows(x, reverse, op, fill):
    n = x.shape[0]
    rows = lax.broadcasted_iota(jnp.int32, x.shape, 0)
    sh = 1
    while sh < n:
        if reverse:
            x = op(x, jnp.where(rows < n - sh, pltpu.roll(x, n - sh, 0), fill))
        else:
            x = op(x, jnp.where(rows >= sh, pltpu.roll(x, sh, 0), fill))
        sh *= 2
    return x


def _log_sigmoid(x):
    return jnp.minimum(x, 0.0) - jnp.log(1.0 + jnp.exp(-jnp.abs(x)))


def _mlstm_kernel(qf_ref, kf_ref, vf_ref, gf_ref, qb_ref, kb_ref, vb_ref, gb_ref, gbias_ref,
                  hf_ref, hb_ref, cn_ref, m_ref):
    c = pl.program_id(1)

    @pl.when(c == 0)
    def _():
        cn_ref[...] = jnp.zeros_like(cn_ref)
        m_ref[...] = jnp.zeros_like(m_ref)

    t = CHUNK
    r_i = lax.broadcasted_iota(jnp.int32, (t, t), 0)
    c_i = lax.broadcasted_iota(jnp.int32, (t, t), 1)
    ones_blk = jnp.ones((t, LANES), BF16)

    dirs = ((qf_ref, kf_ref, vf_ref, gf_ref, hf_ref), (qb_ref, kb_ref, vb_ref, gb_ref, hb_ref))
    for step, d in ((s, d) for s in range(SCAN_CHUNKS) for d in range(2)):
        q_ref, k_ref, v_ref, g_ref, h_ref = dirs[d]
        reverse = d == 1
        ts = pl.ds((SCAN_CHUNKS - 1 - step if reverse else step) * t, t)
        mask = (c_i >= r_i) if reverse else (c_i <= r_i)
        end_row = 0 if reverse else t - 1
        g = g_ref[0, ts, :] + gbias_ref[...]
        bc = _scan_rows(_log_sigmoid(g), reverse, jnp.add, 0.0)
        b_al = pltpu.roll(bc, LANES - M_HEADS, 1)
        a = g - b_al
        m_old = m_ref[d][0:1, :]
        mx = jnp.maximum(m_old, _scan_rows(a, reverse, jnp.maximum, NEG))
        m_end = mx[end_row:end_row + 1, :]
        c_t = -mx
        e_den = jnp.exp(-(b_al + mx))
        w_old = jnp.exp(m_old - m_end)
        a_t = a.T
        w_tok_t = jnp.exp(a - m_end).T
        m_ref[d] = jnp.broadcast_to(b_al[end_row:end_row + 1, :] + m_end, (8, LANES))
        for h in range(M_HEADS):
            icol = 2 * M_HEADS * d + h
            hs = slice(h * M_HEAD_DIM, (h + 1) * M_HEAD_DIM)
            q = q_ref[0, ts, hs]
            k_t = k_ref[0, hs, ts]
            v_ext = jnp.concatenate([v_ref[0, ts, hs], ones_blk], axis=1)
            cn_old = cn_ref[d, h]

            c_b = jnp.broadcast_to(c_t[:, icol:icol + 1], (t, t))
            expo = jnp.concatenate([jnp.where(mask, c_b + a_t[icol:icol + 1, :], NEG),
                                    c_b + m_old[:, icol:icol + 1]], axis=1)
            qk = jnp.dot(q, k_t, preferred_element_type=F32)
            s_ext = (jnp.concatenate([qk, q.astype(F32)], axis=1) * jnp.exp(expo)).astype(BF16)
            rhs = jnp.concatenate([v_ext, cn_old.astype(BF16)], axis=0)
            ext = jnp.dot(s_ext, rhs, preferred_element_type=F32)
            den = jnp.maximum(jnp.abs(ext[:, M_HEAD_DIM:]),
                              jnp.broadcast_to(e_den[:, icol:icol + 1], (t, LANES)))
            h_ref[0, ts, hs] = (ext[:, :M_HEAD_DIM] / den).astype(BF16)

            kw_t = (k_t.astype(F32) * w_tok_t[icol:icol + 1, :]).astype(BF16)
            cn_ref[d, h] = (w_old[:, icol:icol + 1] * cn_old
                            + jnp.dot(kw_t, v_ext, preferred_element_type=F32))


def _mlstm(q, k, v, g, gate_b, B, S, n_ctx_chunks):
    rows = SCAN_CHUNKS * CHUNK
    assert n_ctx_chunks % SCAN_CHUNKS == 0 and S % rows == 0
    nc = S // rows
    ncx = n_ctx_chunks // SCAN_CHUNKS

    def bwd_chunk(c):
        return jnp.where(c < ncx, ncx - 1 - c, nc - 1 + ncx - c)

    fw = lambda w: pl.BlockSpec((1, rows, w), lambda b, c: (b, c, 0))
    bw = lambda w: pl.BlockSpec((1, rows, w), lambda b, c: (b, bwd_chunk(c), 0))
    fw_t = pl.BlockSpec((1, M_WIDTH, rows), lambda b, c: (b, 0, c))
    bw_t = pl.BlockSpec((1, M_WIDTH, rows), lambda b, c: (b, 0, bwd_chunk(c)))
    gbias = jnp.zeros((1, G_PAD), F32).at[0, :G_COLS].set(gate_b)
    return pl.pallas_call(
        _mlstm_kernel,
        grid=(B, nc),
        in_specs=[fw(M_WIDTH), fw_t, fw(M_WIDTH), fw(G_PAD),
                  bw(M_WIDTH), bw_t, bw(M_WIDTH), bw(G_PAD),
                  pl.BlockSpec((1, G_PAD), lambda b, c: (0, 0))],
        out_specs=[fw(M_WIDTH), bw(M_WIDTH)],
        out_shape=[jax.ShapeDtypeStruct((B, S, M_WIDTH), BF16)] * 2,
        scratch_shapes=[pltpu.VMEM((2, M_HEADS, M_HEAD_DIM, 2 * LANES), F32),
                        pltpu.VMEM((2, 8, LANES), F32)],
        compiler_params=_cparams(("arbitrary", "arbitrary")),
        name="mlstm_scan",
    )(q, k, v, g, q, k, v, g, gbias)


def _natten_kernel(typ_ref, st_ref, q_ref, k_ref, v_ref, bias_ref, o_ref, *, n_ctx):
    sub = NA_QROWS * GRID_W
    lane = lax.broadcasted_iota(jnp.int32, (sub, LANES), 1)
    kctx = k_ref[0, 0:n_ctx, :]
    vctx = v_ref[0, 0:n_ctx, :]

    def sub_block(sb, carry):
        typ = typ_ref[sb]
        st = pl.multiple_of(st_ref[sb], GRID_W)
        rows = pl.ds(pl.multiple_of(sb * sub, sub), sub)
        q = q_ref[0, rows, :] * (N_HEAD_DIM ** -0.5)
        kwin = k_ref[0, pl.ds(st, NA_WKEYS), :]
        vwin = v_ref[0, pl.ds(st, NA_WKEYS), :]
        first = lane < N_HEAD_DIM
        zero = jnp.zeros_like(q)
        q2 = jnp.concatenate([jnp.where(first, q, zero), jnp.where(first, zero, q)], axis=0)
        s_win = _dot_nt(q2, kwin) + bias_ref[typ].reshape(2 * sub, NA_WKEYS)
        s_ctx = _dot_nt(q2, kctx)
        m = jnp.maximum(jnp.max(s_win, axis=-1, keepdims=True), jnp.max(s_ctx, axis=-1, keepdims=True))
        p_win = jnp.exp(s_win - m)
        p_ctx = jnp.exp(s_ctx - m)
        l = jnp.sum(p_win, axis=-1, keepdims=True) + jnp.sum(p_ctx, axis=-1, keepdims=True)
        o = (jnp.dot(p_win.astype(BF16), vwin, preferred_element_type=F32)
             + jnp.dot(p_ctx.astype(BF16), vctx, preferred_element_type=F32)) / l
        o_ref[0, rows, :] = jnp.where(first, o[:sub], o[sub:]).astype(BF16)
        return carry

    lax.fori_loop(0, q_ref.shape[1] // sub, sub_block, 0, unroll=17)


def _natten_tables(rpb, rows, n_ctx):
    n_heads = rpb.shape[0]
    n_sb = rows // NA_QROWS
    ws_of = lambda sb: int(np.clip(NA_QROWS * sb - WIN_H // 2, 0, rows - NA_WROWS))
    rs_of = lambda r: int(np.clip(r - WIN_H // 2, 0, rows - WIN_H))
    shape_of = lambda sb: tuple((NA_QROWS * sb + qr - ws_of(sb), rs_of(NA_QROWS * sb + qr) - ws_of(sb))
                                for qr in range(NA_QROWS))
    shapes = sorted(set(shape_of(sb) for sb in range(n_sb)))
    reps = [next(sb for sb in range(n_sb) if shape_of(sb) == sh) for sh in shapes]
    n_dr = 2 * WIN_H - 1
    n_dc = 2 * WIN_W - 1
    lead = GRID_W - WIN_W
    vec = jnp.pad(rpb, ((0, 0), (0, 0), (lead, 2 * GRID_W - lead - n_dc)), constant_values=NEG)
    skew = jnp.broadcast_to(vec[:, :, None, :], (n_heads, n_dr, GRID_W, 2 * GRID_W))
    skew = skew.reshape(n_heads, n_dr, -1)[:, :, :GRID_W * (2 * GRID_W - 1)]
    toep = skew.reshape(n_heads, n_dr, GRID_W, 2 * GRID_W - 1)[..., GRID_W - 1:]
    cq = np.arange(GRID_W)[:, None]
    ck = np.arange(GRID_W)[None, :]
    cs = np.clip(cq - WIN_W // 2, 0, GRID_W - WIN_W)
    toep = jnp.where((ck >= cs) & (ck < cs + WIN_W), toep, NEG)
    masked_tile = jnp.full((n_heads, GRID_W, GRID_W), NEG, F32)
    types = []
    for sb in reps:
        ws = ws_of(sb)
        q_rows = []
        for qr in range(NA_QROWS):
            r = NA_QROWS * sb + qr
            rs = rs_of(r)
            tiles = []
            for kw in range(NA_WROWS):
                kr = ws + kw
                tiles.append(toep[:, kr - r + WIN_H - 1] if rs <= kr < rs + WIN_H else masked_tile)
            q_rows.append(jnp.concatenate(tiles, axis=-1))
        types.append(jnp.concatenate(q_rows, axis=-2))
    types.append(jnp.full_like(types[0], NEG))
    bias = jnp.stack(types, axis=0)
    n_ctx_sb = n_ctx // (NA_QROWS * GRID_W)
    typ, st = [len(shapes)] * n_ctx_sb, [n_ctx] * n_ctx_sb
    for sb in range(n_sb):
        typ.append(shapes.index(shape_of(sb)))
        st.append(n_ctx + ws_of(sb) * GRID_W)
    return jnp.asarray(typ, jnp.int32), jnp.asarray(st, jnp.int32), bias


def _natten(nq, nk, nv, typ, st, bias, layer, B, S, n_ctx):
    n_pairs = N_HEADS // 2
    grid_spec = pltpu.PrefetchScalarGridSpec(
        num_scalar_prefetch=2,
        grid=(n_pairs, B),
        in_specs=[pl.BlockSpec((1, S, LANES), lambda p, b, *_: (b, 0, p)),
                  pl.BlockSpec((1, S, LANES), lambda p, b, *_: (b, 0, p)),
                  pl.BlockSpec((1, S, LANES), lambda p, b, *_: (b, 0, p)),
                  pl.BlockSpec((bias.shape[0], 2, NA_QROWS * GRID_W, NA_WKEYS),
                               lambda p, b, *_: (0, layer * n_pairs + p, 0, 0))],
        out_specs=pl.BlockSpec((1, S, LANES), lambda p, b, *_: (b, 0, p)),
    )
    return pl.pallas_call(
        functools.partial(_natten_kernel, n_ctx=n_ctx),
        grid_spec=grid_spec,
        out_shape=jax.ShapeDtypeStruct((B, S, N_WIDTH), BF16),
        compiler_params=_cparams(("arbitrary", "arbitrary")),
        name="natten",
    )(typ, st, nq, nk, nv, bias)


def _route(logits_t, rb):
    e_i = lax.broadcasted_iota(jnp.int32, logits_t.shape, 0)
    z = logits_t - jnp.max(logits_t, axis=0, keepdims=True)
    ez = jnp.exp(z)
    scores = ez / jnp.sum(ez, axis=0, keepdims=True)
    sel = scores + rb
    best = None
    best_score = None
    for gi in range(N_GROUPS):
        r = [sel[gi * EXPERTS_PER_GROUP + u:gi * EXPERTS_PER_GROUP + u + 1, :] for u in range(EXPERTS_PER_GROUP)]
        gs = None
        for u in range(EXPERTS_PER_GROUP):
            for w in range(u + 1, EXPERTS_PER_GROUP):
                pair = r[u] + r[w]
                gs = pair if gs is None else jnp.maximum(gs, pair)
        if best is None:
            best, best_score = jnp.zeros(gs.shape, jnp.int32), gs
        else:
            better = gs > best_score
            best = jnp.where(better, gi, best)
            best_score = jnp.where(better, gs, best_score)
    masked = jnp.where((e_i // EXPERTS_PER_GROUP) == best, sel, -jnp.inf)
    v1 = jnp.max(masked, axis=0, keepdims=True)
    i1 = jnp.min(jnp.where(masked == v1, e_i, N_EXPERTS), axis=0, keepdims=True)
    masked2 = jnp.where(e_i == i1, -jnp.inf, masked)
    v2 = jnp.max(masked2, axis=0, keepdims=True)
    i2 = jnp.min(jnp.where(masked2 == v2, e_i, N_EXPERTS), axis=0, keepdims=True)
    w1 = jnp.sum(jnp.where(e_i == i1, scores, 0.0), axis=0, keepdims=True)
    w2 = jnp.sum(jnp.where(e_i == i2, scores, 0.0), axis=0, keepdims=True)
    tot = w1 + w2
    gate = jnp.where(e_i == i1, w1 / tot, 0.0) + jnp.where(e_i == i2, w2 / tot, 0.0)
    lo = jnp.minimum(i1, i2) - best * EXPERTS_PER_GROUP
    hi = jnp.maximum(i1, i2) - best * EXPERTS_PER_GROUP
    pair = ((lo * (2 * EXPERTS_PER_GROUP - 1 - lo)) >> 1) + (hi - lo - 1)
    return gate, best * PAIRS_PER_GROUP + pair


def _outproj_kernel(*refs, split_input):
    n_stream = 2 if split_input else 1
    stream = refs[:n_stream]
    (hf_ref, hb_ref, mo_ref, no_ref, mod_ref, mnw_ref, wo_in_ref, n2w_ref, rw_ref, rb_ref,
     xo_ref, rec_ref, grp_ref, rank_ref, cnt_ref, base_ref, wo_ref) = refs[n_stream:]

    @pl.when(jnp.logical_and(pl.program_id(0) == 0, pl.program_id(1) == 0))
    def _():
        base_ref[...] = jnp.zeros_like(base_ref)
        wo_ref[...] = wo_in_ref[0].astype(BF16)

    rh, rl = _split_bf16(rw_ref[...])
    n_rows = xo_ref.shape[1]
    half = n_rows // 2
    logits = []
    for r0 in (0, half):
        rs = slice(r0, r0 + half)
        if split_input:
            xt = jnp.where(pl.program_id(1) == 0, stream[1][0, rs], stream[0][0, rs])
        else:
            xt = stream[0][0, rs]
        hsum = hf_ref[0, rs].astype(F32) + hb_ref[0, rs].astype(F32)
        parts = []
        for h in range(M_HEADS):
            hs = slice(h * M_HEAD_DIM, (h + 1) * M_HEAD_DIM)
            parts.append(_rms(hsum[:, hs], mnw_ref[:, hs]))
        m_out = jnp.concatenate(parts, axis=1) * jax.nn.sigmoid(mo_ref[0, rs].astype(F32))
        y = (jnp.dot(m_out.astype(BF16), wo_ref[0:M_WIDTH, :], preferred_element_type=F32)
             + jnp.dot(no_ref[0, rs], wo_ref[M_WIDTH:, :], preferred_element_type=F32))
        x_new = xt + mod_ref[0, 0, 2:3, :] * y
        xo_ref[0, rs] = x_new
        hx = _rms(x_new, n2w_ref[...]) * (1.0 + mod_ref[0, 0, 4:5, :]) + mod_ref[0, 0, 3:4, :]
        rec_ref[0, rs, :D_MODEL] = hx
        hh, hl = _split_bf16(hx)
        logits.append(_dot_nt(rh, hh) + (_dot_nt(rh, hl) + _dot_nt(rl, hh)))
    logits_t = jnp.concatenate(logits, axis=1)
    gate_t, cls = _route(logits_t, rb_ref[...])
    pad = jnp.zeros((LANES - N_EXPERTS, gate_t.shape[1]), F32)
    rec_ref[0, :, D_MODEL:] = jnp.concatenate([gate_t, pad], axis=0).T
    tm = cls.shape[1]
    onehot = lax.broadcasted_iota(jnp.int32, (CLS_ROWS, tm), 0) == cls
    upper = (lax.broadcasted_iota(jnp.int32, (tm, tm), 0) <= lax.broadcasted_iota(jnp.int32, (tm, tm), 1))
    csum = jnp.dot(onehot.astype(BF16), upper.astype(BF16), preferred_element_type=F32)
    base = base_ref[:, 0:1]
    rank = jnp.sum(jnp.where(onehot, csum - 1.0 + base, 0.0), axis=0, keepdims=True)
    grp_ref[0, 0] = cls
    rank_ref[0, 0] = rank.astype(jnp.int32)
    total = base_ref[...] + csum[:, tm - 1:tm]
    base_ref[...] = total
    cnt_ref[...] = total


def _outproj(stream, hf, hb, mo, n_out, mod, mnorm_w, w_out, layer, norm2_w, router_wt, router_b, B, S):
    split_input = len(stream) == 2
    tm = TOK_TILE
    nt = S // tm
    tok = lambda w: pl.BlockSpec((1, tm, w), lambda b, j: (b, j, 0))
    idx = pl.BlockSpec((1, 1, 1, tm), lambda b, j: (b, j, 0, 0))
    const = lambda shape: pl.BlockSpec(shape, lambda b, j: (0,) * len(shape))
    return pl.pallas_call(
        functools.partial(_outproj_kernel, split_input=split_input),
        grid=(B, nt),
        in_specs=_stream_specs(split_input) + [
            tok(M_WIDTH), tok(M_WIDTH), tok(M_WIDTH), tok(N_WIDTH), _mod_spec(),
            const((1, M_WIDTH)),
            pl.BlockSpec((1, D_MODEL, D_MODEL), lambda b, j: (layer, 0, 0), pipeline_mode=pl.Buffered(1)),
            const((1, D_MODEL)), const((N_EXPERTS, D_MODEL)), const((N_EXPERTS, 1))],
        out_specs=[tok(D_MODEL), tok(REC_W), idx, idx, const((CLS_ROWS, LANES))],
        out_shape=[jax.ShapeDtypeStruct((B, S, D_MODEL), F32),
                   jax.ShapeDtypeStruct((B, S, REC_W), F32),
                   jax.ShapeDtypeStruct((B, nt, 1, tm), jnp.int32),
                   jax.ShapeDtypeStruct((B, nt, 1, tm), jnp.int32),
                   jax.ShapeDtypeStruct((CLS_ROWS, LANES), F32)],
        scratch_shapes=[pltpu.VMEM((CLS_ROWS, LANES), F32), pltpu.VMEM((D_MODEL, D_MODEL), BF16)],
        compiler_params=_cparams(("arbitrary", "arbitrary")),
        name="outproj_router",
    )(*stream, hf, hb, mo, n_out, mod, mnorm_w.reshape(1, -1), w_out, norm2_w.reshape(1, -1),
      router_wt, router_b.reshape(-1, 1))


def _row_copy_wait(src, dst, sem, n_rows):
    pltpu.make_async_copy(src.at[pl.ds(0, n_rows)], dst.at[pl.ds(0, n_rows)], sem).wait()


def _dispatch_kernel(pos_ref, rec_ref, init_ref, xs_ref, sem):
    del init_ref
    n = rec_ref.shape[0]

    def issue(r, carry):
        pltpu.make_async_copy(rec_ref.at[pl.ds(r, 1)], xs_ref.at[pl.ds(pos_ref[r], 1)], sem).start()
        return carry

    lax.fori_loop(0, n, issue, 0, unroll=8)
    _row_copy_wait(rec_ref, xs_ref, sem, n)


def _dispatch(pos, rec, init):
    T, w = rec.shape
    n_sorted = init.shape[0]
    tp = _flat_tile(T)
    return pl.pallas_call(
        _dispatch_kernel,
        grid=(T // tp,),
        in_specs=[pl.BlockSpec((tp,), lambda i: (i,), memory_space=pltpu.SMEM),
                  pl.BlockSpec((tp, w), lambda i: (i, 0)),
                  pl.BlockSpec(memory_space=pl.ANY)],
        out_specs=pl.BlockSpec(memory_space=pl.ANY),
        out_shape=jax.ShapeDtypeStruct((n_sorted, w), F32),
        scratch_shapes=[pltpu.SemaphoreType.DMA(())],
        input_output_aliases={2: 0},
        compiler_params=_cparams(("arbitrary",)),
        name="moe_dispatch",
    )(pos, rec, init)


def _experts_kernel(te_ref, nt_ref, xs_ref, w1_ref, w3_ref, w2_ref, ys_ref):
    i = pl.program_id(0)
    k = pl.program_id(1)
    valid = i < nt_ref[0]

    def contribution():
        e = te_ref[2 * i + k]
        h = xs_ref[:, :D_MODEL].astype(BF16)
        gate = xs_ref[:, D_MODEL:]
        lane = lax.broadcasted_iota(jnp.int32, gate.shape, 1)
        g_e = jnp.sum(jnp.where(lane == e, gate, 0.0), axis=-1, keepdims=True)
        a = jnp.dot(h, w1_ref[0].astype(BF16), preferred_element_type=F32)
        b3 = jnp.dot(h, w3_ref[0].astype(BF16), preferred_element_type=F32)
        act = (a * jax.nn.sigmoid(a) * b3 * g_e).astype(BF16)
        return jnp.dot(act, w2_ref[0].astype(BF16), preferred_element_type=F32)

    @pl.when(jnp.logical_and(valid, k == 0))
    def _():
        ys_ref[...] = contribution()

    @pl.when(jnp.logical_and(valid, k == 1))
    def _():
        ys_ref[...] += contribution()

    @pl.when(jnp.logical_and(jnp.logical_not(valid), k == 0))
    def _():
        ys_ref[...] = jnp.zeros_like(ys_ref)


def _experts(tile_experts, n_tiles, xs, w1, w3, w2, layer):
    n_sorted, w = xs.shape
    tm = MOE_TILE
    expert = lambda i, k, te, nt: (
        layer * N_EXPERTS + te[2 * jnp.minimum(i, nt[0] - 1) + jnp.where(i < nt[0], k, 1)], 0, 0)
    rows = lambda i, k, te, nt: (jnp.minimum(i, nt[0] - 1), 0)
    grid_spec = pltpu.PrefetchScalarGridSpec(
        num_scalar_prefetch=2,
        grid=(n_sorted // tm, 2),
        in_specs=[pl.BlockSpec((tm, w), rows),
                  pl.BlockSpec((1, D_MODEL, MOE_D_FF), expert),
                  pl.BlockSpec((1, D_MODEL, MOE_D_FF), expert),
                  pl.BlockSpec((1, MOE_D_FF, D_MODEL), expert)],
        out_specs=pl.BlockSpec((tm, D_MODEL), lambda i, k, *_: (i, 0)),
    )
    return pl.pallas_call(
        _experts_kernel,
        grid_spec=grid_spec,
        out_shape=jax.ShapeDtypeStruct((n_sorted, D_MODEL), F32),
        compiler_params=_cparams(("arbitrary", "arbitrary")),
        name="moe_experts",
    )(tile_experts, n_tiles, xs, w1, w3, w2)


def _combine_kernel(pos_ref, ys_ref, x_ref, mod_ref, o_ref, y_buf, sem, *, S, n_ctx):
    n = y_buf.shape[0]

    def issue(r, carry):
        pltpu.make_async_copy(ys_ref.at[pl.ds(pos_ref[r], 1)], y_buf.at[pl.ds(r, 1)], sem).start()
        return carry

    lax.fori_loop(0, n, issue, 0, unroll=8)
    _row_copy_wait(ys_ref, y_buf, sem, n)
    for r0 in range(0, n, TOK_TILE):
        t0 = pl.program_id(0) * n + r0
        b = t0 // S
        kind = jnp.where(t0 - b * S < n_ctx, 0, 1)
        rs = slice(r0, r0 + TOK_TILE)
        o_ref[rs, :] = x_ref[rs, :] + mod_ref[b, kind, 5:6, :] * y_buf[rs, :]


def _combine(pos, ys, xa, mod, S, n_ctx):
    T = xa.shape[0]
    tu = _flat_tile(T)
    assert tu <= S
    return pl.pallas_call(
        functools.partial(_combine_kernel, S=S, n_ctx=n_ctx),
        grid=(T // tu,),
        in_specs=[pl.BlockSpec((tu,), lambda i: (i,), memory_space=pltpu.SMEM),
                  pl.BlockSpec(memory_space=pl.ANY),
                  pl.BlockSpec((tu, D_MODEL), lambda i: (i, 0)),
                  pl.BlockSpec(mod.shape, lambda i: (0, 0, 0, 0))],
        out_specs=pl.BlockSpec((tu, D_MODEL), lambda i: (i, 0)),
        out_shape=jax.ShapeDtypeStruct((T, D_MODEL), F32),
        scratch_shapes=[pltpu.VMEM((tu, D_MODEL), F32), pltpu.SemaphoreType.DMA(())],
        compiler_params=_cparams(("arbitrary",)),
        name="moe_combine",
    )(pos, ys, xa, mod)


def _combine_latent_kernel(pos_ref, ys_ref, xa_ref, mod_ref, fw_ref, o_ref, y_buf, x_buf, sem, x_sem,
                           *, S, L, n_ctx):
    n = y_buf.shape[0]
    per_batch = L // n
    b = pl.program_id(0) // per_batch
    row0 = b * S + n_ctx + (pl.program_id(0) % per_batch) * n
    x_copy = pltpu.make_async_copy(xa_ref.at[pl.ds(row0, n)], x_buf, x_sem)
    x_copy.start()

    def issue(r, carry):
        pltpu.make_async_copy(ys_ref.at[pl.ds(pos_ref[r], 1)], y_buf.at[pl.ds(r, 1)], sem).start()
        return carry

    lax.fori_loop(0, n, issue, 0, unroll=8)
    x_copy.wait()
    _row_copy_wait(ys_ref, y_buf, sem, n)
    o_ref[...] = _rms(x_buf[...] + mod_ref[b, 1, 5:6, :] * y_buf[...], fw_ref[...])


def _combine_latent(pos, ys, xa, mod, final_w, B, S, L, n_ctx):
    tu = _flat_tile(L)
    pos_lat = pos.reshape(B, S)[:, n_ctx:].reshape(B * L)
    return pl.pallas_call(
        functools.partial(_combine_latent_kernel, S=S, L=L, n_ctx=n_ctx),
        grid=(B * L // tu,),
        in_specs=[pl.BlockSpec((tu,), lambda i: (i,), memory_space=pltpu.SMEM),
                  pl.BlockSpec(memory_space=pl.ANY),
                  pl.BlockSpec(memory_space=pl.ANY),
                  pl.BlockSpec(mod.shape, lambda i: (0, 0, 0, 0)),
                  pl.BlockSpec((1, D_MODEL), lambda i: (0, 0))],
        out_specs=pl.BlockSpec((tu, D_MODEL), lambda i: (i, 0)),
        out_shape=jax.ShapeDtypeStruct((B * L, D_MODEL), F32),
        scratch_shapes=[pltpu.VMEM((tu, D_MODEL), F32), pltpu.VMEM((tu, D_MODEL), F32),
                        pltpu.SemaphoreType.DMA(()), pltpu.SemaphoreType.DMA(())],
        compiler_params=_cparams(("arbitrary",)),
        name="moe_combine_latent",
    )(pos_lat, ys, xa, mod, final_w.reshape(1, -1))


def _flat_tile(T):
    return MOE_TILE if T % MOE_TILE == 0 else TOK_TILE


def _sorted_rows(T):
    return (-(-T // MOE_TILE) + N_CLASSES) * MOE_TILE


def _moe(rec, cls, rank, counts, xa, mod, w1, w3, w2, layer, final_w, sorted_init, B, S, n_ctx, last):
    T = B * S
    tm = MOE_TILE
    n_tiles_max = _sorted_rows(T) // tm
    cnt = counts[:N_CLASSES, 0].astype(jnp.int32)
    tiles = (cnt + tm - 1) // tm
    tile_end = jnp.cumsum(tiles)
    row0 = (tile_end - tiles) * tm
    cls = cls.reshape(T)
    pos = rank.reshape(T)
    for ci in range(N_CLASSES):
        pos = pos + jnp.where(cls == ci, row0[ci], 0)
    n_tiles = tile_end[N_CLASSES - 1:].astype(jnp.int32)
    tile_cls = jnp.sum(jnp.arange(n_tiles_max)[:, None] >= tile_end[None, :], axis=1)
    tile_cls = jnp.minimum(tile_cls, N_CLASSES - 1)
    grp0 = (tile_cls // PAIRS_PER_GROUP) * EXPERTS_PER_GROUP
    pair = tile_cls % PAIRS_PER_GROUP
    lo = sum(jnp.where(pair == p, PAIR_LO[p], 0) for p in range(PAIRS_PER_GROUP))
    hi = sum(jnp.where(pair == p, PAIR_HI[p], 0) for p in range(PAIRS_PER_GROUP))
    first_tile = jnp.sum(jnp.where(tile_cls[:, None] == jnp.arange(N_CLASSES)[None, :],
                                   (tile_end - tiles)[None, :], 0), axis=1)
    swap = (jnp.arange(n_tiles_max) - first_tile) % 2 == 1
    tile_experts = jnp.stack([grp0 + jnp.where(swap, hi, lo), grp0 + jnp.where(swap, lo, hi)],
                             axis=1).reshape(-1).astype(jnp.int32)
    xs = _dispatch(pos, rec.reshape(T, REC_W), sorted_init)
    ys = _experts(tile_experts, n_tiles, xs, w1, w3, w2, layer)
    if last:
        L = S - n_ctx
        out = _combine_latent(pos, ys, xa.reshape(T, D_MODEL), mod, final_w, B, S, L, n_ctx)
        return out.reshape(B, L, D_MODEL), xs
    out = _combine(pos, ys, xa.reshape(T, D_MODEL), mod, S, n_ctx)
    return out.reshape(B, S, D_MODEL), xs


def _rope_tables(L, n_ctx):
    t = jnp.arange(L)
    row = (t // GRID_W).astype(F32)
    col = (t % GRID_W).astype(F32)
    inv = ROPE_BASE ** (-jnp.arange(0, ROPE_AXIS_DIM, 2, dtype=F32) / ROPE_AXIS_DIM)
    ar = row[:, None] * inv
    ac = col[:, None] * inv
    cos_l = jnp.concatenate([jnp.cos(ar), jnp.cos(ar), jnp.cos(ac), jnp.cos(ac)], axis=1)
    sin_l = jnp.concatenate([-jnp.sin(ar), jnp.sin(ar), -jnp.sin(ac), jnp.sin(ac)], axis=1)
    cos_t = jnp.concatenate([jnp.ones((n_ctx, M_HEAD_DIM), F32), cos_l], axis=0)
    sin_t = jnp.concatenate([jnp.zeros((n_ctx, M_HEAD_DIM), F32), sin_l], axis=0)
    return cos_t, sin_t


def kernel(x, c, ctx, c_ctx, ada_w, ada_b, norm1_w, w_in, conv_w, conv_b, gate_b, mnorm_w, rpb, w_out,
           norm2_w, router_w, router_b, exp_w1, exp_w3, exp_w2, final_norm_w):
    B, L, _ = x.shape
    n_ctx = ctx.shape[1]
    depth = ada_w.shape[0]
    assert n_ctx == TOK_TILE and L % TOK_TILE == 0 and L % GRID_W == 0
    S = n_ctx + L
    rows = L // GRID_W
    assert rows >= NA_WROWS + 4 and rows % NA_QROWS == 0 and (S // 4) % HALO == 0

    cvec = jnp.concatenate([c, c_ctx[None], jnp.zeros((16 - B - 1, D_MODEL), F32)], axis=0)
    mods = _adaln(cvec, ada_w, ada_b)
    cos_t, sin_t = _rope_tables(L, n_ctx)
    router_wt = router_w.T
    typ, st, bias = _natten_tables(rpb.reshape((depth * N_HEADS,) + rpb.shape[2:]), rows, n_ctx)

    stream = (x, ctx)
    out = None
    sorted_buf = jnp.zeros((_sorted_rows(B * S), REC_W), F32)
    w1_all = exp_w1.reshape((depth * N_EXPERTS,) + exp_w1.shape[2:])
    w3_all = exp_w3.reshape((depth * N_EXPERTS,) + exp_w3.shape[2:])
    w2_all = exp_w2.reshape((depth * N_EXPERTS,) + exp_w2.shape[2:])
    for layer in range(depth):
        last = layer == depth - 1
        m6 = mods[layer].reshape(16, 6, D_MODEL)
        mod = jnp.stack([jnp.broadcast_to(m6[B], (B, 6, D_MODEL)), m6[:B]], axis=1)
        mq, mo, nq, mk, mv, nk, nv, g = _inproj(stream, mod, norm1_w[layer], w_in, layer, B, S)
        q, k = _prep(mq, mk, conv_w[layer], conv_b[layer], cos_t, sin_t, B, S)
        hf, hb = _mlstm(q, k, mv, g, gate_b[layer], B, S, n_ctx // CHUNK)
        n_out = _natten(nq, nk, nv, typ, st, bias, layer, B, S, n_ctx)
        xa, rec, cls, rank, counts = _outproj(stream, hf, hb, mo, n_out, mod, mnorm_w[layer], w_out, layer,
                                              norm2_w[layer], router_wt, router_b, B, S)
        out, sorted_buf = _moe(rec, cls, rank, counts, xa, mod, w1_all, w3_all, w2_all, layer,
                               final_norm_w, sorted_buf, B, S, n_ctx, last)
        stream = (out,)
    return out
```
